```python
import jax, jax.numpy as jnp
from jax import lax
import numpy as np

D_MODEL = 1024
BATCH = 8
SEQ = 2048
DEPTH = 1

N_META = 16
Q_BLOCK = 128
MLA_HEADS = 8
MLA_Q_RANK = 384
MLA_KV_RANK = 128
MLA_NOPE_DIM = 64
MLA_ROPE_DIM = 32
MLA_QK_DIM = MLA_NOPE_DIM + MLA_ROPE_DIM
MLA_V_DIM = 64
MLA_V_WIDTH = MLA_HEADS * MLA_V_DIM
ROPE_THETA = 10000.0
FOX_HEADS = 8
FOX_HEAD_DIM = 64
FOX_WIDTH = FOX_HEADS * FOX_HEAD_DIM
D_FF = 2816
CONV_WIDTH = 3
LN_EPS = 1e-5
RMS_EPS = 1e-6
DN_ALPHA = (2 * DEPTH) ** 0.25
DN_BETA = (8 * DEPTH) ** -0.25
NEG_INF = -1e30
IN_SPLITS = (MLA_Q_RANK, MLA_KV_RANK, MLA_ROPE_DIM, FOX_WIDTH, FOX_WIDTH, FOX_WIDTH, FOX_HEADS, 2 * D_MODEL)
IN_TOTAL = MLA_Q_RANK + MLA_KV_RANK + MLA_ROPE_DIM + 3 * FOX_WIDTH + FOX_HEADS + 2 * D_MODEL

kernel_name = 'hybrid_mla_fox_convglu_deepnorm_meta'


def layer_norm(x, g, b):
    xf = x.astype(jnp.float32)
    mu = jnp.mean(xf, axis=-1, keepdims=True)
    var = jnp.mean(jnp.square(xf - mu), axis=-1, keepdims=True)
    y = (xf - mu) * lax.rsqrt(var + LN_EPS)
    return (y * g.astype(jnp.float32) + b.astype(jnp.float32)).astype(x.dtype)


def rms_norm(x, g):
    xf = x.astype(jnp.float32)
    y = xf * lax.rsqrt(jnp.mean(jnp.square(xf), axis=-1, keepdims=True) + RMS_EPS)
    return (y * g.astype(jnp.float32)).astype(x.dtype)


def apply_rope(t, pos):
    half = t.shape[-1] // 2
    inv_freq = ROPE_THETA ** (-jnp.arange(half, dtype=jnp.float32) / half)
    ang = pos.astype(jnp.float32)[:, None] * inv_freq[None, :]
    cos = jnp.cos(ang).astype(t.dtype)
    sin = jnp.sin(ang).astype(t.dtype)
    t1, t2 = t[..., :half], t[..., half:]
    return jnp.concatenate([t1 * cos - t2 * sin, t2 * cos + t1 * sin], axis=-1)


def causal_block_attention(q, k, v, scale, cum_logf=None):
    B, H, L, dk = q.shape
    dv = v.shape[-1]
    pad = (-N_META) % Q_BLOCK
    Lp = L + pad
    nb = Lp // Q_BLOCK
    pad4 = lambda a: jnp.pad(a, ((0, 0), (0, 0), (pad, 0), (0, 0)))
    q, k, v = pad4(q), pad4(k), pad4(v)
    kpos = jnp.arange(Lp)
    key_valid = kpos >= pad
    qb = q.reshape(B, H, nb, Q_BLOCK, dk).transpose(2, 0, 1, 3, 4)
    idx = jnp.arange(nb)
    if cum_logf is None:
        c = None
        xs = (idx, qb)
    else:
        c = jnp.pad(cum_logf, ((0, 0), (0, 0), (pad, 0)))
        cb = c.reshape(B, H, nb, Q_BLOCK).transpose(2, 0, 1, 3)
        xs = (idx, qb, cb)

    def one_block(blk):
        i, q_i = blk[0], blk[1]
        s = jnp.einsum('bhqd,bhkd->bhqk', q_i, k).astype(jnp.float32) * scale
        if c is not None:
            s = s + (blk[2][..., :, None] - c[:, :, None, :])
        qpos = i * Q_BLOCK + jnp.arange(Q_BLOCK)
        mask = (kpos[None, :] <= qpos[:, None]) & key_valid[None, :]
        s = jnp.where(mask[None, None], s, NEG_INF)
        p = jax.nn.softmax(s, axis=-1)
        return jnp.einsum('bhqk,bhkd->bhqd', p.astype(v.dtype), v)

    ob = lax.map(one_block, xs)
    o = ob.transpose(1, 2, 0, 3, 4).reshape(B, H, Lp, dv)
    return o[:, :, pad:, :]


def hybrid_mixer(h, w_in, b_gate, b_forget, q_norm_g, w_q_up, kv_norm_g, w_kv_up,
                 w_branch_mla, w_branch_fox, w_out):
    B, L, _ = h.shape
    proj = h @ w_in
    offs = np.cumsum(IN_SPLITS)[:-1].tolist()
    q_lat, kv_lat, k_rope, fq, fk, fv, f_logit, gate_logit = jnp.split(proj, offs, axis=-1)
    pos = jnp.arange(L)

    q = (rms_norm(q_lat, q_norm_g) @ w_q_up).reshape(B, L, MLA_HEADS, MLA_QK_DIM).transpose(0, 2, 1, 3)
    q_nope, q_pe = q[..., :MLA_NOPE_DIM], q[..., MLA_NOPE_DIM:]
    kv = (rms_norm(kv_lat, kv_norm_g) @ w_kv_up).reshape(B, L, MLA_HEADS, MLA_NOPE_DIM + MLA_V_DIM).transpose(0, 2, 1, 3)
    k_nope, v_mla = kv[..., :MLA_NOPE_DIM], kv[..., MLA_NOPE_DIM:]
    q_pe = apply_rope(q_pe, pos)
    k_pe = apply_rope(k_rope[:, None], pos)
    q_mla = jnp.concatenate([q_nope, q_pe], axis=-1)
    k_mla = jnp.concatenate([k_nope, jnp.broadcast_to(k_pe, (B, MLA_HEADS, L, MLA_ROPE_DIM))], axis=-1)
    o_mla = causal_block_attention(q_mla, k_mla, v_mla, MLA_QK_DIM ** -0.5)
    o_mla = o_mla.transpose(0, 2, 1, 3).reshape(B, L, MLA_V_WIDTH)

    heads = lambda t: t.reshape(B, L, FOX_HEADS, FOX_HEAD_DIM).transpose(0, 2, 1, 3)
    log_f = jax.nn.log_sigmoid((f_logit + b_forget).astype(jnp.float32))
    cum = lax.cumsum(log_f, axis=1).transpose(0, 2, 1)
    o_fox = causal_block_attention(heads(fq), heads(fk), heads(fv), FOX_HEAD_DIM ** -0.5, cum)
    o_fox = o_fox.transpose(0, 2, 1, 3).reshape(B, L, FOX_WIDTH)

    gates = jax.nn.sigmoid(gate_logit + b_gate)
    g_mla, g_fox = gates[..., :D_MODEL], gates[..., D_MODEL:]
    merged = g_mla * (o_mla @ w_branch_mla) + g_fox * (o_fox @ w_branch_fox)
    return merged @ w_out


def causal_depthwise_conv(x, w, b):
    C = x.shape[-1]
    y = lax.conv_general_dilated(x, w[:, None, :].astype(x.dtype), window_strides=(1,),
                                 padding=[(CONV_WIDTH - 1, 0)],
                                 dimension_numbers=('NWC', 'WIO', 'NWC'),
                                 feature_group_count=C)
    return y + b


def conv_glu_ffn(h, w_up, conv_w, conv_b, w_down):
    up = h @ w_up
    gate, val = up[..., :D_FF], up[..., D_FF:]
    gate = causal_depthwise_conv(gate, conv_w, conv_b)
    return (jax.nn.silu(gate) * val) @ w_down


def setup_inputs(seed: int = 0) -> dict:
    key = jax.random.key(seed)
    ks = jax.random.split(key, 24)
    f32 = jnp.float32
    nrm = lambda k, shape, s: jax.random.normal(k, shape, f32) * s
    gain = lambda k, shape: 1.0 + 0.05 * jax.random.normal(k, shape, f32)
    return {
        'x': nrm(ks[0], (BATCH, SEQ, D_MODEL), 1.0),
        'meta_tokens': nrm(ks[1], (N_META, D_MODEL), 1.0),
        'ln_emb_g': gain(ks[2], (D_MODEL,)),
        'ln_emb_b': nrm(ks[3], (D_MODEL,), 0.02),
        'w_in': nrm(ks[4], (DEPTH, D_MODEL, IN_TOTAL), D_MODEL ** -0.5),
        'b_gate': nrm(ks[5], (DEPTH, 2 * D_MODEL), 0.02),
        'b_forget': jax.random.uniform(ks[6], (DEPTH, FOX_HEADS), f32, 2.0, 6.0),
        'q_norm_g': gain(ks[7], (DEPTH, MLA_Q_RANK)),
        'w_q_up': nrm(ks[8], (DEPTH, MLA_Q_RANK, MLA_HEADS * MLA_QK_DIM), MLA_Q_RANK ** -0.5),
        'kv_norm_g': gain(ks[9], (DEPTH, MLA_KV_RANK)),
        'w_kv_up': nrm(ks[10], (DEPTH, MLA_KV_RANK, MLA_HEADS * (MLA_NOPE_DIM + MLA_V_DIM)), MLA_KV_RANK ** -0.5),
        'w_branch_mla': nrm(ks[11], (DEPTH, MLA_V_WIDTH, D_MODEL), MLA_V_WIDTH ** -0.5 * DN_BETA),
        'w_branch_fox': nrm(ks[12], (DEPTH, FOX_WIDTH, D_MODEL), FOX_WIDTH ** -0.5 * DN_BETA),
        'w_out': nrm(ks[13], (DEPTH, D_MODEL, D_MODEL), D_MODEL ** -0.5 * DN_BETA),
        'ln_mix_g': gain(ks[14], (DEPTH, D_MODEL)),
        'ln_mix_b': nrm(ks[15], (DEPTH, D_MODEL), 0.02),
        'w_ffn_up': nrm(ks[16], (DEPTH, D_MODEL, 2 * D_FF), D_MODEL ** -0.5 * DN_BETA),
        'conv_w': nrm(ks[17], (DEPTH, CONV_WIDTH, D_FF), CONV_WIDTH ** -0.5),
        'conv_b': nrm(ks[18], (DEPTH, D_FF), 0.02),
        'w_ffn_down': nrm(ks[19], (DEPTH, D_FF, D_MODEL), D_FF ** -0.5 * DN_BETA),
        'ln_ffn_g': gain(ks[20], (DEPTH, D_MODEL)),
        'ln_ffn_b': nrm(ks[21], (DEPTH, D_MODEL), 0.02),
    }


def reference(x, meta_tokens, ln_emb_g, ln_emb_b, w_in, b_gate, b_forget, q_norm_g, w_q_up,
              kv_norm_g, w_kv_up, w_branch_mla, w_branch_fox, w_out, ln_mix_g, ln_mix_b,
              w_ffn_up, conv_w, conv_b, w_ffn_down, ln_ffn_g, ln_ffn_b):
    B = x.shape[0]
    meta = jnp.broadcast_to(meta_tokens[None].astype(x.dtype), (B, N_META, D_MODEL))
    h = layer_norm(jnp.concatenate([meta, x], axis=1), ln_emb_g, ln_emb_b)
    for l in range(DEPTH):
        m = hybrid_mixer(h, w_in[l], b_gate[l], b_forget[l], q_norm_g[l], w_q_up[l], kv_norm_g[l],
                         w_kv_up[l], w_branch_mla[l], w_branch_fox[l], w_out[l])
        h = layer_norm(DN_ALPHA * h + m, ln_mix_g[l], ln_mix_b[l])
        f = conv_glu_ffn(h, w_ffn_up[l], conv_w[l], conv_b[l], w_ffn_down[l])
        h = layer_norm(DN_ALPHA * h + f, ln_ffn_g[l], ln_ffn_b[l])
    return h[:, N_META:, :]
```

```python
import functools

import numpy as np
import jax
import jax.numpy as jnp
from jax import lax
from jax.experimental import pallas as pl
from jax.experimental.pallas import tpu as pltpu

D_MODEL = 1024
N_META = 16
MLA_HEADS = 8
MLA_Q_RANK = 384
MLA_KV_RANK = 128
MLA_NOPE_DIM = 64
MLA_ROPE_DIM = 32
MLA_QK_DIM = MLA_NOPE_DIM + MLA_ROPE_DIM
MLA_V_DIM = 64
ROPE_THETA = 10000.0
FOX_HEADS = 8
FOX_HEAD_DIM = 64
FOX_WIDTH = FOX_HEADS * FOX_HEAD_DIM
D_FF = 2816
LN_EPS = 1e-5
RMS_EPS = 1e-6
DEPTH = 1
DN_ALPHA = (2 * DEPTH) ** 0.25
NEG_INF = -1e30

LANES = 128
N_PAIRS = 8
PAIR_QK = 2 * LANES
PAIR_V = 2 * MLA_V_DIM
QK_WIDTH = N_PAIRS * PAIR_QK
V_WIDTH = N_PAIRS * PAIR_V
EXTRA_GROUP = 32
FF_CHUNK = 256
N_FF_CHUNKS = D_FF // FF_CHUNK
HALO_ROWS = 8
VMEM_LIMIT = 56 * 1024 * 1024

C_QLAT = 0
C_KVLAT = C_QLAT + MLA_Q_RANK
C_KROPE = C_KVLAT + MLA_KV_RANK
C_FQ = C_KROPE + LANES
C_FK = C_FQ + FOX_WIDTH
C_FV = C_FK + FOX_WIDTH
C_FLOGIT = C_FV + FOX_WIDTH
C_GATE = C_FLOGIT + LANES
C_TOTAL = C_GATE + 2 * D_MODEL

F32 = jnp.float32
BF16 = jnp.bfloat16


def _dot(a, b):
    return jnp.dot(a, b, preferred_element_type=F32)


def _layer_norm(x, g, b):
    mu = jnp.mean(x, axis=-1, keepdims=True)
    xc = x - mu
    var = jnp.mean(xc * xc, axis=-1, keepdims=True)
    return xc * lax.rsqrt(var + LN_EPS) * g + b


def _rms_norm(x, g):
    return x * lax.rsqrt(jnp.mean(x * x, axis=-1, keepdims=True) + RMS_EPS) * g


def _resident(shape):
    zeros = (0,) * len(shape)
    return pl.BlockSpec(shape, lambda *_: zeros, pipeline_mode=pl.Buffered(1))


def _params():
    return pltpu.CompilerParams(dimension_semantics=("arbitrary", "arbitrary"),
                                vmem_limit_bytes=VMEM_LIMIT)


def _rope(t, cos_t, sa_t, sb_t):
    return (t * cos_t + pltpu.roll(t, MLA_ROPE_DIM // 2, 1) * sa_t
            + pltpu.roll(t, LANES - MLA_ROPE_DIM // 2, 1) * sb_t)


def _cumsum_rows(x):
    rows = x.shape[0]
    row = lax.broadcasted_iota(jnp.int32, x.shape, 0)
    step = 1
    while step < rows:
        x = x + jnp.where(row >= step, pltpu.roll(x, step, 0), 0.0)
        step *= 2
    return x


def _proj_kernel(x_ref, lng_ref, lnb_ref, win_ref, bgate_ref, bforget_ref, qng_ref, wq_ref,
                 kvng_ref, wk_ref, wv_ref, cos_ref, sa_ref, sb_ref, selq_ref, cq_ref, selk_ref,
                 ck_ref, cum0_ref,
                 q_ref, k_ref, v_ref, g_ref, cumend_ref,
                 carry_ref):
    @pl.when(pl.program_id(1) == 0)
    def _():
        carry_ref[...] = cum0_ref[...]

    hb = _layer_norm(x_ref[0], lng_ref[...], lnb_ref[...]).astype(BF16)
    cos_t, sa_t, sb_t = cos_ref[...], sa_ref[...], sb_ref[...]
    mla_scale = MLA_QK_DIM ** -0.5
    fox_scale = FOX_HEAD_DIM ** -0.5

    q_lat = _dot(hb, win_ref[:, C_QLAT:C_QLAT + MLA_Q_RANK])
    qn = _rms_norm(q_lat, qng_ref[...]).astype(BF16)
    for p in range(N_PAIRS // 2):
        qp = _dot(qn, wq_ref[:, p * PAIR_QK:(p + 1) * PAIR_QK]) * mla_scale
        q_ref[0, :, p * PAIR_QK:p * PAIR_QK + LANES] = qp[:, :LANES].astype(BF16)
        q_ref[0, :, p * PAIR_QK + LANES:(p + 1) * PAIR_QK] = _rope(
            qp[:, LANES:], cos_t, sa_t, sb_t).astype(BF16)

    kv_lat = _dot(hb, win_ref[:, C_KVLAT:C_KVLAT + MLA_KV_RANK])
    kvn = _rms_norm(kv_lat, kvng_ref[...]).astype(BF16)
    k_pe = _rope(_dot(hb, win_ref[:, C_KROPE:C_KROPE + LANES]), cos_t, sa_t, sb_t).astype(BF16)
    for p in range(N_PAIRS // 2):
        k_ref[0, :, p * PAIR_QK:p * PAIR_QK + LANES] = _dot(
            kvn, wk_ref[:, p * LANES:(p + 1) * LANES]).astype(BF16)
        k_ref[0, :, p * PAIR_QK + LANES:(p + 1) * PAIR_QK] = k_pe
    v_ref[0, :, :V_WIDTH // 2] = _dot(kvn, wv_ref[...]).astype(BF16)

    z = _dot(hb, win_ref[:, C_FLOGIT:C_FLOGIT + LANES]) + bforget_ref[...]
    log_f = -(jnp.maximum(-z, 0.0) + jnp.log1p(jnp.exp(-jnp.abs(z))))
    cum = _cumsum_rows(log_f) + carry_ref[0:1, :]
    rows = cum.shape[0]
    carry_ref[...] = jnp.broadcast_to(cum[rows - 1:rows, :], carry_ref.shape)
    cumend_ref[0] = jnp.broadcast_to(cum[rows - 1:rows, :], carry_ref.shape)
    c_hi = cum.astype(BF16)
    r1 = cum - c_hi.astype(F32)
    c_mid = r1.astype(BF16)
    c_lo = (r1 - c_mid.astype(F32)).astype(BF16)
    aug_q = (_dot(c_hi, selq_ref[0]) + _dot(c_mid, selq_ref[1]) + _dot(c_lo, selq_ref[2])
             + cq_ref[...]).astype(BF16)
    aug_k = (_dot(c_hi, selk_ref[0]) + _dot(c_mid, selk_ref[1]) + _dot(c_lo, selk_ref[2])
             + ck_ref[...]).astype(BF16)
    fq = (_dot(hb, win_ref[:, C_FQ:C_FQ + FOX_WIDTH]) * fox_scale).astype(BF16)
    fk = _dot(hb, win_ref[:, C_FK:C_FK + FOX_WIDTH]).astype(BF16)
    base = (N_PAIRS // 2) * PAIR_QK
    for p in range(N_PAIRS // 2):
        lo_col = base + p * PAIR_QK
        q_ref[0, :, lo_col:lo_col + LANES] = fq[:, p * LANES:(p + 1) * LANES]
        q_ref[0, :, lo_col + LANES:lo_col + PAIR_QK] = aug_q[:, p * LANES:(p + 1) * LANES]
        k_ref[0, :, lo_col:lo_col + LANES] = fk[:, p * LANES:(p + 1) * LANES]
        k_ref[0, :, lo_col + LANES:lo_col + PAIR_QK] = aug_k[:, p * LANES:(p + 1) * LANES]
    v_ref[0, :, V_WIDTH // 2:] = _dot(hb, win_ref[:, C_FV:C_FV + FOX_WIDTH]).astype(BF16)

    gate_chunk = 512
    for c in range(2 * D_MODEL // gate_chunk):
        sl = slice(c * gate_chunk, (c + 1) * gate_chunk)
        zg = _dot(hb, win_ref[:, C_GATE + c * gate_chunk:C_GATE + (c + 1) * gate_chunk])
        zg = zg + bgate_ref[:, sl]
        g_ref[0, :, sl] = (1.0 / (1.0 + jnp.exp(-zg))).astype(BF16)


def _proj_call(x3, tabs, cum0, w, tm):
    nb, seq, _ = x3.shape
    row_blk = lambda width: pl.BlockSpec((1, tm, width), lambda b, j: (b, j, 0))
    tab_blk = pl.BlockSpec((tm, LANES), lambda b, j: (j, 0))
    consts = [w["ln_emb_g"], w["ln_emb_b"], w["w_in"], w["b_gate"], w["b_forget"], w["q_norm_g"],
              w["w_q"], w["kv_norm_g"], w["w_k"], w["w_v"]]
    sels = [w["sel_q"], w["const_q"], w["sel_k"], w["const_k"], cum0]
    in_specs = ([row_blk(D_MODEL)] + [_resident(a.shape) for a in consts] + [tab_blk] * 3
                + [_resident(a.shape) for a in sels])
    out_shape = (jax.ShapeDtypeStruct((nb, seq, QK_WIDTH), BF16),
                 jax.ShapeDtypeStruct((nb, seq, QK_WIDTH), BF16),
                 jax.ShapeDtypeStruct((nb, seq, V_WIDTH), BF16),
                 jax.ShapeDtypeStruct((nb, seq, 2 * D_MODEL), BF16),
                 jax.ShapeDtypeStruct((nb, HALO_ROWS, LANES), F32))
    out_specs = (row_blk(QK_WIDTH), row_blk(QK_WIDTH), row_blk(V_WIDTH), row_blk(2 * D_MODEL),
                 pl.BlockSpec((1, HALO_ROWS, LANES), lambda b, j: (b, 0, 0)))
    return pl.pallas_call(
        _proj_kernel, grid=(nb, seq // tm), in_specs=in_specs, out_specs=out_specs,
        out_shape=out_shape, scratch_shapes=[pltpu.VMEM((HALO_ROWS, LANES), F32)],
        compiler_params=_params(), name="proj",
    )(x3, *consts, *tabs, *sels)


def _attn_kernel(*refs, tq, n_prefix):
    if n_prefix:
        q_ref, k_ref, v_ref, kp_ref, vp_ref, o_ref, m_ref, l_ref, acc_ref = refs
    else:
        q_ref, k_ref, v_ref, o_ref, m_ref, l_ref, acc_ref = refs
    i = pl.program_id(2)
    q = q_ref[0]
    lane = lax.broadcasted_iota(jnp.int32, (1, PAIR_QK), 1)
    head_a = (lane < MLA_NOPE_DIM) | ((lane >= LANES) & (lane < LANES + EXTRA_GROUP))
    head_b = (((lane >= MLA_NOPE_DIM) & (lane < LANES))
              | ((lane >= LANES + EXTRA_GROUP) & (lane < LANES + 2 * EXTRA_GROUP)))
    zero = jnp.zeros_like(q)
    qs = (jnp.where(head_a, q, zero), jnp.where(head_b, q, zero))

    def scores(qh, kt):
        return lax.dot_general(qh, kt, (((1,), (1,)), ((), ())), preferred_element_type=F32)

    def first(h, s, vt):
        m = jnp.max(s, axis=1, keepdims=True)
        p = jnp.exp(s - m)
        m_ref[h] = m
        l_ref[h] = jnp.sum(p, axis=1, keepdims=True)
        acc_ref[h] = _dot(p.astype(BF16), vt)

    def update(h, s, vt):
        m_prev = m_ref[h]
        m_new = jnp.maximum(m_prev, jnp.max(s, axis=1, keepdims=True))
        alpha = jnp.exp(m_prev - m_new)
        p = jnp.exp(s - m_new)
        m_ref[h] = m_new
        l_ref[h] = alpha * l_ref[h] + jnp.sum(p, axis=1, keepdims=True)
        acc_ref[h] = alpha * acc_ref[h] + _dot(p.astype(BF16), vt)

    row = lax.broadcasted_iota(jnp.int32, (tq, tq), 0)
    col = lax.broadcasted_iota(jnp.int32, (tq, tq), 1)
    causal = col <= row
    diag = pl.ds(pl.multiple_of(i * tq, tq), tq)

    if n_prefix:
        for h in range(2):
            first(h, scores(qs[h], kp_ref[...]), vp_ref[...])
    else:
        for h in range(2):
            first(h, jnp.where(causal, scores(qs[h], k_ref[0, diag, :]), NEG_INF),
                  v_ref[0, diag, :])

    def body(j, carry):
        sl = pl.ds(pl.multiple_of(j * tq, tq), tq)
        kt = k_ref[0, sl, :]
        vt = v_ref[0, sl, :]
        for h in range(2):
            update(h, scores(qs[h], kt), vt)
        return carry

    lax.fori_loop(0, i, body, 0)

    if n_prefix:
        for h in range(2):
            update(h, jnp.where(causal, scores(qs[h], k_ref[0, diag, :]), NEG_INF),
                   v_ref[0, diag, :])

    out_lane = lax.broadcasted_iota(jnp.int32, (1, PAIR_V), 1)
    o = jnp.where(out_lane < MLA_V_DIM, acc_ref[0] / l_ref[0], acc_ref[1] / l_ref[1])
    o_ref[0] = o.astype(BF16)


def _attn_call(q, k, v, prefix, tq):
    nb, seq, _ = q.shape
    n_prefix = 0 if prefix is None else prefix[0].shape[0]
    in_specs = [pl.BlockSpec((1, tq, PAIR_QK), lambda b, p, i: (b, i, p)),
                pl.BlockSpec((1, seq, PAIR_QK), lambda b, p, i: (b, 0, p)),
                pl.BlockSpec((1, seq, PAIR_V), lambda b, p, i: (b, 0, p))]
    args = [q, k, v]
    if n_prefix:
        in_specs += [pl.BlockSpec((n_prefix, PAIR_QK), lambda b, p, i: (0, p)),
                     pl.BlockSpec((n_prefix, PAIR_V), lambda b, p, i: (0, p))]
        args += list(prefix)
    return pl.pallas_call(
        functools.partial(_attn_kernel, tq=tq, n_prefix=n_prefix),
        grid=(nb, N_PAIRS, seq // tq), in_specs=in_specs,
        out_specs=pl.BlockSpec((1, tq, PAIR_V), lambda b, p, i: (b, i, p)),
        out_shape=jax.ShapeDtypeStruct((nb, seq, V_WIDTH), BF16),
        scratch_shapes=[pltpu.VMEM((2, tq, 1), F32), pltpu.VMEM((2, tq, 1), F32),
                        pltpu.VMEM((2, tq, PAIR_V), F32)],
        compiler_params=pltpu.CompilerParams(
            dimension_semantics=("arbitrary", "arbitrary", "arbitrary"),
            vmem_limit_bytes=VMEM_LIMIT),
        name="attn",
    )(*args)


def _merge_kernel(x_ref, o_ref, g_ref, lng_ref, lnb_ref, wbm_ref, wbf_ref, wout_ref, mixg_ref,
                  mixb_ref, h_ref):
    half = V_WIDTH // 2
    merged = (g_ref[0, :, :D_MODEL].astype(F32) * _dot(o_ref[0, :, :half], wbm_ref[...])
              + g_ref[0, :, D_MODEL:].astype(F32) * _dot(o_ref[0, :, half:], wbf_ref[...]))
    mixed = _dot(merged.astype(BF16), wout_ref[...])
    h0 = _layer_norm(x_ref[0], lng_ref[...], lnb_ref[...])
    h_ref[0] = _layer_norm(DN_ALPHA * h0 + mixed, mixg_ref[...], mixb_ref[...])


def _merge_call(x3, o, g, w, tm):
    nb, seq, _ = x3.shape
    row_blk = lambda width: pl.BlockSpec((1, tm, width), lambda b, j: (b, j, 0))
    consts = [w["ln_emb_g"], w["ln_emb_b"], w["w_bm"], w["w_bf"], w["w_out"], w["ln_mix_g"],
              w["ln_mix_b"]]
    return pl.pallas_call(
        _merge_kernel, grid=(nb, seq // tm),
        in_specs=[row_blk(D_MODEL), row_blk(V_WIDTH), row_blk(2 * D_MODEL)]
        + [_resident(a.shape) for a in consts],
        out_specs=row_blk(D_MODEL), out_shape=jax.ShapeDtypeStruct((nb, seq, D_MODEL), F32),
        compiler_params=_params(), name="merge",
    )(x3, o, g, *consts)


def _ffn_kernel(h_ref, halo0_ref, wup_ref, cw_ref, cb_ref, wdown_ref, lng_ref, lnb_ref,
                out_ref, tail_ref, halo_ref):
    @pl.when(pl.program_id(1) == 0)
    def _():
        halo_ref[...] = halo0_ref[...]

    h = h_ref[0]
    hb = h.astype(BF16)
    rows = h.shape[0]
    row = lax.broadcasted_iota(jnp.int32, (rows, FF_CHUNK), 0)
    acc = jnp.zeros((rows, D_MODEL), F32)
    for c in range(N_FF_CHUNKS):
        sl = slice(c * FF_CHUNK, (c + 1) * FF_CHUNK)
        gate = _dot(hb, wup_ref[:, sl])
        val = _dot(hb, wup_ref[:, D_FF + c * FF_CHUNK:D_FF + (c + 1) * FF_CHUNK])
        prev = halo_ref[:, sl]
        back1 = prev[HALO_ROWS - 1:HALO_ROWS, :]
        back2 = prev[HALO_ROWS - 2:HALO_ROWS - 1, :]
        g1 = jnp.where(row == 0, back1, pltpu.roll(gate, 1, 0))
        g2 = jnp.where(row == 0, back2, jnp.where(row == 1, back1, pltpu.roll(gate, 2, 0)))
        halo_ref[:, sl] = gate[rows - HALO_ROWS:, :]
        conv = (cw_ref[0:1, sl] * g2 + cw_ref[1:2, sl] * g1 + cw_ref[2:3, sl] * gate
                + cb_ref[:, sl])
        act = conv * (1.0 / (1.0 + jnp.exp(-conv))) * val
        acc = acc + _dot(act.astype(BF16), wdown_ref[sl, :])
    out_ref[0] = _layer_norm(DN_ALPHA * h + acc, lng_ref[...], lnb_ref[...])
    tail_ref[0] = halo_ref[...]


def _ffn_call(h1, halo0, w, tm):
    nb, seq, _ = h1.shape
    row_blk = pl.BlockSpec((1, tm, D_MODEL), lambda b, j: (b, j, 0))
    consts = [halo0, w["w_up"], w["conv_w"], w["conv_b"], w["w_down"], w["ln_ffn_g"],
              w["ln_ffn_b"]]
    return pl.pallas_call(
        _ffn_kernel, grid=(nb, seq // tm),
        in_specs=[row_blk] + [_resident(a.shape) for a in consts],
        out_specs=(row_blk, pl.BlockSpec((1, HALO_ROWS, D_FF), lambda b, j: (b, 0, 0))),
        out_shape=(jax.ShapeDtypeStruct((nb, seq, D_MODEL), F32),
                   jax.ShapeDtypeStruct((nb, HALO_ROWS, D_FF), F32)),
        scratch_shapes=[pltpu.VMEM((HALO_ROWS, D_FF), F32)],
        compiler_params=_params(), name="ffn",
    )(h1, *consts)


def _rope_tables(length):
    half = MLA_ROPE_DIM // 2
    inv_freq = ROPE_THETA ** (-jnp.arange(half, dtype=F32) / half)
    ang = jnp.arange(length).astype(F32)[:, None] * inv_freq[None, :]
    cos, sin = jnp.cos(ang), jnp.sin(ang)
    zero = jnp.zeros_like(cos)
    widen = lambda a, b: jnp.pad(jnp.tile(jnp.concatenate([a, b], axis=1), (1, 2)),
                                 ((0, 0), (0, LANES - 2 * MLA_ROPE_DIM)))
    return widen(cos, cos), widen(zero, sin), widen(-sin, zero)


def _bias_selectors():
    width = (N_PAIRS // 2) * LANES
    sel_q = np.zeros((3, LANES, width), np.float32)
    sel_k = np.zeros((3, LANES, width), np.float32)
    const_q = np.zeros((1, width), np.float32)
    const_k = np.zeros((1, width), np.float32)
    for h in range(FOX_HEADS):
        base = LANES * (h // 2) + EXTRA_GROUP * (h % 2)
        for piece in range(3):
            sel_q[piece, h, base + piece] = 1.0
            const_q[0, base + 3 + piece] = 1.0
            const_k[0, base + piece] = 1.0
            sel_k[piece, h, base + 3 + piece] = -1.0
    return (jnp.asarray(sel_q, BF16), jnp.asarray(const_q), jnp.asarray(sel_k, BF16),
            jnp.asarray(const_k))


def _prepare_weights(w_in, b_gate, b_forget, q_norm_g, w_q_up, kv_norm_g, w_kv_up, w_branch_mla,
                     w_branch_fox, w_out, w_ffn_up, conv_w, conv_b, w_ffn_down):
    n_half = N_PAIRS // 2
    o_qlat, o_kvlat, o_krope = 0, MLA_Q_RANK, MLA_Q_RANK + MLA_KV_RANK
    o_fq = o_krope + MLA_ROPE_DIM
    o_fk, o_fv = o_fq + FOX_WIDTH, o_fq + 2 * FOX_WIDTH
    o_flogit = o_fv + FOX_WIDTH
    o_gate = o_flogit + FOX_HEADS
    k_rope = w_in[:, o_krope:o_fq]
    w_in_p = jnp.concatenate([
        w_in[:, o_qlat:o_krope],
        k_rope, k_rope, jnp.zeros((D_MODEL, LANES - 2 * MLA_ROPE_DIM), F32),
        w_in[:, o_fq:o_flogit],
        w_in[:, o_flogit:o_gate], jnp.zeros((D_MODEL, LANES - FOX_HEADS), F32),
        w_in[:, o_gate:],
    ], axis=1).astype(BF16)
    wq = w_q_up.reshape(MLA_Q_RANK, n_half, 2, MLA_QK_DIM)
    wq_p = jnp.concatenate([
        wq[..., :MLA_NOPE_DIM].reshape(MLA_Q_RANK, n_half, LANES),
        wq[..., MLA_NOPE_DIM:].reshape(MLA_Q_RANK, n_half, 2 * MLA_ROPE_DIM),
        jnp.zeros((MLA_Q_RANK, n_half, LANES - 2 * MLA_ROPE_DIM), F32)], axis=-1)
    wkv = w_kv_up.reshape(MLA_KV_RANK, MLA_HEADS, MLA_NOPE_DIM + MLA_V_DIM)
    return {
        "w_in": w_in_p,
        "b_gate": b_gate.reshape(1, -1),
        "b_forget": jnp.pad(b_forget.reshape(1, -1), ((0, 0), (0, LANES - FOX_HEADS))),
        "q_norm_g": q_norm_g.reshape(1, -1),
        "w_q": wq_p.reshape(MLA_Q_RANK, n_half * PAIR_QK).astype(BF16),
        "kv_norm_g": kv_norm_g.reshape(1, -1),
        "w_k": wkv[..., :MLA_NOPE_DIM].reshape(MLA_KV_RANK, -1).astype(BF16),
        "w_v": wkv[..., MLA_NOPE_DIM:].reshape(MLA_KV_RANK, -1).astype(BF16),
        "w_bm": w_branch_mla.astype(BF16),
        "w_bf": w_branch_fox.astype(BF16),
        "w_out": w_out.astype(BF16),
        "w_up": w_ffn_up.astype(BF16),
        "conv_w": conv_w,
        "conv_b": conv_b.reshape(1, -1),
        "w_down": w_ffn_down.astype(BF16),
    }


def _layer(x3, tabs, cum0, halo0, prefix, w, tm, tq):
    q, k, v, gates, cum_end = _proj_call(x3, tabs, cum0, w, tm)
    o = _attn_call(q, k, v, prefix, tq)
    h1 = _merge_call(x3, o, gates, w, tm)
    out, tail = _ffn_call(h1, halo0, w, tm)
    return out, (k[0], v[0]), cum_end[0], tail[0]


def kernel(x, meta_tokens, ln_emb_g, ln_emb_b, w_in, b_gate, b_forget, q_norm_g, w_q_up, kv_norm_g, w_kv_up, w_branch_mla, w_branch_fox, w_out, ln_mix_g, ln_mix_b, w_ffn_up, conv_w, conv_b, w_ffn_down, ln_ffn_g, ln_ffn_b):
    seq = x.shape[1]
    w = _prepare_weights(w_in[0], b_gate[0], b_forget[0], q_norm_g[0], w_q_up[0], kv_norm_g[0],
                         w_kv_up[0], w_branch_mla[0], w_branch_fox[0], w_out[0], w_ffn_up[0],
                         conv_w[0], conv_b[0], w_ffn_down[0])
    w.update(ln_emb_g=ln_emb_g.reshape(1, -1), ln_emb_b=ln_emb_b.reshape(1, -1),
             ln_mix_g=ln_mix_g[0].reshape(1, -1), ln_mix_b=ln_mix_b[0].reshape(1, -1),
             ln_ffn_g=ln_ffn_g[0].reshape(1, -1), ln_ffn_b=ln_ffn_b[0].reshape(1, -1))
    w["sel_q"], w["const_q"], w["sel_k"], w["const_k"] = _bias_selectors()
    tabs = _rope_tables(N_META + seq)
    meta_tabs = tuple(t[:N_META] for t in tabs)
    real_tabs = tuple(t[N_META:] for t in tabs)

    _, meta_kv, meta_cum, meta_tail = _layer(
        meta_tokens[None].astype(x.dtype), meta_tabs, jnp.zeros((HALO_ROWS, LANES), F32),
        jnp.zeros((HALO_ROWS, D_FF), F32), None, w, N_META, N_META)
    out, _, _, _ = _layer(x, real_tabs, meta_cum, meta_tail, meta_kv, w, 512, 512)
    return out
```

```python
import functools

import numpy as np
import jax
import jax.numpy as jnp
from jax import lax
from jax.experimental import pallas as pl
from jax.experimental.pallas import tpu as pltpu

D_MODEL = 1024
N_META = 16
MLA_HEADS = 8
MLA_Q_RANK = 384
MLA_KV_RANK = 128
MLA_NOPE_DIM = 64
MLA_ROPE_DIM = 32
MLA_QK_DIM = MLA_NOPE_DIM + MLA_ROPE_DIM
MLA_V_DIM = 64
ROPE_THETA = 10000.0
FOX_HEADS = 8
FOX_HEAD_DIM = 64
FOX_WIDTH = FOX_HEADS * FOX_HEAD_DIM
D_FF = 2816
LN_EPS = 1e-5
RMS_EPS = 1e-6
DEPTH = 1
DN_ALPHA = (2 * DEPTH) ** 0.25
NEG_INF = -1e30
LOG2E = 1.4426950408889634

LANES = 128
N_PAIRS = 8
PAIR_QK = 2 * LANES
PAIR_V = 2 * MLA_V_DIM
QK_WIDTH = N_PAIRS * PAIR_QK
V_WIDTH = N_PAIRS * PAIR_V
EXTRA_GROUP = 32
FF_CHUNK = 256
N_FF_CHUNKS = D_FF // FF_CHUNK
HALO_ROWS = 8
VMEM_LIMIT = 56 * 1024 * 1024

C_QLAT = 0
C_KVLAT = C_QLAT + MLA_Q_RANK
C_KROPE = C_KVLAT + MLA_KV_RANK
C_FQ = C_KROPE + LANES
C_FK = C_FQ + FOX_WIDTH
C_FV = C_FK + FOX_WIDTH
C_FLOGIT = C_FV + FOX_WIDTH
C_GATE = C_FLOGIT + LANES
C_TOTAL = C_GATE + 2 * D_MODEL

F32 = jnp.float32
BF16 = jnp.bfloat16


def _dot(a, b):
    return jnp.dot(a, b, preferred_element_type=F32)


def _layer_norm(x, g, b):
    mu = jnp.mean(x, axis=-1, keepdims=True)
    xc = x - mu
    var = jnp.mean(xc * xc, axis=-1, keepdims=True)
    return xc * lax.rsqrt(var + LN_EPS) * g + b


def _rms_norm(x, g):
    return x * lax.rsqrt(jnp.mean(x * x, axis=-1, keepdims=True) + RMS_EPS) * g


def _resident(shape):
    zeros = (0,) * len(shape)
    return pl.BlockSpec(shape, lambda *_: zeros, pipeline_mode=pl.Buffered(1))


def _params():
    return pltpu.CompilerParams(dimension_semantics=("arbitrary", "arbitrary"),
                                vmem_limit_bytes=VMEM_LIMIT)


def _rope(t, cos_t, sa_t, sb_t):
    return (t * cos_t + pltpu.roll(t, MLA_ROPE_DIM // 2, 1) * sa_t
            + pltpu.roll(t, LANES - MLA_ROPE_DIM // 2, 1) * sb_t)


def _cumsum_rows(x):
    rows = x.shape[0]
    row = lax.broadcasted_iota(jnp.int32, x.shape, 0)
    step = 1
    while step < rows:
        x = x + jnp.where(row >= step, pltpu.roll(x, step, 0), 0.0)
        step *= 2
    return x


def _proj_kernel(x_ref, lng_ref, lnb_ref, win_ref, bgate_ref, bforget_ref, qng_ref, wq_ref,
                 kvng_ref, wk_ref, wv_ref, cos_ref, sa_ref, sb_ref, selq_ref, cq_ref, selk_ref,
                 ck_ref, cum0_ref,
                 q_ref, k_ref, v_ref, g_ref, cumend_ref,
                 carry_ref):
    @pl.when(pl.program_id(1) == 0)
    def _():
        carry_ref[...] = cum0_ref[...]

    hb = _layer_norm(x_ref[0], lng_ref[...], lnb_ref[...]).astype(BF16)
    cos_t, sa_t, sb_t = cos_ref[...], sa_ref[...], sb_ref[...]
    mla_scale = MLA_QK_DIM ** -0.5 * LOG2E
    fox_scale = FOX_HEAD_DIM ** -0.5 * LOG2E

    q_lat = _dot(hb, win_ref[:, C_QLAT:C_QLAT + MLA_Q_RANK])
    qn = _rms_norm(q_lat, qng_ref[...]).astype(BF16)
    for p in range(N_PAIRS // 2):
        qp = _dot(qn, wq_ref[:, p * PAIR_QK:(p + 1) * PAIR_QK]) * mla_scale
        q_ref[0, :, p * PAIR_QK:p * PAIR_QK + LANES] = qp[:, :LANES].astype(BF16)
        q_ref[0, :, p * PAIR_QK + LANES:(p + 1) * PAIR_QK] = _rope(
            qp[:, LANES:], cos_t, sa_t, sb_t).astype(BF16)

    kv_lat = _dot(hb, win_ref[:, C_KVLAT:C_KVLAT + MLA_KV_RANK])
    kvn = _rms_norm(kv_lat, kvng_ref[...]).astype(BF16)
    k_pe = _rope(_dot(hb, win_ref[:, C_KROPE:C_KROPE + LANES]), cos_t, sa_t, sb_t).astype(BF16)
    for p in range(N_PAIRS // 2):
        k_ref[0, :, p * PAIR_QK:p * PAIR_QK + LANES] = _dot(
            kvn, wk_ref[:, p * LANES:(p + 1) * LANES]).astype(BF16)
        k_ref[0, :, p * PAIR_QK + LANES:(p + 1) * PAIR_QK] = k_pe
    v_ref[0, :, :V_WIDTH // 2] = _dot(kvn, wv_ref[...]).astype(BF16)

    z = _dot(hb, win_ref[:, C_FLOGIT:C_FLOGIT + LANES]) + bforget_ref[...]
    log_f = -(jnp.maximum(-z, 0.0) + jnp.log1p(jnp.exp(-jnp.abs(z))))
    cum = _cumsum_rows(log_f) + carry_ref[0:1, :]
    rows = cum.shape[0]
    carry_ref[...] = jnp.broadcast_to(cum[rows - 1:rows, :], carry_ref.shape)
    cumend_ref[0] = jnp.broadcast_to(cum[rows - 1:rows, :], carry_ref.shape)
    cum = cum * LOG2E
    c_hi = cum.astype(BF16)
    r1 = cum - c_hi.astype(F32)
    c_mid = r1.astype(BF16)
    c_lo = (r1 - c_mid.astype(F32)).astype(BF16)
    aug_q = (_dot(c_hi, selq_ref[0]) + _dot(c_mid, selq_ref[1]) + _dot(c_lo, selq_ref[2])
             + cq_ref[...]).astype(BF16)
    aug_k = (_dot(c_hi, selk_ref[0]) + _dot(c_mid, selk_ref[1]) + _dot(c_lo, selk_ref[2])
             + ck_ref[...]).astype(BF16)
    fq = (_dot(hb, win_ref[:, C_FQ:C_FQ + FOX_WIDTH]) * fox_scale).astype(BF16)
    fk = _dot(hb, win_ref[:, C_FK:C_FK + FOX_WIDTH]).astype(BF16)
    base = (N_PAIRS // 2) * PAIR_QK
    for p in range(N_PAIRS // 2):
        lo_col = base + p * PAIR_QK
        q_ref[0, :, lo_col:lo_col + LANES] = fq[:, p * LANES:(p + 1) * LANES]
        q_ref[0, :, lo_col + LANES:lo_col + PAIR_QK] = aug_q[:, p * LANES:(p + 1) * LANES]
        k_ref[0, :, lo_col:lo_col + LANES] = fk[:, p * LANES:(p + 1) * LANES]
        k_ref[0, :, lo_col + LANES:lo_col + PAIR_QK] = aug_k[:, p * LANES:(p + 1) * LANES]
    v_ref[0, :, V_WIDTH // 2:] = _dot(hb, win_ref[:, C_FV:C_FV + FOX_WIDTH]).astype(BF16)

    gate_chunk = 512
    for c in range(2 * D_MODEL // gate_chunk):
        sl = slice(c * gate_chunk, (c + 1) * gate_chunk)
        zg = _dot(hb, win_ref[:, C_GATE + c * gate_chunk:C_GATE + (c + 1) * gate_chunk])
        zg = zg + bgate_ref[:, sl]
        g_ref[0, :, sl] = (1.0 / (1.0 + jnp.exp(-zg))).astype(BF16)


def _proj_call(x3, tabs, cum0, w, tm):
    nb, seq, _ = x3.shape
    row_blk = lambda width: pl.BlockSpec((1, tm, width), lambda b, j: (b, j, 0))
    tab_blk = pl.BlockSpec((tm, LANES), lambda b, j: (j, 0))
    consts = [w["ln_emb_g"], w["ln_emb_b"], w["w_in"], w["b_gate"], w["b_forget"], w["q_norm_g"],
              w["w_q"], w["kv_norm_g"], w["w_k"], w["w_v"]]
    sels = [w["sel_q"], w["const_q"], w["sel_k"], w["const_k"], cum0]
    in_specs = ([row_blk(D_MODEL)] + [_resident(a.shape) for a in consts] + [tab_blk] * 3
                + [_resident(a.shape) for a in sels])
    out_shape = (jax.ShapeDtypeStruct((nb, seq, QK_WIDTH), BF16),
                 jax.ShapeDtypeStruct((nb, seq, QK_WIDTH), BF16),
                 jax.ShapeDtypeStruct((nb, seq, V_WIDTH), BF16),
                 jax.ShapeDtypeStruct((nb, seq, 2 * D_MODEL), BF16),
                 jax.ShapeDtypeStruct((nb, HALO_ROWS, LANES), F32))
    out_specs = (row_blk(QK_WIDTH), row_blk(QK_WIDTH), row_blk(V_WIDTH), row_blk(2 * D_MODEL),
                 pl.BlockSpec((1, HALO_ROWS, LANES), lambda b, j: (b, 0, 0)))
    return pl.pallas_call(
        _proj_kernel, grid=(nb, seq // tm), in_specs=in_specs, out_specs=out_specs,
        out_shape=out_shape, scratch_shapes=[pltpu.VMEM((HALO_ROWS, LANES), F32)],
        compiler_params=_params(), name="proj",
    )(x3, *consts, *tabs, *sels)


def _attn_kernel(*refs, tq, tk, rb, n_prefix):
    if n_prefix:
        q_ref, k_ref, v_ref, kp_ref, vp_ref, o_ref, m_ref, l_ref, acc_ref = refs
    else:
        q_ref, k_ref, v_ref, o_ref, m_ref, l_ref, acc_ref = refs
    i = pl.program_id(2)
    q = q_ref[0]
    lane = lax.broadcasted_iota(jnp.int32, (1, PAIR_QK), 1)
    head_a = (lane < MLA_NOPE_DIM) | ((lane >= LANES) & (lane < LANES + EXTRA_GROUP))
    head_b = (((lane >= MLA_NOPE_DIM) & (lane < LANES))
              | ((lane >= LANES + EXTRA_GROUP) & (lane < LANES + 2 * EXTRA_GROUP)))
    zero = jnp.zeros_like(q)
    qs = (jnp.where(head_a, q, zero), jnp.where(head_b, q, zero))

    m_ref[...] = jnp.full(m_ref.shape, NEG_INF, F32)
    l_ref[...] = jnp.zeros(l_ref.shape, F32)
    acc_ref[...] = jnp.zeros(acc_ref.shape, F32)
    first_lane = lax.broadcasted_iota(jnp.int32, (1, LANES), 1) == 0

    def update(h, kt, vt, visible, row0):
        width = kt.shape[0]
        tiles = width // LANES
        s_all = lax.dot_general(qs[h][row0:, :], kt, (((1,), (1,)), ((), ())),
                                preferred_element_type=F32)
        for r0 in range(row0, tq, rb):
            mask = visible(r0, rb, width)
            if mask is False:
                continue
            rows = slice(r0, r0 + rb)
            s = s_all[r0 - row0:r0 - row0 + rb, :]
            if mask is not None:
                s = jnp.where(mask, s, NEG_INF)
            m_prev = m_ref[h, rows, :]
            if tiles:
                m_cur = s[:, :LANES]
                for t in range(1, tiles):
                    m_cur = jnp.maximum(m_cur, s[:, t * LANES:(t + 1) * LANES])
            else:
                m_cur = s
            m_new = jnp.maximum(m_prev, jnp.max(m_cur, axis=1, keepdims=True))
            alpha = jnp.exp2(m_prev - m_new)
            if tiles:
                p = jnp.exp2(s - pltpu.repeat(m_new, tiles, 1))
                p_sum = p[:, :LANES]
                for t in range(1, tiles):
                    p_sum = p_sum + p[:, t * LANES:(t + 1) * LANES]
            else:
                p = jnp.exp2(s - m_new[:, :width])
                p_sum = jnp.where(first_lane, jnp.sum(p, axis=1, keepdims=True), 0.0)
            l_ref[h, rows, :] = alpha * l_ref[h, rows, :] + p_sum
            acc_ref[h, rows, :] = alpha * acc_ref[h, rows, :] + _dot(p.astype(BF16), vt)
            m_ref[h, rows, :] = m_new

    all_visible = lambda r0, rows, width: None

    if n_prefix:
        def prefix_visible(r0, rows, width):
            return lax.broadcasted_iota(jnp.int32, (rows, width), 1) < n_prefix
        for h in range(2):
            update(h, kp_ref[...], vp_ref[...], prefix_visible, 0)

    def body(j, carry):
        for c in range(tq // tk):
            sl = pl.ds(pl.multiple_of(j * tq + c * tk, tk), tk)
            kt = k_ref[0, sl, :]
            vt = v_ref[0, sl, :]
            for h in range(2):
                update(h, kt, vt, all_visible, 0)
        return carry

    lax.fori_loop(0, i, body, 0)

    for c in range(tq // tk):
        def diag_visible(r0, rows, width, c=c):
            if r0 >= c * tk + width - 1:
                return None
            if r0 + rows - 1 < c * tk:
                return False
            row = lax.broadcasted_iota(jnp.int32, (rows, width), 0) + r0
            col = lax.broadcasted_iota(jnp.int32, (rows, width), 1) + c * tk
            return col <= row
        sl = pl.ds(pl.multiple_of(i * tq + c * tk, tk), tk)
        kt = k_ref[0, sl, :]
        vt = v_ref[0, sl, :]
        for h in range(2):
            update(h, kt, vt, diag_visible, (c * tk // rb) * rb)

    out_lane = lax.broadcasted_iota(jnp.int32, (1, PAIR_V), 1)
    l_a = jnp.sum(l_ref[0], axis=1, keepdims=True)
    l_b = jnp.sum(l_ref[1], axis=1, keepdims=True)
    o = jnp.where(out_lane < MLA_V_DIM, acc_ref[0] / l_a, acc_ref[1] / l_b)
    o_ref[0] = o.astype(BF16)


def _attn_call(q, k, v, prefix, tq, tk, rb):
    nb, seq, _ = q.shape
    n_prefix = 0 if prefix is None else N_META
    in_specs = [pl.BlockSpec((1, tq, PAIR_QK), lambda b, p, i: (b, i, p)),
                pl.BlockSpec((1, seq, PAIR_QK), lambda b, p, i: (b, 0, p)),
                pl.BlockSpec((1, seq, PAIR_V), lambda b, p, i: (b, 0, p))]
    args = [q, k, v]
    if n_prefix:
        pad = ((0, LANES - n_prefix), (0, 0))
        in_specs += [pl.BlockSpec((LANES, PAIR_QK), lambda b, p, i: (0, p)),
                     pl.BlockSpec((LANES, PAIR_V), lambda b, p, i: (0, p))]
        args += [jnp.pad(prefix[0], pad), jnp.pad(prefix[1], pad)]
    return pl.pallas_call(
        functools.partial(_attn_kernel, tq=tq, tk=tk, rb=rb, n_prefix=n_prefix),
        grid=(nb, N_PAIRS, seq // tq), in_specs=in_specs,
        out_specs=pl.BlockSpec((1, tq, PAIR_V), lambda b, p, i: (b, i, p)),
        out_shape=jax.ShapeDtypeStruct((nb, seq, V_WIDTH), BF16),
        scratch_shapes=[pltpu.VMEM((2, tq, LANES), F32), pltpu.VMEM((2, tq, LANES), F32),
                        pltpu.VMEM((2, tq, PAIR_V), F32)],
        compiler_params=pltpu.CompilerParams(
            dimension_semantics=("arbitrary", "arbitrary", "arbitrary"),
            vmem_limit_bytes=VMEM_LIMIT),
        name="attn",
    )(*args)


def _merge_kernel(x_ref, o_ref, g_ref, lng_ref, lnb_ref, wbm_ref, wbf_ref, wout_ref, mixg_ref,
                  mixb_ref, h_ref):
    half = V_WIDTH // 2
    merged = (g_ref[0, :, :D_MODEL].astype(F32) * _dot(o_ref[0, :, :half], wbm_ref[...])
              + g_ref[0, :, D_MODEL:].astype(F32) * _dot(o_ref[0, :, half:], wbf_ref[...]))
    mixed = _dot(merged.astype(BF16), wout_ref[...])
    h0 = _layer_norm(x_ref[0], lng_ref[...], lnb_ref[...])
    h_ref[0] = _layer_norm(DN_ALPHA * h0 + mixed, mixg_ref[...], mixb_ref[...])


def _merge_call(x3, o, g, w, tm):
    nb, seq, _ = x3.shape
    row_blk = lambda width: pl.BlockSpec((1, tm, width), lambda b, j: (b, j, 0))
    consts = [w["ln_emb_g"], w["ln_emb_b"], w["w_bm"], w["w_bf"], w["w_out"], w["ln_mix_g"],
              w["ln_mix_b"]]
    return pl.pallas_call(
        _merge_kernel, grid=(nb, seq // tm),
        in_specs=[row_blk(D_MODEL), row_blk(V_WIDTH), row_blk(2 * D_MODEL)]
        + [_resident(a.shape) for a in consts],
        out_specs=row_blk(D_MODEL), out_shape=jax.ShapeDtypeStruct((nb, seq, D_MODEL), F32),
        compiler_params=_params(), name="merge",
    )(x3, o, g, *consts)


def _ffn_kernel(h_ref, halo0_ref, wup_ref, cw_ref, cb_ref, wdown_ref, lng_ref, lnb_ref,
                out_ref, tail_ref, halo_ref):
    @pl.when(pl.program_id(1) == 0)
    def _():
        halo_ref[...] = halo0_ref[...]

    h = h_ref[0]
    hb = h.astype(BF16)
    rows = h.shape[0]
    row = lax.broadcasted_iota(jnp.int32, (rows, FF_CHUNK), 0)
    acc = jnp.zeros((rows, D_MODEL), F32)
    for c in range(N_FF_CHUNKS):
        sl = slice(c * FF_CHUNK, (c + 1) * FF_CHUNK)
        gate = _dot(hb, wup_ref[:, sl])
        val = _dot(hb, wup_ref[:, D_FF + c * FF_CHUNK:D_FF + (c + 1) * FF_CHUNK])
        prev = halo_ref[:, sl]
        back1 = prev[HALO_ROWS - 1:HALO_ROWS, :]
        back2 = prev[HALO_ROWS - 2:HALO_ROWS - 1, :]
        g1 = jnp.where(row == 0, back1, pltpu.roll(gate, 1, 0))
        g2 = jnp.where(row == 0, back2, jnp.where(row == 1, back1, pltpu.roll(gate, 2, 0)))
        halo_ref[:, sl] = gate[rows - HALO_ROWS:, :]
        conv = (cw_ref[0:1, sl] * g2 + cw_ref[1:2, sl] * g1 + cw_ref[2:3, sl] * gate
                + cb_ref[:, sl])
        act = conv * (1.0 / (1.0 + jnp.exp(-conv))) * val
        acc = acc + _dot(act.astype(BF16), wdown_ref[sl, :])
    out_ref[0] = _layer_norm(DN_ALPHA * h + acc, lng_ref[...], lnb_ref[...])
    tail_ref[0] = halo_ref[...]


def _ffn_call(h1, halo0, w, tm):
    nb, seq, _ = h1.shape
    row_blk = pl.BlockSpec((1, tm, D_MODEL), lambda b, j: (b, j, 0))
    consts = [halo0, w["w_up"], w["conv_w"], w["conv_b"], w["w_down"], w["ln_ffn_g"],
              w["ln_ffn_b"]]
    return pl.pallas_call(
        _ffn_kernel, grid=(nb, seq // tm),
        in_specs=[row_blk] + [_resident(a.shape) for a in consts],
        out_specs=(row_blk, pl.BlockSpec((1, HALO_ROWS, D_FF), lambda b, j: (b, 0, 0))),
        out_shape=(jax.ShapeDtypeStruct((nb, seq, D_MODEL), F32),
                   jax.ShapeDtypeStruct((nb, HALO_ROWS, D_FF), F32)),
        scratch_shapes=[pltpu.VMEM((HALO_ROWS, D_FF), F32)],
        compiler_params=_params(), name="ffn",
    )(h1, *consts)


def _rope_tables(length):
    half = MLA_ROPE_DIM // 2
    inv_freq = ROPE_THETA ** (-jnp.arange(half, dtype=F32) / half)
    ang = jnp.arange(length).astype(F32)[:, None] * inv_freq[None, :]
    cos, sin = jnp.cos(ang), jnp.sin(ang)
    zero = jnp.zeros_like(cos)
    widen = lambda a, b: jnp.pad(jnp.tile(jnp.concatenate([a, b], axis=1), (1, 2)),
                                 ((0, 0), (0, LANES - 2 * MLA_ROPE_DIM)))
    return widen(cos, cos), widen(zero, sin), widen(-sin, zero)


def _bias_selectors():
    width = (N_PAIRS // 2) * LANES
    sel_q = np.zeros((3, LANES, width), np.float32)
    sel_k = np.zeros((3, LANES, width), np.float32)
    const_q = np.zeros((1, width), np.float32)
    const_k = np.zeros((1, width), np.float32)
    for h in range(FOX_HEADS):
        base = LANES * (h // 2) + EXTRA_GROUP * (h % 2)
        for piece in range(3):
            sel_q[piece, h, base + piece] = 1.0
            const_q[0, base + 3 + piece] = 1.0
            const_k[0, base + piece] = 1.0
            sel_k[piece, h, base + 3 + piece] = -1.0
    return (jnp.asarray(sel_q, BF16), jnp.asarray(const_q), jnp.asarray(sel_k, BF16),
            jnp.asarray(const_k))


def _prepare_weights(w_in, b_gate, b_forget, q_norm_g, w_q_up, kv_norm_g, w_kv_up, w_branch_mla,
                     w_branch_fox, w_out, w_ffn_up, conv_w, conv_b, w_ffn_down):
    n_half = N_PAIRS // 2
    o_qlat, o_kvlat, o_krope = 0, MLA_Q_RANK, MLA_Q_RANK + MLA_KV_RANK
    o_fq = o_krope + MLA_ROPE_DIM
    o_fk, o_fv = o_fq + FOX_WIDTH, o_fq + 2 * FOX_WIDTH
    o_flogit = o_fv + FOX_WIDTH
    o_gate = o_flogit + FOX_HEADS
    k_rope = w_in[:, o_krope:o_fq]
    w_in_p = jnp.concatenate([
        w_in[:, o_qlat:o_krope],
        k_rope, k_rope, jnp.zeros((D_MODEL, LANES - 2 * MLA_ROPE_DIM), F32),
        w_in[:, o_fq:o_flogit],
        w_in[:, o_flogit:o_gate], jnp.zeros((D_MODEL, LANES - FOX_HEADS), F32),
        w_in[:, o_gate:],
    ], axis=1).astype(BF16)
    wq = w_q_up.reshape(MLA_Q_RANK, n_half, 2, MLA_QK_DIM)
    wq_p = jnp.concatenate([
        wq[..., :MLA_NOPE_DIM].reshape(MLA_Q_RANK, n_half, LANES),
        wq[..., MLA_NOPE_DIM:].reshape(MLA_Q_RANK, n_half, 2 * MLA_ROPE_DIM),
        jnp.zeros((MLA_Q_RANK, n_half, LANES - 2 * MLA_ROPE_DIM), F32)], axis=-1)
    wkv = w_kv_up.reshape(MLA_KV_RANK, MLA_HEADS, MLA_NOPE_DIM + MLA_V_DIM)
    return {
        "w_in": w_in_p,
        "b_gate": b_gate.reshape(1, -1),
        "b_forget": jnp.pad(b_forget.reshape(1, -1), ((0, 0), (0, LANES - FOX_HEADS))),
        "q_norm_g": q_norm_g.reshape(1, -1),
        "w_q": wq_p.reshape(MLA_Q_RANK, n_half * PAIR_QK).astype(BF16),
        "kv_norm_g": kv_norm_g.reshape(1, -1),
        "w_k": wkv[..., :MLA_NOPE_DIM].reshape(MLA_KV_RANK, -1).astype(BF16),
        "w_v": wkv[..., MLA_NOPE_DIM:].reshape(MLA_KV_RANK, -1).astype(BF16),
        "w_bm": w_branch_mla.astype(BF16),
        "w_bf": w_branch_fox.astype(BF16),
        "w_out": w_out.astype(BF16),
        "w_up": w_ffn_up.astype(BF16),
        "conv_w": conv_w,
        "conv_b": conv_b.reshape(1, -1),
        "w_down": w_ffn_down.astype(BF16),
    }


def _layer(x3, tabs, cum0, halo0, prefix, w, tm, tq, tk, rb):
    q, k, v, gates, cum_end = _proj_call(x3, tabs, cum0, w, tm)
    o = _attn_call(q, k, v, prefix, tq, tk, rb)
    h1 = _merge_call(x3, o, gates, w, tm)
    out, tail = _ffn_call(h1, halo0, w, tm)
    return out, (k[0], v[0]), cum_end[0], tail[0]


def kernel(x, meta_tokens, ln_emb_g, ln_emb_b, w_in, b_gate, b_forget, q_norm_g, w_q_up, kv_norm_g, w_kv_up, w_branch_mla, w_branch_fox, w_out, ln_mix_g, ln_mix_b, w_ffn_up, conv_w, conv_b, w_ffn_down, ln_ffn_g, ln_ffn_b):
    seq = x.shape[1]
    w = _prepare_weights(w_in[0], b_gate[0], b_forget[0], q_norm_g[0], w_q_up[0], kv_norm_g[0],
                         w_kv_up[0], w_branch_mla[0], w_branch_fox[0], w_out[0], w_ffn_up[0],
                         conv_w[0], conv_b[0], w_ffn_down[0])
    w.update(ln_emb_g=ln_emb_g.reshape(1, -1), ln_emb_b=ln_emb_b.reshape(1, -1),
             ln_mix_g=ln_mix_g[0].reshape(1, -1), ln_mix_b=ln_mix_b[0].reshape(1, -1),
             ln_ffn_g=ln_ffn_g[0].reshape(1, -1), ln_ffn_b=ln_ffn_b[0].reshape(1, -1))
    w["sel_q"], w["const_q"], w["sel_k"], w["const_k"] = _bias_selectors()
    tabs = _rope_tables(N_META + seq)
    meta_tabs = tuple(t[:N_META] for t in tabs)
    real_tabs = tuple(t[N_META:] for t in tabs)

    _, meta_kv, meta_cum, meta_tail = _layer(
        meta_tokens[None].astype(x.dtype), meta_tabs, jnp.zeros((HALO_ROWS, LANES), F32),
        jnp.zeros((HALO_ROWS, D_FF), F32), None, w, N_META, N_META, N_META, N_META)
    out, _, _, _ = _layer(x, real_tabs, meta_cum, meta_tail, meta_kv, w, 512, seq, 256, 128)
    return out
```

```python
import functools

import numpy as np
import jax
import jax.numpy as jnp
from jax import lax
from jax.experimental import pallas as pl
from jax.experimental.pallas import tpu as pltpu

D_MODEL = 1024
N_META = 16
MLA_HEADS = 8
MLA_Q_RANK = 384
MLA_KV_RANK = 128
MLA_NOPE_DIM = 64
MLA_ROPE_DIM = 32
MLA_QK_DIM = MLA_NOPE_DIM + MLA_ROPE_DIM
MLA_V_DIM = 64
ROPE_THETA = 10000.0
FOX_HEADS = 8
FOX_HEAD_DIM = 64
FOX_WIDTH = FOX_HEADS * FOX_HEAD_DIM
D_FF = 2816
LN_EPS = 1e-5
RMS_EPS = 1e-6
DEPTH = 1
DN_ALPHA = (2 * DEPTH) ** 0.25
NEG_INF = -1e30
LOG2E = 1.4426950408889634

LANES = 128
N_PAIRS = 8
PAIR_QK = 2 * LANES
PAIR_V = 2 * MLA_V_DIM
QK_WIDTH = N_PAIRS * PAIR_QK
V_WIDTH = N_PAIRS * PAIR_V
EXTRA_GROUP = 32
FF_CHUNK = 256
N_FF_CHUNKS = D_FF // FF_CHUNK
HALO_ROWS = 8
VMEM_LIMIT = 56 * 1024 * 1024

C_QLAT = 0
C_KVLAT = C_QLAT + MLA_Q_RANK
C_KROPE = C_KVLAT + MLA_KV_RANK
C_FLOGIT = C_KROPE + LANES
C_FQ = C_FLOGIT + LANES
C_FK = C_FQ + FOX_WIDTH
C_FV = C_FK + FOX_WIDTH
C_GATE = C_FV + FOX_WIDTH
C_TOTAL = C_GATE + 2 * D_MODEL

F32 = jnp.float32
BF16 = jnp.bfloat16


def _dot(a, b):
    return jnp.dot(a, b, preferred_element_type=F32)


def _layer_norm(x, g, b):
    mu = jnp.mean(x, axis=-1, keepdims=True)
    xc = x - mu
    var = jnp.mean(xc * xc, axis=-1, keepdims=True)
    return xc * lax.rsqrt(var + LN_EPS) * g + b


def _rms_norm(x, g):
    return x * lax.rsqrt(jnp.mean(x * x, axis=-1, keepdims=True) + RMS_EPS) * g


def _resident(shape):
    zeros = (0,) * len(shape)
    return pl.BlockSpec(shape, lambda *_: zeros, pipeline_mode=pl.Buffered(1))


def _params():
    return pltpu.CompilerParams(dimension_semantics=("arbitrary", "arbitrary"),
                                vmem_limit_bytes=VMEM_LIMIT)


def _rope(t, cos_t, sa_t, sb_t):
    return (t * cos_t + pltpu.roll(t, MLA_ROPE_DIM // 2, 1) * sa_t
            + pltpu.roll(t, LANES - MLA_ROPE_DIM // 2, 1) * sb_t)


def _cumsum_rows(x):
    rows = x.shape[0]
    row = lax.broadcasted_iota(jnp.int32, x.shape, 0)
    step = 1
    while step < rows:
        x = x + jnp.where(row >= step, pltpu.roll(x, step, 0), 0.0)
        step *= 2
    return x


def _proj_kernel(x_ref, lng_ref, lnb_ref, win_ref, bgate_ref, bforget_ref, qng_ref, wq_ref,
                 kvng_ref, wk_ref, wv_ref, cos_ref, sa_ref, sb_ref, sel_ref, selc_ref, cum0_ref,
                 q_ref, k_ref, v_ref, g_ref, cumend_ref,
                 carry_ref):
    @pl.when(pl.program_id(1) == 0)
    def _():
        carry_ref[...] = cum0_ref[...]

    hb = _layer_norm(x_ref[0], lng_ref[...], lnb_ref[...]).astype(BF16)
    cos_t, sa_t, sb_t = cos_ref[...], sa_ref[...], sb_ref[...]
    mla_scale = MLA_QK_DIM ** -0.5 * LOG2E
    fox_scale = FOX_HEAD_DIM ** -0.5 * LOG2E
    n_half = N_PAIRS // 2

    lat = _dot(hb, win_ref[:, C_QLAT:C_KROPE])
    qn = _rms_norm(lat[:, :MLA_Q_RANK], qng_ref[...]).astype(BF16)
    kvn = _rms_norm(lat[:, MLA_Q_RANK:], kvng_ref[...]).astype(BF16)
    for p in range(n_half):
        qp = _dot(qn, wq_ref[:, p * PAIR_QK:(p + 1) * PAIR_QK]) * mla_scale
        q_ref[0, :, p * PAIR_QK:p * PAIR_QK + LANES] = qp[:, :LANES].astype(BF16)
        q_ref[0, :, p * PAIR_QK + LANES:(p + 1) * PAIR_QK] = _rope(
            qp[:, LANES:], cos_t, sa_t, sb_t).astype(BF16)

    small = _dot(hb, win_ref[:, C_KROPE:C_FQ])
    k_pe = _rope(small[:, :LANES], cos_t, sa_t, sb_t).astype(BF16)
    k_nope = _dot(kvn, wk_ref[...]).astype(BF16)
    for p in range(n_half):
        k_ref[0, :, p * PAIR_QK:p * PAIR_QK + LANES] = k_nope[:, p * LANES:(p + 1) * LANES]
        k_ref[0, :, p * PAIR_QK + LANES:(p + 1) * PAIR_QK] = k_pe
    v_ref[0, :, :V_WIDTH // 2] = _dot(kvn, wv_ref[...]).astype(BF16)

    z = small[:, LANES:] + bforget_ref[...]
    log_f = -(jnp.maximum(-z, 0.0) + jnp.log1p(jnp.exp(-jnp.abs(z))))
    cum = _cumsum_rows(log_f) + carry_ref[0:1, :]
    rows = cum.shape[0]
    carry_ref[...] = jnp.broadcast_to(cum[rows - 1:rows, :], carry_ref.shape)
    cumend_ref[0] = jnp.broadcast_to(cum[rows - 1:rows, :], carry_ref.shape)
    head_lane = lax.broadcasted_iota(jnp.int32, (1, LANES), 1) < FOX_HEADS
    cum = jnp.where(head_lane, cum * LOG2E, 0.0)
    c_hi = cum.astype(BF16).astype(F32)
    r1 = cum - c_hi
    c_mid = r1.astype(BF16).astype(F32)
    c_lo = (r1 - c_mid).astype(BF16).astype(F32)
    pieces = (c_hi + pltpu.roll(c_mid, FOX_HEADS, 1) + pltpu.roll(c_lo, 2 * FOX_HEADS, 1))
    aug = (_dot(pieces.astype(BF16), sel_ref[...]) + selc_ref[...]).astype(BF16)
    fq = (_dot(hb, win_ref[:, C_FQ:C_FQ + FOX_WIDTH]) * fox_scale).astype(BF16)
    fk = _dot(hb, win_ref[:, C_FK:C_FK + FOX_WIDTH]).astype(BF16)
    base = n_half * PAIR_QK
    for p in range(n_half):
        lo_col = base + p * PAIR_QK
        q_ref[0, :, lo_col:lo_col + LANES] = fq[:, p * LANES:(p + 1) * LANES]
        q_ref[0, :, lo_col + LANES:lo_col + PAIR_QK] = aug[:, p * LANES:(p + 1) * LANES]
        k_ref[0, :, lo_col:lo_col + LANES] = fk[:, p * LANES:(p + 1) * LANES]
        k_ref[0, :, lo_col + LANES:lo_col + PAIR_QK] = aug[:, (n_half + p) * LANES:
                                                         (n_half + p + 1) * LANES]
    v_ref[0, :, V_WIDTH // 2:] = _dot(hb, win_ref[:, C_FV:C_FV + FOX_WIDTH]).astype(BF16)

    gate_chunk = 512
    for c in range(2 * D_MODEL // gate_chunk):
        sl = slice(c * gate_chunk, (c + 1) * gate_chunk)
        zg = _dot(hb, win_ref[:, C_GATE + c * gate_chunk:C_GATE + (c + 1) * gate_chunk])
        zg = zg + bgate_ref[:, sl]
        g_ref[0, :, sl] = (1.0 / (1.0 + jnp.exp(-zg))).astype(BF16)


def _proj_call(x3, tabs, cum0, w, tm):
    nb, seq, _ = x3.shape
    row_blk = lambda width: pl.BlockSpec((1, tm, width), lambda b, j: (b, j, 0))
    tab_blk = pl.BlockSpec((tm, LANES), lambda b, j: (j, 0))
    consts = [w["ln_emb_g"], w["ln_emb_b"], w["w_in"], w["b_gate"], w["b_forget"], w["q_norm_g"],
              w["w_q"], w["kv_norm_g"], w["w_k"], w["w_v"]]
    sels = [w["sel"], w["sel_const"], cum0]
    in_specs = ([row_blk(D_MODEL)] + [_resident(a.shape) for a in consts] + [tab_blk] * 3
                + [_resident(a.shape) for a in sels])
    out_shape = (jax.ShapeDtypeStruct((nb, seq, QK_WIDTH), BF16),
                 jax.ShapeDtypeStruct((nb, seq, QK_WIDTH), BF16),
                 jax.ShapeDtypeStruct((nb, seq, V_WIDTH), BF16),
                 jax.ShapeDtypeStruct((nb, seq, 2 * D_MODEL), BF16),
                 jax.ShapeDtypeStruct((nb, HALO_ROWS, LANES), F32))
    out_specs = (row_blk(QK_WIDTH), row_blk(QK_WIDTH), row_blk(V_WIDTH), row_blk(2 * D_MODEL),
                 pl.BlockSpec((1, HALO_ROWS, LANES), lambda b, j: (b, 0, 0)))
    return pl.pallas_call(
        _proj_kernel, grid=(nb, seq // tm), in_specs=in_specs, out_specs=out_specs,
        out_shape=out_shape, scratch_shapes=[pltpu.VMEM((HALO_ROWS, LANES), F32)],
        compiler_params=_params(), name="proj",
    )(x3, *consts, *tabs, *sels)


def _attn_kernel(*refs, tk, rb, n_prefix):
    if n_prefix:
        q_ref, k_ref, v_ref, kp_ref, vp_ref, o_ref = refs[:6]
    else:
        q_ref, k_ref, v_ref, o_ref = refs[:4]
    m_ref, l_ref, acc_ref, sp_ref = refs[-8:-4]
    s_refs = (refs[-4:-2], refs[-2:])
    q = q_ref[0]
    tq = q.shape[0]
    lane = lax.broadcasted_iota(jnp.int32, (1, PAIR_QK), 1)
    head_a = (lane < MLA_NOPE_DIM) | ((lane >= LANES) & (lane < LANES + EXTRA_GROUP))
    head_b = (((lane >= MLA_NOPE_DIM) & (lane < LANES))
              | ((lane >= LANES + EXTRA_GROUP) & (lane < LANES + 2 * EXTRA_GROUP)))
    zero = jnp.zeros_like(q)
    qs = (jnp.where(head_a, q, zero), jnp.where(head_b, q, zero))
    nt_dims = (((1,), (1,)), ((), ()))
    key_lane = lax.broadcasted_iota(jnp.int32, (1, LANES), 1)
    first_lane = key_lane == 0

    if n_prefix:
        grp_a = key_lane < n_prefix
        grp_b = (key_lane >= n_prefix) & (key_lane < 2 * n_prefix)
        sp_ref[...] = lax.dot_general(q, kp_ref[...], nt_dims, preferred_element_type=F32)
        for r0 in range(0, tq, rb):
            rows = slice(r0, r0 + rb)
            s = sp_ref[rows, :]
            m_a = jnp.max(jnp.where(grp_a, s, NEG_INF), axis=1, keepdims=True)
            m_b = jnp.max(jnp.where(grp_b, s, NEG_INF), axis=1, keepdims=True)
            p = jnp.exp2(jnp.where(grp_a, s - m_a, jnp.where(grp_b, s - m_b, NEG_INF)))
            l_a = jnp.sum(jnp.where(grp_a, p, 0.0), axis=1, keepdims=True)
            l_b = jnp.sum(jnp.where(grp_b, p, 0.0), axis=1, keepdims=True)
            pv = _dot(p.astype(BF16), vp_ref[...])
            for h, m_h, l_h in ((0, m_a, l_a), (1, m_b, l_b)):
                m_ref[h, rows, :] = jnp.broadcast_to(m_h, (rb, LANES))
                l_ref[h, rows, :] = jnp.where(first_lane, l_h, 0.0)
                acc_ref[h, rows, :] = pv
    else:
        m_ref[...] = jnp.full(m_ref.shape, NEG_INF, F32)
        l_ref[...] = jnp.zeros(l_ref.shape, F32)
        acc_ref[...] = jnp.zeros(acc_ref.shape, F32)

    def update(h, key0, s_ref):
        row0 = (key0 // rb) * rb
        s_ref[row0:, :] = lax.dot_general(qs[h][row0:, :], k_ref[0, key0:key0 + tk, :],
                                          nt_dims, preferred_element_type=F32)
        for r0 in range(row0, tq, rb):
            n_vis = r0 + rb - key0
            if n_vis <= 0:
                continue
            width = tk
            step = min(width, LANES)
            rows = slice(r0, r0 + rb)
            masked = r0 < key0 + width - 1

            def s_tile(t):
                s = s_ref[rows, t * step:(t + 1) * step]
                if masked:
                    row = lax.broadcasted_iota(jnp.int32, (rb, step), 0) + r0
                    col = lax.broadcasted_iota(jnp.int32, (rb, step), 1) + (key0 + t * step)
                    s = jnp.where(col <= row, s, NEG_INF)
                return s

            tiles = width // step
            m_cur = s_tile(0)
            for t in range(1, tiles):
                m_cur = jnp.maximum(m_cur, s_tile(t))
            m_prev = m_ref[h, rows, :]
            m_new = jnp.maximum(m_prev, jnp.max(m_cur, axis=1, keepdims=True))
            alpha = jnp.exp2(m_prev - m_new)
            p_tiles = [jnp.exp2(s_tile(t) - m_new[:, :step]) for t in range(tiles)]
            p_sum = p_tiles[0]
            for t in range(1, tiles):
                p_sum = p_sum + p_tiles[t]
            if step < LANES:
                p_sum = jnp.where(first_lane, jnp.sum(p_sum, axis=1, keepdims=True), 0.0)
            p = p_tiles[0] if tiles == 1 else jnp.concatenate(p_tiles, axis=1)
            pv = _dot(p.astype(BF16), v_ref[0, key0:key0 + width, :])
            l_ref[h, rows, :] = alpha * l_ref[h, rows, :] + p_sum
            acc_ref[h, rows, :] = alpha * acc_ref[h, rows, :] + pv
            m_ref[h, rows, :] = m_new

    for c, key0 in enumerate(range(0, tq, tk)):
        for h in range(2):
            update(h, key0, s_refs[c % 2][h])

    out_lane = lax.broadcasted_iota(jnp.int32, (1, PAIR_V), 1)
    l_a = jnp.sum(l_ref[0], axis=1, keepdims=True)
    l_b = jnp.sum(l_ref[1], axis=1, keepdims=True)
    o = jnp.where(out_lane < MLA_V_DIM, acc_ref[0] / l_a, acc_ref[1] / l_b)
    o_ref[0] = o.astype(BF16)


def _pair_prefix(k_meta, v_meta):
    n = k_meta.shape[0]
    lane = jnp.arange(QK_WIDTH) % PAIR_QK
    k_a = (lane < MLA_NOPE_DIM) | ((lane >= LANES) & (lane < LANES + EXTRA_GROUP))
    k_b = (((lane >= MLA_NOPE_DIM) & (lane < LANES))
           | ((lane >= LANES + EXTRA_GROUP) & (lane < LANES + 2 * EXTRA_GROUP)))
    v_a = (jnp.arange(V_WIDTH) % PAIR_V) < MLA_V_DIM
    zero_k, zero_v = jnp.zeros_like(k_meta), jnp.zeros_like(v_meta)
    pad = ((0, LANES - 2 * n), (0, 0))
    kp = jnp.concatenate([jnp.where(k_a, k_meta, zero_k), jnp.where(k_b, k_meta, zero_k)])
    vp = jnp.concatenate([jnp.where(v_a, v_meta, zero_v), jnp.where(v_a, zero_v, v_meta)])
    return jnp.pad(kp, pad), jnp.pad(vp, pad)


def _attn_call(q, k, v, prefix, tk, rb):
    nb, seq, _ = q.shape
    n_prefix = 0 if prefix is None else N_META
    in_specs = [pl.BlockSpec((1, seq, PAIR_QK), lambda b, p: (b, 0, p)),
                pl.BlockSpec((1, seq, PAIR_QK), lambda b, p: (b, 0, p)),
                pl.BlockSpec((1, seq, PAIR_V), lambda b, p: (b, 0, p))]
    args = [q, k, v]
    if n_prefix:
        in_specs += [pl.BlockSpec((LANES, PAIR_QK), lambda b, p: (0, p)),
                     pl.BlockSpec((LANES, PAIR_V), lambda b, p: (0, p))]
        args += list(_pair_prefix(*prefix))
    return pl.pallas_call(
        functools.partial(_attn_kernel, tk=tk, rb=rb, n_prefix=n_prefix),
        grid=(nb, N_PAIRS), in_specs=in_specs,
        out_specs=pl.BlockSpec((1, seq, PAIR_V), lambda b, p: (b, 0, p)),
        out_shape=jax.ShapeDtypeStruct((nb, seq, V_WIDTH), BF16),
        scratch_shapes=[pltpu.VMEM((2, seq, LANES), F32), pltpu.VMEM((2, seq, LANES), F32),
                        pltpu.VMEM((2, seq, PAIR_V), F32), pltpu.VMEM((seq, LANES), F32)]
        + [pltpu.VMEM((seq, tk), F32)] * 4,
        compiler_params=_params(),
        name="attn",
    )(*args)


def _merge_kernel(x_ref, o_ref, g_ref, lng_ref, lnb_ref, wbm_ref, wbf_ref, wout_ref, mixg_ref,
                  mixb_ref, h_ref):
    half = V_WIDTH // 2
    merged = (g_ref[0, :, :D_MODEL].astype(F32) * _dot(o_ref[0, :, :half], wbm_ref[...])
              + g_ref[0, :, D_MODEL:].astype(F32) * _dot(o_ref[0, :, half:], wbf_ref[...]))
    mixed = _dot(merged.astype(BF16), wout_ref[...])
    h0 = _layer_norm(x_ref[0], lng_ref[...], lnb_ref[...])
    h_ref[0] = _layer_norm(DN_ALPHA * h0 + mixed, mixg_ref[...], mixb_ref[...])


def _merge_call(x3, o, g, w, tm):
    nb, seq, _ = x3.shape
    row_blk = lambda width: pl.BlockSpec((1, tm, width), lambda b, j: (b, j, 0))
    consts = [w["ln_emb_g"], w["ln_emb_b"], w["w_bm"], w["w_bf"], w["w_out"], w["ln_mix_g"],
              w["ln_mix_b"]]
    return pl.pallas_call(
        _merge_kernel, grid=(nb, seq // tm),
        in_specs=[row_blk(D_MODEL), row_blk(V_WIDTH), row_blk(2 * D_MODEL)]
        + [_resident(a.shape) for a in consts],
        out_specs=row_blk(D_MODEL), out_shape=jax.ShapeDtypeStruct((nb, seq, D_MODEL), F32),
        compiler_params=_params(), name="merge",
    )(x3, o, g, *consts)


def _ffn_kernel(h_ref, halo0_ref, wup_ref, cw_ref, cb_ref, wdown_ref, lng_ref, lnb_ref,
                out_ref, tail_ref, halo_ref, gate_ref, act_ref):
    @pl.when(pl.program_id(1) == 0)
    def _():
        halo_ref[...] = halo0_ref[...]

    h = h_ref[0]
    hb = h.astype(BF16)
    rows = h.shape[0]
    for c in range(N_FF_CHUNKS):
        sl = slice(c * FF_CHUNK, (c + 1) * FF_CHUNK)
        gate = _dot(hb, wup_ref[:, sl])
        val = _dot(hb, wup_ref[:, D_FF + c * FF_CHUNK:D_FF + (c + 1) * FF_CHUNK])
        gate_ref[c, :HALO_ROWS, :] = halo_ref[:, sl]
        gate_ref[c, HALO_ROWS:, :] = gate
        halo_ref[:, sl] = gate[rows - HALO_ROWS:, :]
        back1 = gate_ref[c, HALO_ROWS - 1:HALO_ROWS - 1 + rows, :]
        back2 = gate_ref[c, HALO_ROWS - 2:HALO_ROWS - 2 + rows, :]
        conv = (cw_ref[0:1, sl] * back2 + cw_ref[1:2, sl] * back1 + cw_ref[2:3, sl] * gate
                + cb_ref[:, sl])
        act_ref[:, sl] = (conv * (1.0 / (1.0 + jnp.exp(-conv))) * val).astype(BF16)
    ffn = _dot(act_ref[...], wdown_ref[...])
    out_ref[0] = _layer_norm(DN_ALPHA * h + ffn, lng_ref[...], lnb_ref[...])
    tail_ref[0] = halo_ref[...]


def _ffn_call(h1, halo0, w, tm):
    nb, seq, _ = h1.shape
    row_blk = pl.BlockSpec((1, tm, D_MODEL), lambda b, j: (b, j, 0))
    consts = [halo0, w["w_up"], w["conv_w"], w["conv_b"], w["w_down"], w["ln_ffn_g"],
              w["ln_ffn_b"]]
    return pl.pallas_call(
        _ffn_kernel, grid=(nb, seq // tm),
        in_specs=[row_blk] + [_resident(a.shape) for a in consts],
        out_specs=(row_blk, pl.BlockSpec((1, HALO_ROWS, D_FF), lambda b, j: (b, 0, 0))),
        out_shape=(jax.ShapeDtypeStruct((nb, seq, D_MODEL), F32),
                   jax.ShapeDtypeStruct((nb, HALO_ROWS, D_FF), F32)),
        scratch_shapes=[pltpu.VMEM((HALO_ROWS, D_FF), F32),
                        pltpu.VMEM((N_FF_CHUNKS, HALO_ROWS + tm, FF_CHUNK), F32),
                        pltpu.VMEM((tm, D_FF), BF16)],
        compiler_params=_params(), name="ffn",
    )(h1, *consts)


def _rope_tables(length):
    half = MLA_ROPE_DIM // 2
    inv_freq = ROPE_THETA ** (-jnp.arange(half, dtype=F32) / half)
    ang = jnp.arange(length).astype(F32)[:, None] * inv_freq[None, :]
    cos, sin = jnp.cos(ang), jnp.sin(ang)
    zero = jnp.zeros_like(cos)
    widen = lambda a, b: jnp.pad(jnp.tile(jnp.concatenate([a, b], axis=1), (1, 2)),
                                 ((0, 0), (0, LANES - 2 * MLA_ROPE_DIM)))
    return widen(cos, cos), widen(zero, sin), widen(-sin, zero)


def _bias_selectors():
    half = (N_PAIRS // 2) * LANES
    sel = np.zeros((LANES, 2 * half), np.float32)
    const = np.zeros((1, 2 * half), np.float32)
    for h in range(FOX_HEADS):
        base = LANES * (h // 2) + EXTRA_GROUP * (h % 2)
        for piece in range(3):
            sel[piece * FOX_HEADS + h, base + piece] = 1.0
            const[0, base + 3 + piece] = 1.0
            const[0, half + base + piece] = 1.0
            sel[piece * FOX_HEADS + h, half + base + 3 + piece] = -1.0
    return jnp.asarray(sel, BF16), jnp.asarray(const)


def _prepare_weights(w_in, b_gate, b_forget, q_norm_g, w_q_up, kv_norm_g, w_kv_up, w_branch_mla,
                     w_branch_fox, w_out, w_ffn_up, conv_w, conv_b, w_ffn_down):
    n_half = N_PAIRS // 2
    o_qlat, o_kvlat, o_krope = 0, MLA_Q_RANK, MLA_Q_RANK + MLA_KV_RANK
    o_fq = o_krope + MLA_ROPE_DIM
    o_fk, o_fv = o_fq + FOX_WIDTH, o_fq + 2 * FOX_WIDTH
    o_flogit = o_fv + FOX_WIDTH
    o_gate = o_flogit + FOX_HEADS
    w_in = w_in.astype(BF16)
    k_rope = w_in[:, o_krope:o_fq]
    w_in_p = jnp.concatenate([
        w_in[:, o_qlat:o_krope],
        k_rope, k_rope, jnp.zeros((D_MODEL, LANES - 2 * MLA_ROPE_DIM), BF16),
        w_in[:, o_flogit:o_gate], jnp.zeros((D_MODEL, LANES - FOX_HEADS), BF16),
        w_in[:, o_fq:o_flogit],
        w_in[:, o_gate:],
    ], axis=1)
    wq = w_q_up.reshape(MLA_Q_RANK, n_half, 2, MLA_QK_DIM)
    wq_p = jnp.concatenate([
        wq[..., :MLA_NOPE_DIM].reshape(MLA_Q_RANK, n_half, LANES),
        wq[..., MLA_NOPE_DIM:].reshape(MLA_Q_RANK, n_half, 2 * MLA_ROPE_DIM),
        jnp.zeros((MLA_Q_RANK, n_half, LANES - 2 * MLA_ROPE_DIM), F32)], axis=-1)
    wkv = w_kv_up.reshape(MLA_KV_RANK, MLA_HEADS, MLA_NOPE_DIM + MLA_V_DIM)
    return {
        "w_in": w_in_p,
        "b_gate": b_gate.reshape(1, -1),
        "b_forget": jnp.pad(b_forget.reshape(1, -1), ((0, 0), (0, LANES - FOX_HEADS))),
        "q_norm_g": q_norm_g.reshape(1, -1),
        "w_q": wq_p.reshape(MLA_Q_RANK, n_half * PAIR_QK).astype(BF16),
        "kv_norm_g": kv_norm_g.reshape(1, -1),
        "w_k": wkv[..., :MLA_NOPE_DIM].reshape(MLA_KV_RANK, -1).astype(BF16),
        "w_v": wkv[..., MLA_NOPE_DIM:].reshape(MLA_KV_RANK, -1).astype(BF16),
        "w_bm": w_branch_mla.astype(BF16),
        "w_bf": w_branch_fox.astype(BF16),
        "w_out": w_out.astype(BF16),
        "w_up": w_ffn_up.astype(BF16),
        "conv_w": conv_w,
        "conv_b": conv_b.reshape(1, -1),
        "w_down": w_ffn_down.astype(BF16),
    }


def _layer(x3, tabs, cum0, halo0, prefix, w, tm, tk, rb):
    q, k, v, gates, cum_end = _proj_call(x3, tabs, cum0, w, tm)
    o = _attn_call(q, k, v, prefix, tk, rb)
    h1 = _merge_call(x3, o, gates, w, tm)
    out, tail = _ffn_call(h1, halo0, w, tm)
    return out, (k[0], v[0]), cum_end[0], tail[0]


def kernel(x, meta_tokens, ln_emb_g, ln_emb_b, w_in, b_gate, b_forget, q_norm_g, w_q_up, kv_norm_g, w_kv_up, w_branch_mla, w_branch_fox, w_out, ln_mix_g, ln_mix_b, w_ffn_up, conv_w, conv_b, w_ffn_down, ln_ffn_g, ln_ffn_b):
    seq = x.shape[1]
    w = _prepare_weights(w_in[0], b_gate[0], b_forget[0], q_norm_g[0], w_q_up[0], kv_norm_g[0],
                         w_kv_up[0], w_branch_mla[0], w_branch_fox[0], w_out[0], w_ffn_up[0],
                         conv_w[0], conv_b[0], w_ffn_down[0])
    w.update(ln_emb_g=ln_emb_g.reshape(1, -1), ln_emb_b=ln_emb_b.reshape(1, -1),
             ln_mix_g=ln_mix_g[0].reshape(1, -1), ln_mix_b=ln_mix_b[0].reshape(1, -1),
             ln_ffn_g=ln_ffn_g[0].reshape(1, -1), ln_ffn_b=ln_ffn_b[0].reshape(1, -1))
    w["sel"], w["sel_const"] = _bias_selectors()
    tabs = _rope_tables(N_META + seq)
    meta_tabs = tuple(t[:N_META] for t in tabs)
    real_tabs = tuple(t[N_META:] for t in tabs)

    _, meta_kv, meta_cum, meta_tail = _layer(
        meta_tokens[None].astype(x.dtype), meta_tabs, jnp.zeros((HALO_ROWS, LANES), F32),
        jnp.zeros((HALO_ROWS, D_FF), F32), None, w, N_META, N_META, N_META)
    out, _, _, _ = _layer(x, real_tabs, meta_cum, meta_tail, meta_kv, w, 512, 256, 256)
    return out
```

```python
import functools

import numpy as np
import jax
import jax.numpy as jnp
from jax import lax
from jax.experimental import pallas as pl
from jax.experimental.pallas import tpu as pltpu

D_MODEL = 1024
N_META = 16
MLA_HEADS = 8
MLA_Q_RANK = 384
MLA_KV_RANK = 128
MLA_NOPE_DIM = 64
MLA_ROPE_DIM = 32
MLA_QK_DIM = MLA_NOPE_DIM + MLA_ROPE_DIM
MLA_V_DIM = 64
ROPE_THETA = 10000.0
FOX_HEADS = 8
FOX_HEAD_DIM = 64
FOX_WIDTH = FOX_HEADS * FOX_HEAD_DIM
D_FF = 2816
LN_EPS = 1e-5
RMS_EPS = 1e-6
DEPTH = 1
DN_ALPHA = (2 * DEPTH) ** 0.25
NEG_INF = -1e30
LOG2E = 1.4426950408889634

LANES = 128
N_PAIRS = 8
PAIR_QK = 2 * LANES
PAIR_V = 2 * MLA_V_DIM
QK_WIDTH = N_PAIRS * PAIR_QK
V_WIDTH = N_PAIRS * PAIR_V
VX_WIDTH = N_PAIRS * PAIR_QK
EXTRA_GROUP = 32
FF_CHUNK = 256
N_FF_CHUNKS = D_FF // FF_CHUNK
HALO_ROWS = 8
VMEM_LIMIT = 56 * 1024 * 1024

C_QLAT = 0
C_KVLAT = C_QLAT + MLA_Q_RANK
C_KROPE = C_KVLAT + MLA_KV_RANK
C_FLOGIT = C_KROPE + LANES
C_FQ = C_FLOGIT + LANES
C_FK = C_FQ + FOX_WIDTH
C_FV = C_FK + FOX_WIDTH
C_GATE = C_FV + FOX_WIDTH
C_TOTAL = C_GATE + 2 * D_MODEL

F32 = jnp.float32
BF16 = jnp.bfloat16


def _dot(a, b):
    return jnp.dot(a, b, preferred_element_type=F32)


def _layer_norm(x, g, b):
    mu = jnp.mean(x, axis=-1, keepdims=True)
    xc = x - mu
    var = jnp.mean(xc * xc, axis=-1, keepdims=True)
    return xc * lax.rsqrt(var + LN_EPS) * g + b


def _rms_norm(x, g):
    return x * lax.rsqrt(jnp.mean(x * x, axis=-1, keepdims=True) + RMS_EPS) * g


def _resident(shape):
    zeros = (0,) * len(shape)
    return pl.BlockSpec(shape, lambda *_: zeros, pipeline_mode=pl.Buffered(1))


def _params():
    return pltpu.CompilerParams(dimension_semantics=("arbitrary", "arbitrary"),
                                vmem_limit_bytes=VMEM_LIMIT)


def _rope(t, cos_t, sa_t, sb_t):
    return (t * cos_t + pltpu.roll(t, MLA_ROPE_DIM // 2, 1) * sa_t
            + pltpu.roll(t, LANES - MLA_ROPE_DIM // 2, 1) * sb_t)


def _cumsum_rows(x):
    rows = x.shape[0]
    row = lax.broadcasted_iota(jnp.int32, x.shape, 0)
    step = 1
    while step < rows:
        x = x + jnp.where(row >= step, pltpu.roll(x, step, 0), 0.0)
        step *= 2
    return x


def _store_values(v_ref, first_pair, v):
    low_half = lax.broadcasted_iota(jnp.int32, (1, LANES), 1) < MLA_V_DIM
    one = jnp.ones((v.shape[0], LANES), v.dtype)
    for p in range(N_PAIRS // 2):
        pair = v[:, p * LANES:(p + 1) * LANES]
        col = (first_pair + p) * PAIR_QK
        v_ref[0, :, col:col + LANES] = jnp.where(low_half, pair, one)
        v_ref[0, :, col + LANES:col + PAIR_QK] = jnp.where(low_half, one, pair)


def _proj_kernel(x_ref, lng_ref, lnb_ref, win_ref, bgate_ref, bforget_ref, qng_ref, wq_ref,
                 kvng_ref, wk_ref, wv_ref, cos_ref, sa_ref, sb_ref, sel_ref, selc_ref, cum0_ref,
                 q_ref, k_ref, v_ref, g_ref, cumend_ref,
                 carry_ref):
    @pl.when(pl.program_id(1) == 0)
    def _():
        carry_ref[...] = cum0_ref[...]

    hb = _layer_norm(x_ref[0], lng_ref[...], lnb_ref[...]).astype(BF16)
    cos_t, sa_t, sb_t = cos_ref[...], sa_ref[...], sb_ref[...]
    mla_scale = MLA_QK_DIM ** -0.5 * LOG2E
    fox_scale = FOX_HEAD_DIM ** -0.5 * LOG2E
    n_half = N_PAIRS // 2

    lat = _dot(hb, win_ref[:, C_QLAT:C_KROPE])
    qn = _rms_norm(lat[:, :MLA_Q_RANK], qng_ref[...]).astype(BF16)
    kvn = _rms_norm(lat[:, MLA_Q_RANK:], kvng_ref[...]).astype(BF16)
    for p in range(n_half):
        qp = _dot(qn, wq_ref[:, p * PAIR_QK:(p + 1) * PAIR_QK]) * mla_scale
        q_ref[0, :, p * PAIR_QK:p * PAIR_QK + LANES] = qp[:, :LANES].astype(BF16)
        q_ref[0, :, p * PAIR_QK + LANES:(p + 1) * PAIR_QK] = _rope(
            qp[:, LANES:], cos_t, sa_t, sb_t).astype(BF16)

    small = _dot(hb, win_ref[:, C_KROPE:C_FQ])
    k_pe = _rope(small[:, :LANES], cos_t, sa_t, sb_t).astype(BF16)
    k_nope = _dot(kvn, wk_ref[...]).astype(BF16)
    for p in range(n_half):
        k_ref[0, :, p * PAIR_QK:p * PAIR_QK + LANES] = k_nope[:, p * LANES:(p + 1) * LANES]
        k_ref[0, :, p * PAIR_QK + LANES:(p + 1) * PAIR_QK] = k_pe
    _store_values(v_ref, 0, _dot(kvn, wv_ref[...]).astype(BF16))

    z = small[:, LANES:] + bforget_ref[...]
    log_f = -(jnp.maximum(-z, 0.0) + jnp.log1p(jnp.exp(-jnp.abs(z))))
    cum = _cumsum_rows(log_f) + carry_ref[0:1, :]
    rows = cum.shape[0]
    carry_ref[...] = jnp.broadcast_to(cum[rows - 1:rows, :], carry_ref.shape)
    cumend_ref[0] = jnp.broadcast_to(cum[rows - 1:rows, :], carry_ref.shape)
    head_lane = lax.broadcasted_iota(jnp.int32, (1, LANES), 1) < FOX_HEADS
    cum = jnp.where(head_lane, cum * LOG2E, 0.0)
    c_hi = cum.astype(BF16).astype(F32)
    r1 = cum - c_hi
    c_mid = r1.astype(BF16).astype(F32)
    c_lo = (r1 - c_mid).astype(BF16).astype(F32)
    pieces = (c_hi + pltpu.roll(c_mid, FOX_HEADS, 1) + pltpu.roll(c_lo, 2 * FOX_HEADS, 1))
    aug = (_dot(pieces.astype(BF16), sel_ref[...]) + selc_ref[...]).astype(BF16)
    fq = (_dot(hb, win_ref[:, C_FQ:C_FQ + FOX_WIDTH]) * fox_scale).astype(BF16)
    fk = _dot(hb, win_ref[:, C_FK:C_FK + FOX_WIDTH]).astype(BF16)
    base = n_half * PAIR_QK
    for p in range(n_half):
        lo_col = base + p * PAIR_QK
        q_ref[0, :, lo_col:lo_col + LANES] = fq[:, p * LANES:(p + 1) * LANES]
        q_ref[0, :, lo_col + LANES:lo_col + PAIR_QK] = aug[:, p * LANES:(p + 1) * LANES]
        k_ref[0, :, lo_col:lo_col + LANES] = fk[:, p * LANES:(p + 1) * LANES]
        k_ref[0, :, lo_col + LANES:lo_col + PAIR_QK] = aug[:, (n_half + p) * LANES:
                                                         (n_half + p + 1) * LANES]
    _store_values(v_ref, n_half, _dot(hb, win_ref[:, C_FV:C_FV + FOX_WIDTH]).astype(BF16))

    gate_chunk = 512
    for c in range(2 * D_MODEL // gate_chunk):
        sl = slice(c * gate_chunk, (c + 1) * gate_chunk)
        zg = _dot(hb, win_ref[:, C_GATE + c * gate_chunk:C_GATE + (c + 1) * gate_chunk])
        zg = zg + bgate_ref[:, sl]
        g_ref[0, :, sl] = (1.0 / (1.0 + jnp.exp(-zg))).astype(BF16)


def _proj_call(x3, tabs, cum0, w, tm):
    nb, seq, _ = x3.shape
    row_blk = lambda width: pl.BlockSpec((1, tm, width), lambda b, j: (b, j, 0))
    tab_blk = pl.BlockSpec((tm, LANES), lambda b, j: (j, 0))
    consts = [w["ln_emb_g"], w["ln_emb_b"], w["w_in"], w["b_gate"], w["b_forget"], w["q_norm_g"],
              w["w_q"], w["kv_norm_g"], w["w_k"], w["w_v"]]
    sels = [w["sel"], w["sel_const"], cum0]
    in_specs = ([row_blk(D_MODEL)] + [_resident(a.shape) for a in consts] + [tab_blk] * 3
                + [_resident(a.shape) for a in sels])
    out_shape = (jax.ShapeDtypeStruct((nb, seq, QK_WIDTH), BF16),
                 jax.ShapeDtypeStruct((nb, seq, QK_WIDTH), BF16),
                 jax.ShapeDtypeStruct((nb, seq, VX_WIDTH), BF16),
                 jax.ShapeDtypeStruct((nb, seq, 2 * D_MODEL), BF16),
                 jax.ShapeDtypeStruct((nb, HALO_ROWS, LANES), F32))
    out_specs = (row_blk(QK_WIDTH), row_blk(QK_WIDTH), row_blk(VX_WIDTH), row_blk(2 * D_MODEL),
                 pl.BlockSpec((1, HALO_ROWS, LANES), lambda b, j: (b, 0, 0)))
    return pl.pallas_call(
        _proj_kernel, grid=(nb, seq // tm), in_specs=in_specs, out_specs=out_specs,
        out_shape=out_shape, scratch_shapes=[pltpu.VMEM((HALO_ROWS, LANES), F32)],
        compiler_params=_params(), name="proj",
    )(x3, *consts, *tabs, *sels)


def _attn_kernel(*refs, tk, rb, n_prefix):
    if n_prefix:
        q_ref, k_ref, v_ref, kp_ref, vp_ref, o_ref, m_ref, acc_ref = refs
    else:
        q_ref, k_ref, v_ref, o_ref, m_ref, acc_ref = refs
    q = q_ref[0]
    tq = q.shape[0]
    lane = lax.broadcasted_iota(jnp.int32, (1, PAIR_QK), 1)
    head_a = (lane < MLA_NOPE_DIM) | ((lane >= LANES) & (lane < LANES + EXTRA_GROUP))
    head_b = (((lane >= MLA_NOPE_DIM) & (lane < LANES))
              | ((lane >= LANES + EXTRA_GROUP) & (lane < LANES + 2 * EXTRA_GROUP)))
    zero = jnp.zeros_like(q)
    qs = (jnp.where(head_a, q, zero), jnp.where(head_b, q, zero))
    nt_dims = (((1,), (1,)), ((), ()))
    key_lane = lax.broadcasted_iota(jnp.int32, (1, LANES), 1)

    if n_prefix:
        grp_a = key_lane < n_prefix
        grp_b = (key_lane >= n_prefix) & (key_lane < 2 * n_prefix)
        s_all = lax.dot_general(q, kp_ref[...], nt_dims, preferred_element_type=F32)
        for r0 in range(0, tq, rb):
            rows = slice(r0, r0 + rb)
            s = s_all[rows, :]
            m_a = jnp.max(jnp.where(grp_a, s, NEG_INF), axis=1, keepdims=True)
            m_b = jnp.max(jnp.where(grp_b, s, NEG_INF), axis=1, keepdims=True)
            x = jnp.where(grp_a, s - m_a, jnp.where(grp_b, s - m_b, NEG_INF))
            p = jnp.exp2(x.astype(BF16))
            for h, m_h in ((0, m_a), (1, m_b)):
                m_ref[h, rows, :] = jnp.broadcast_to(m_h, (rb, LANES))
                acc_ref[h, rows, :] = _dot(p, vp_ref[:, h * LANES:(h + 1) * LANES])
    else:
        m_ref[...] = jnp.full(m_ref.shape, NEG_INF, F32)
        acc_ref[...] = jnp.zeros(acc_ref.shape, F32)

    def scores(h, key0):
        row0 = (key0 // rb) * rb
        return lax.dot_general(qs[h][row0:, :], k_ref[0, key0:key0 + tk, :], nt_dims,
                               preferred_element_type=F32)

    def update(h, key0, s_all):
        row0 = (key0 // rb) * rb
        step = min(tk, LANES)
        tiles = tk // step
        for r0 in range(row0, tq, rb):
            if r0 + rb <= key0:
                continue
            rows = slice(r0, r0 + rb)
            s = s_all[r0 - row0:r0 - row0 + rb, :]
            if r0 < key0 + tk - 1:
                row = lax.broadcasted_iota(jnp.int32, (rb, tk), 0) + r0
                col = lax.broadcasted_iota(jnp.int32, (rb, tk), 1) + key0
                s = jnp.where(col <= row, s, NEG_INF)
            m_cur = s[:, :step]
            for t in range(1, tiles):
                m_cur = jnp.maximum(m_cur, s[:, t * step:(t + 1) * step])
            m_prev = m_ref[h, rows, :]
            m_new = jnp.maximum(m_prev, jnp.max(m_cur, axis=1, keepdims=True))
            alpha = jnp.exp2(m_prev - m_new)
            m_wide = m_new[:, :step] if tiles == 1 else jnp.tile(m_new, (1, tiles))
            p = jnp.exp2((s - m_wide).astype(BF16))
            pv = _dot(p, v_ref[0, key0:key0 + tk, h * LANES:(h + 1) * LANES])
            acc_ref[h, rows, :] = alpha * acc_ref[h, rows, :] + pv
            m_ref[h, rows, :] = m_new

    key_tiles = list(range(0, tq, tk))
    ahead = 3
    pending = [[scores(h, key0) for h in range(2)] for key0 in key_tiles[:ahead]]
    for c, key0 in enumerate(key_tiles):
        s_cur = pending.pop(0)
        if c + ahead < len(key_tiles):
            pending.append([scores(h, key_tiles[c + ahead]) for h in range(2)])
        for h in range(2):
            update(h, key0, s_cur[h])

    acc_a, acc_b = acc_ref[0], acc_ref[1]
    o = jnp.where(key_lane < MLA_V_DIM, acc_a / pltpu.roll(acc_a, MLA_V_DIM, 1),
                  acc_b / pltpu.roll(acc_b, MLA_V_DIM, 1))
    o_ref[0] = o.astype(BF16)


def _pair_prefix(k_meta, v_meta):
    n = k_meta.shape[0]
    lane = jnp.arange(QK_WIDTH) % PAIR_QK
    k_a = (lane < MLA_NOPE_DIM) | ((lane >= LANES) & (lane < LANES + EXTRA_GROUP))
    k_b = (((lane >= MLA_NOPE_DIM) & (lane < LANES))
           | ((lane >= LANES + EXTRA_GROUP) & (lane < LANES + 2 * EXTRA_GROUP)))
    v_a = lane < LANES
    zero_k, zero_v = jnp.zeros_like(k_meta), jnp.zeros_like(v_meta)
    pad = ((0, LANES - 2 * n), (0, 0))
    kp = jnp.concatenate([jnp.where(k_a, k_meta, zero_k), jnp.where(k_b, k_meta, zero_k)])
    vp = jnp.concatenate([jnp.where(v_a, v_meta, zero_v), jnp.where(v_a, zero_v, v_meta)])
    return jnp.pad(kp, pad), jnp.pad(vp, pad)


def _attn_call(q, k, v, prefix, tk, rb):
    nb, seq, _ = q.shape
    n_prefix = 0 if prefix is None else N_META
    in_specs = [pl.BlockSpec((1, seq, PAIR_QK), lambda b, p: (b, 0, p)),
                pl.BlockSpec((1, seq, PAIR_QK), lambda b, p: (b, 0, p)),
                pl.BlockSpec((1, seq, PAIR_QK), lambda b, p: (b, 0, p))]
    args = [q, k, v]
    if n_prefix:
        in_specs += [pl.BlockSpec((LANES, PAIR_QK), lambda b, p: (0, p)),
                     pl.BlockSpec((LANES, PAIR_QK), lambda b, p: (0, p))]
        args += list(_pair_prefix(*prefix))
    return pl.pallas_call(
        functools.partial(_attn_kernel, tk=tk, rb=rb, n_prefix=n_prefix),
        grid=(nb, N_PAIRS), in_specs=in_specs,
        out_specs=pl.BlockSpec((1, seq, PAIR_V), lambda b, p: (b, 0, p)),
        out_shape=jax.ShapeDtypeStruct((nb, seq, V_WIDTH), BF16),
        scratch_shapes=[pltpu.VMEM((2, seq, LANES), F32), pltpu.VMEM((2, seq, PAIR_V), F32)],
        compiler_params=_params(),
        name="attn",
    )(*args)


def _merge_kernel(x_ref, o_ref, g_ref, lng_ref, lnb_ref, wbm_ref, wbf_ref, wout_ref, mixg_ref,
                  mixb_ref, h_ref):
    half = V_WIDTH // 2
    merged = (g_ref[0, :, :D_MODEL].astype(F32) * _dot(o_ref[0, :, :half], wbm_ref[...])
              + g_ref[0, :, D_MODEL:].astype(F32) * _dot(o_ref[0, :, half:], wbf_ref[...]))
    mixed = _dot(merged.astype(BF16), wout_ref[...])
    h0 = _layer_norm(x_ref[0], lng_ref[...], lnb_ref[...])
    h_ref[0] = _layer_norm(DN_ALPHA * h0 + mixed, mixg_ref[...], mixb_ref[...])


def _merge_call(x3, o, g, w, tm):
    nb, seq, _ = x3.shape
    row_blk = lambda width: pl.BlockSpec((1, tm, width), lambda b, j: (b, j, 0))
    consts = [w["ln_emb_g"], w["ln_emb_b"], w["w_bm"], w["w_bf"], w["w_out"], w["ln_mix_g"],
              w["ln_mix_b"]]
    return pl.pallas_call(
        _merge_kernel, grid=(nb, seq // tm),
        in_specs=[row_blk(D_MODEL), row_blk(V_WIDTH), row_blk(2 * D_MODEL)]
        + [_resident(a.shape) for a in consts],
        out_specs=row_blk(D_MODEL), out_shape=jax.ShapeDtypeStruct((nb, seq, D_MODEL), F32),
        compiler_params=_params(), name="merge",
    )(x3, o, g, *consts)


def _ffn_kernel(h_ref, halo0_ref, wup_ref, cw_ref, cb_ref, wdown_ref, lng_ref, lnb_ref,
                out_ref, tail_ref, halo_ref, gate_ref, act_ref):
    @pl.when(pl.program_id(1) == 0)
    def _():
        halo_ref[...] = halo0_ref[...]

    h = h_ref[0]
    hb = h.astype(BF16)
    rows = h.shape[0]
    for c in range(N_FF_CHUNKS):
        sl = slice(c * FF_CHUNK, (c + 1) * FF_CHUNK)
        gate = _dot(hb, wup_ref[:, sl])
        val = _dot(hb, wup_ref[:, D_FF + c * FF_CHUNK:D_FF + (c + 1) * FF_CHUNK])
        gate_ref[c, :HALO_ROWS, :] = halo_ref[:, sl]
        gate_ref[c, HALO_ROWS:, :] = gate
        halo_ref[:, sl] = gate[rows - HALO_ROWS:, :]
        back1 = gate_ref[c, HALO_ROWS - 1:HALO_ROWS - 1 + rows, :]
        back2 = gate_ref[c, HALO_ROWS - 2:HALO_ROWS - 2 + rows, :]
        conv = (cw_ref[0:1, sl] * back2 + cw_ref[1:2, sl] * back1 + cw_ref[2:3, sl] * gate
                + cb_ref[:, sl])
        act_ref[:, sl] = (conv * (1.0 / (1.0 + jnp.exp(-conv))) * val).astype(BF16)
    ffn = _dot(act_ref[...], wdown_ref[...])
    out_ref[0] = _layer_norm(DN_ALPHA * h + ffn, lng_ref[...], lnb_ref[...])
    tail_ref[0] = halo_ref[...]


def _ffn_call(h1, halo0, w, tm):
    nb, seq, _ = h1.shape
    row_blk = pl.BlockSpec((1, tm, D_MODEL), lambda b, j: (b, j, 0))
    consts = [halo0, w["w_up"], w["conv_w"], w["conv_b"], w["w_down"], w["ln_ffn_g"],
              w["ln_ffn_b"]]
    return pl.pallas_call(
        _ffn_kernel, grid=(nb, seq // tm),
        in_specs=[row_blk] + [_resident(a.shape) for a in consts],
        out_specs=(row_blk, pl.BlockSpec((1, HALO_ROWS, D_FF), lambda b, j: (b, 0, 0))),
        out_shape=(jax.ShapeDtypeStruct((nb, seq, D_MODEL), F32),
                   jax.ShapeDtypeStruct((nb, HALO_ROWS, D_FF), F32)),
        scratch_shapes=[pltpu.VMEM((HALO_ROWS, D_FF), F32),
                        pltpu.VMEM((N_FF_CHUNKS, HALO_ROWS + tm, FF_CHUNK), F32),
                        pltpu.VMEM((tm, D_FF), BF16)],
        compiler_params=_params(), name="ffn",
    )(h1, *consts)


def _rope_tables(length):
    half = MLA_ROPE_DIM // 2
    inv_freq = ROPE_THETA ** (-jnp.arange(half, dtype=F32) / half)
    ang = jnp.arange(length).astype(F32)[:, None] * inv_freq[None, :]
    cos, sin = jnp.cos(ang), jnp.sin(ang)
    zero = jnp.zeros_like(cos)
    widen = lambda a, b: jnp.pad(jnp.tile(jnp.concatenate([a, b], axis=1), (1, 2)),
                                 ((0, 0), (0, LANES - 2 * MLA_ROPE_DIM)))
    return widen(cos, cos), widen(zero, sin), widen(-sin, zero)


def _bias_selectors():
    half = (N_PAIRS // 2) * LANES
    sel = np.zeros((LANES, 2 * half), np.float32)
    const = np.zeros((1, 2 * half), np.float32)
    for h in range(FOX_HEADS):
        base = LANES * (h // 2) + EXTRA_GROUP * (h % 2)
        for piece in range(3):
            sel[piece * FOX_HEADS + h, base + piece] = 1.0
            const[0, base + 3 + piece] = 1.0
            const[0, half + base + piece] = 1.0
            sel[piece * FOX_HEADS + h, half + base + 3 + piece] = -1.0
    return jnp.asarray(sel, BF16), jnp.asarray(const)


def _prepare_weights(w_in, b_gate, b_forget, q_norm_g, w_q_up, kv_norm_g, w_kv_up, w_branch_mla,
                     w_branch_fox, w_out, w_ffn_up, conv_w, conv_b, w_ffn_down):
    n_half = N_PAIRS // 2
    o_qlat, o_kvlat, o_krope = 0, MLA_Q_RANK, MLA_Q_RANK + MLA_KV_RANK
    o_fq = o_krope + MLA_ROPE_DIM
    o_fk, o_fv = o_fq + FOX_WIDTH, o_fq + 2 * FOX_WIDTH
    o_flogit = o_fv + FOX_WIDTH
    o_gate = o_flogit + FOX_HEADS
    w_in = w_in.astype(BF16)
    k_rope = w_in[:, o_krope:o_fq]
    w_in_p = jnp.concatenate([
        w_in[:, o_qlat:o_krope],
        k_rope, k_rope, jnp.zeros((D_MODEL, LANES - 2 * MLA_ROPE_DIM), BF16),
        w_in[:, o_flogit:o_gate], jnp.zeros((D_MODEL, LANES - FOX_HEADS), BF16),
        w_in[:, o_fq:o_flogit],
        w_in[:, o_gate:],
    ], axis=1)
    wq = w_q_up.reshape(MLA_Q_RANK, n_half, 2, MLA_QK_DIM)
    wq_p = jnp.concatenate([
        wq[..., :MLA_NOPE_DIM].reshape(MLA_Q_RANK, n_half, LANES),
        wq[..., MLA_NOPE_DIM:].reshape(MLA_Q_RANK, n_half, 2 * MLA_ROPE_DIM),
        jnp.zeros((MLA_Q_RANK, n_half, LANES - 2 * MLA_ROPE_DIM), F32)], axis=-1)
    wkv = w_kv_up.reshape(MLA_KV_RANK, MLA_HEADS, MLA_NOPE_DIM + MLA_V_DIM)
    return {
        "w_in": w_in_p,
        "b_gate": b_gate.reshape(1, -1),
        "b_forget": jnp.pad(b_forget.reshape(1, -1), ((0, 0), (0, LANES - FOX_HEADS))),
        "q_norm_g": q_norm_g.reshape(1, -1),
        "w_q": wq_p.reshape(MLA_Q_RANK, n_half * PAIR_QK).astype(BF16),
        "kv_norm_g": kv_norm_g.reshape(1, -1),
        "w_k": wkv[..., :MLA_NOPE_DIM].reshape(MLA_KV_RANK, -1).astype(BF16),
        "w_v": wkv[..., MLA_NOPE_DIM:].reshape(MLA_KV_RANK, -1).astype(BF16),
        "w_bm": w_branch_mla.astype(BF16),
        "w_bf": w_branch_fox.astype(BF16),
        "w_out": w_out.astype(BF16),
        "w_up": w_ffn_up.astype(BF16),
        "conv_w": conv_w,
        "conv_b": conv_b.reshape(1, -1),
        "w_down": w_ffn_down.astype(BF16),
    }


def _layer(x3, tabs, cum0, halo0, prefix, w, tm, tk, rb):
    q, k, v, gates, cum_end = _proj_call(x3, tabs, cum0, w, tm)
    o = _attn_call(q, k, v, prefix, tk, rb)
    h1 = _merge_call(x3, o, gates, w, tm)
    out, tail = _ffn_call(h1, halo0, w, tm)
    return out, (k[0], v[0]), cum_end[0], tail[0]


def kernel(x, meta_tokens, ln_emb_g, ln_emb_b, w_in, b_gate, b_forget, q_norm_g, w_q_up, kv_norm_g, w_kv_up, w_branch_mla, w_branch_fox, w_out, ln_mix_g, ln_mix_b, w_ffn_up, conv_w, conv_b, w_ffn_down, ln_ffn_g, ln_ffn_b):
    seq = x.shape[1]
    w = _prepare_weights(w_in[0], b_gate[0], b_forget[0], q_norm_g[0], w_q_up[0], kv_norm_g[0],
                         w_kv_up[0], w_branch_mla[0], w_branch_fox[0], w_out[0], w_ffn_up[0],
                         conv_w[0], conv_b[0], w_ffn_down[0])
    w.update(ln_emb_g=ln_emb_g.reshape(1, -1), ln_emb_b=ln_emb_b.reshape(1, -1),
             ln_mix_g=ln_mix_g[0].reshape(1, -1), ln_mix_b=ln_mix_b[0].reshape(1, -1),
             ln_ffn_g=ln_ffn_g[0].reshape(1, -1), ln_ffn_b=ln_ffn_b[0].reshape(1, -1))
    w["sel"], w["sel_const"] = _bias_selectors()
    tabs = _rope_tables(N_META + seq)
    meta_tabs = tuple(t[:N_META] for t in tabs)
    real_tabs = tuple(t[N_META:] for t in tabs)

    _, meta_kv, meta_cum, meta_tail = _layer(
        meta_tokens[None].astype(x.dtype), meta_tabs, jnp.zeros((HALO_ROWS, LANES), F32),
        jnp.zeros((HALO_ROWS, D_FF), F32), None, w, N_META, N_META, N_META)
    out, _, _, _ = _layer(x, real_tabs, meta_cum, meta_tail, meta_kv, w, 512, 256, 128)
    return out
```

```python
import functools

import numpy as np
import jax
import jax.numpy as jnp
from jax import lax
from jax.experimental import pallas as pl
from jax.experimental.pallas import tpu as pltpu

D_MODEL = 1024
N_META = 16
MLA_HEADS = 8
MLA_Q_RANK = 384
MLA_KV_RANK = 128
MLA_NOPE_DIM = 64
MLA_ROPE_DIM = 32
MLA_QK_DIM = MLA_NOPE_DIM + MLA_ROPE_DIM
MLA_V_DIM = 64
ROPE_THETA = 10000.0
FOX_HEADS = 8
FOX_HEAD_DIM = 64
FOX_WIDTH = FOX_HEADS * FOX_HEAD_DIM
D_FF = 2816
LN_EPS = 1e-5
RMS_EPS = 1e-6
DEPTH = 1
DN_ALPHA = (2 * DEPTH) ** 0.25
NEG_INF = -1e30
LOG2E = 1.4426950408889634

LANES = 128
N_PAIRS = 8
PAIR_QK = 2 * LANES
PAIR_V = 2 * MLA_V_DIM
QK_WIDTH = N_PAIRS * PAIR_QK
V_WIDTH = N_PAIRS * PAIR_V
VX_WIDTH = N_PAIRS * PAIR_QK
EXTRA_GROUP = 32
FF_CHUNK = 256
N_FF_CHUNKS = D_FF // FF_CHUNK
HALO_ROWS = 8
SUB_ROWS = 256
VMEM_LIMIT = 56 * 1024 * 1024

C_QLAT = 0
C_KVLAT = C_QLAT + MLA_Q_RANK
C_KROPE = C_KVLAT + MLA_KV_RANK
C_FLOGIT = C_KROPE + LANES
C_FQ = C_FLOGIT + LANES
C_FK = C_FQ + FOX_WIDTH
C_FV = C_FK + FOX_WIDTH
C_GATE = C_FV + FOX_WIDTH
C_TOTAL = C_GATE + 2 * D_MODEL

F32 = jnp.float32
BF16 = jnp.bfloat16


def _dot(a, b):
    return jnp.dot(a, b, preferred_element_type=F32)


def _layer_norm(x, g, b):
    mu = jnp.mean(x, axis=-1, keepdims=True)
    xc = x - mu
    var = jnp.mean(xc * xc, axis=-1, keepdims=True)
    return xc * lax.rsqrt(var + LN_EPS) * g + b


def _rms_norm(x, g):
    return x * lax.rsqrt(jnp.mean(x * x, axis=-1, keepdims=True) + RMS_EPS) * g


def _resident(shape):
    zeros = (0,) * len(shape)
    return pl.BlockSpec(shape, lambda *_: zeros, pipeline_mode=pl.Buffered(1))


def _params():
    return pltpu.CompilerParams(dimension_semantics=("arbitrary", "arbitrary"),
                                vmem_limit_bytes=VMEM_LIMIT)


def _rope(t, cos_t, sa_t, sb_t):
    return (t * cos_t + pltpu.roll(t, MLA_ROPE_DIM // 2, 1) * sa_t
            + pltpu.roll(t, LANES - MLA_ROPE_DIM // 2, 1) * sb_t)


def _cumsum_rows(x):
    rows = x.shape[0]
    row = lax.broadcasted_iota(jnp.int32, x.shape, 0)
    step = 1
    while step < rows:
        x = x + jnp.where(row >= step, pltpu.roll(x, step, 0), 0.0)
        step *= 2
    return x


def _store_values(v_ref, first_pair, v):
    low_half = lax.broadcasted_iota(jnp.int32, (1, LANES), 1) < MLA_V_DIM
    one = jnp.ones((v.shape[0], LANES), v.dtype)
    for p in range(N_PAIRS // 2):
        pair = v[:, p * LANES:(p + 1) * LANES]
        col = (first_pair + p) * PAIR_QK
        v_ref[0, :, col:col + LANES] = jnp.where(low_half, pair, one)
        v_ref[0, :, col + LANES:col + PAIR_QK] = jnp.where(low_half, one, pair)


def _proj_kernel(x_ref, lng_ref, lnb_ref, win_ref, bgate_ref, bforget_ref, qng_ref, wq_ref,
                 kvng_ref, wk_ref, wv_ref, cos_ref, sa_ref, sb_ref, sel_ref, selc_ref, cum0_ref,
                 q_ref, k_ref, v_ref, g_ref, cumend_ref,
                 carry_ref):
    @pl.when(pl.program_id(1) == 0)
    def _():
        carry_ref[...] = cum0_ref[...]

    hb = _layer_norm(x_ref[0], lng_ref[...], lnb_ref[...]).astype(BF16)
    cos_t, sa_t, sb_t = cos_ref[...], sa_ref[...], sb_ref[...]
    mla_scale = MLA_QK_DIM ** -0.5 * LOG2E
    fox_scale = FOX_HEAD_DIM ** -0.5 * LOG2E
    n_half = N_PAIRS // 2

    lat = _dot(hb, win_ref[:, C_QLAT:C_KROPE])
    small = _dot(hb, win_ref[:, C_KROPE:C_FQ])
    gate_chunk = 512
    for c in range(2 * D_MODEL // gate_chunk):
        sl = slice(c * gate_chunk, (c + 1) * gate_chunk)
        zg = _dot(hb, win_ref[:, C_GATE + c * gate_chunk:C_GATE + (c + 1) * gate_chunk])
        zg = zg + bgate_ref[:, sl]
        g_ref[0, :, sl] = (1.0 / (1.0 + jnp.exp(-zg))).astype(BF16)

    qn = _rms_norm(lat[:, :MLA_Q_RANK], qng_ref[...]).astype(BF16)
    kvn = _rms_norm(lat[:, MLA_Q_RANK:], kvng_ref[...]).astype(BF16)
    for p in range(n_half):
        qp = _dot(qn, wq_ref[:, p * PAIR_QK:(p + 1) * PAIR_QK]) * mla_scale
        q_ref[0, :, p * PAIR_QK:p * PAIR_QK + LANES] = qp[:, :LANES].astype(BF16)
        q_ref[0, :, p * PAIR_QK + LANES:(p + 1) * PAIR_QK] = _rope(
            qp[:, LANES:], cos_t, sa_t, sb_t).astype(BF16)
    k_pe = _rope(small[:, :LANES], cos_t, sa_t, sb_t).astype(BF16)
    k_nope = _dot(kvn, wk_ref[...]).astype(BF16)
    for p in range(n_half):
        k_ref[0, :, p * PAIR_QK:p * PAIR_QK + LANES] = k_nope[:, p * LANES:(p + 1) * LANES]
        k_ref[0, :, p * PAIR_QK + LANES:(p + 1) * PAIR_QK] = k_pe
    _store_values(v_ref, 0, _dot(kvn, wv_ref[...]).astype(BF16))

    z = small[:, LANES:] + bforget_ref[...]
    log_f = -(jnp.maximum(-z, 0.0) + jnp.log1p(jnp.exp(-jnp.abs(z))))
    cum = _cumsum_rows(log_f) + carry_ref[0:1, :]
    rows = cum.shape[0]
    carry_ref[...] = jnp.broadcast_to(cum[rows - 1:rows, :], carry_ref.shape)
    cumend_ref[0] = jnp.broadcast_to(cum[rows - 1:rows, :], carry_ref.shape)
    head_lane = lax.broadcasted_iota(jnp.int32, (1, LANES), 1) < FOX_HEADS
    cum = jnp.where(head_lane, cum * LOG2E, 0.0)
    c_hi = cum.astype(BF16).astype(F32)
    r1 = cum - c_hi
    c_mid = r1.astype(BF16).astype(F32)
    c_lo = (r1 - c_mid).astype(BF16).astype(F32)
    pieces = (c_hi + pltpu.roll(c_mid, FOX_HEADS, 1) + pltpu.roll(c_lo, 2 * FOX_HEADS, 1))
    aug = (_dot(pieces.astype(BF16), sel_ref[...]) + selc_ref[...]).astype(BF16)
    base = n_half * PAIR_QK
    for p in range(n_half):
        lo_col = base + p * PAIR_QK + LANES
        q_ref[0, :, lo_col:lo_col + LANES] = aug[:, p * LANES:(p + 1) * LANES]
        k_ref[0, :, lo_col:lo_col + LANES] = aug[:, (n_half + p) * LANES:(n_half + p + 1) * LANES]

    fq = (_dot(hb, win_ref[:, C_FQ:C_FQ + FOX_WIDTH]) * fox_scale).astype(BF16)
    fk = _dot(hb, win_ref[:, C_FK:C_FK + FOX_WIDTH]).astype(BF16)
    for p in range(n_half):
        lo_col = base + p * PAIR_QK
        q_ref[0, :, lo_col:lo_col + LANES] = fq[:, p * LANES:(p + 1) * LANES]
        k_ref[0, :, lo_col:lo_col + LANES] = fk[:, p * LANES:(p + 1) * LANES]
    _store_values(v_ref, n_half, _dot(hb, win_ref[:, C_FV:C_FV + FOX_WIDTH]).astype(BF16))


def _proj_call(x3, tabs, cum0, w, tm):
    nb, seq, _ = x3.shape
    row_blk = lambda width: pl.BlockSpec((1, tm, width), lambda b, j: (b, j, 0))
    tab_blk = pl.BlockSpec((tm, LANES), lambda b, j: (j, 0))
    consts = [w["ln_emb_g"], w["ln_emb_b"], w["w_in"], w["b_gate"], w["b_forget"], w["q_norm_g"],
              w["w_q"], w["kv_norm_g"], w["w_k"], w["w_v"]]
    sels = [w["sel"], w["sel_const"], cum0]
    in_specs = ([row_blk(D_MODEL)] + [_resident(a.shape) for a in consts] + [tab_blk] * 3
                + [_resident(a.shape) for a in sels])
    out_shape = (jax.ShapeDtypeStruct((nb, seq, QK_WIDTH), BF16),
                 jax.ShapeDtypeStruct((nb, seq, QK_WIDTH), BF16),
                 jax.ShapeDtypeStruct((nb, seq, VX_WIDTH), BF16),
                 jax.ShapeDtypeStruct((nb, seq, 2 * D_MODEL), BF16),
                 jax.ShapeDtypeStruct((nb, HALO_ROWS, LANES), F32))
    out_specs = (row_blk(QK_WIDTH), row_blk(QK_WIDTH), row_blk(VX_WIDTH), row_blk(2 * D_MODEL),
                 pl.BlockSpec((1, HALO_ROWS, LANES), lambda b, j: (b, 0, 0)))
    return pl.pallas_call(
        _proj_kernel, grid=(nb, seq // tm), in_specs=in_specs, out_specs=out_specs,
        out_shape=out_shape, scratch_shapes=[pltpu.VMEM((HALO_ROWS, LANES), F32)],
        compiler_params=_params(), name="proj",
    )(x3, *consts, *tabs, *sels)


def _attn_kernel(*refs, tk, rb, n_prefix):
    if n_prefix:
        q_ref, k_ref, v_ref, kp_ref, vp_ref, o_ref, m_ref, acc_ref = refs
    else:
        q_ref, k_ref, v_ref, o_ref, m_ref, acc_ref = refs
    q = q_ref[0]
    tq = q.shape[0]
    lane = lax.broadcasted_iota(jnp.int32, (1, PAIR_QK), 1)
    head_a = (lane < MLA_NOPE_DIM) | ((lane >= LANES) & (lane < LANES + EXTRA_GROUP))
    head_b = (((lane >= MLA_NOPE_DIM) & (lane < LANES))
              | ((lane >= LANES + EXTRA_GROUP) & (lane < LANES + 2 * EXTRA_GROUP)))
    zero = jnp.zeros_like(q)
    qs = (jnp.where(head_a, q, zero), jnp.where(head_b, q, zero))
    nt_dims = (((1,), (1,)), ((), ()))
    key_lane = lax.broadcasted_iota(jnp.int32, (1, LANES), 1)

    if n_prefix:
        grp_a = key_lane < n_prefix
        grp_b = (key_lane >= n_prefix) & (key_lane < 2 * n_prefix)
        s_all = lax.dot_general(q, kp_ref[...], nt_dims, preferred_element_type=F32)
        for r0 in range(0, tq, rb):
            rows = slice(r0, r0 + rb)
            s = s_all[rows, :]
            m_a = jnp.max(jnp.where(grp_a, s, NEG_INF), axis=1, keepdims=True)
            m_b = jnp.max(jnp.where(grp_b, s, NEG_INF), axis=1, keepdims=True)
            x = jnp.where(grp_a, s - m_a, jnp.where(grp_b, s - m_b, NEG_INF))
            p = jnp.exp2(x.astype(BF16))
            for h, m_h in ((0, m_a), (1, m_b)):
                m_ref[h, rows, :] = jnp.broadcast_to(m_h, (rb, LANES))
                acc_ref[h, rows, :] = _dot(p, vp_ref[:, h * LANES:(h + 1) * LANES])
    else:
        m_ref[...] = jnp.full(m_ref.shape, NEG_INF, F32)
        acc_ref[...] = jnp.zeros(acc_ref.shape, F32)

    def scores(h, key0):
        row0 = (key0 // rb) * rb
        return lax.dot_general(qs[h][row0:, :], k_ref[0, key0:key0 + tk, :], nt_dims,
                               preferred_element_type=F32)

    def update(key0, s_pair):
        row0 = (key0 // rb) * rb
        step = min(tk, LANES)
        tiles = tk // step
        for r0 in range(row0, tq, rb):
            if r0 + rb <= key0:
                continue
            rows = slice(r0, r0 + rb)
            for h in range(2):
                s = s_pair[h][r0 - row0:r0 - row0 + rb, :]
                if r0 < key0 + tk - 1:
                    row = lax.broadcasted_iota(jnp.int32, (rb, tk), 0) + r0
                    col = lax.broadcasted_iota(jnp.int32, (rb, tk), 1) + key0
                    s = jnp.where(col <= row, s, NEG_INF)
                m_cur = s[:, :step]
                for t in range(1, tiles):
                    m_cur = jnp.maximum(m_cur, s[:, t * step:(t + 1) * step])
                m_prev = m_ref[h, rows, :]
                m_new = jnp.maximum(m_prev, jnp.max(m_cur, axis=1, keepdims=True))
                alpha = jnp.exp2(m_prev - m_new)
                m_wide = m_new[:, :step] if tiles == 1 else jnp.tile(m_new, (1, tiles))
                p = jnp.exp2((s - m_wide).astype(BF16))
                pv = _dot(p, v_ref[0, key0:key0 + tk, h * LANES:(h + 1) * LANES])
                acc_ref[h, rows, :] = alpha * acc_ref[h, rows, :] + pv
                m_ref[h, rows, :] = m_new

    key_tiles = list(range(0, tq, tk))
    ahead = 3
    pending = [[scores(h, key0) for h in range(2)] for key0 in key_tiles[:ahead]]
    for c, key0 in enumerate(key_tiles):
        s_cur = pending.pop(0)
        if c + ahead < len(key_tiles):
            pending.append([scores(h, key_tiles[c + ahead]) for h in range(2)])
        update(key0, s_cur)

    acc_a, acc_b = acc_ref[0], acc_ref[1]
    o = jnp.where(key_lane < MLA_V_DIM, acc_a / pltpu.roll(acc_a, MLA_V_DIM, 1),
                  acc_b / pltpu.roll(acc_b, MLA_V_DIM, 1))
    o_ref[0] = o.astype(BF16)


def _pair_prefix(k_meta, v_meta):
    n = k_meta.shape[0]
    lane = jnp.arange(QK_WIDTH) % PAIR_QK
    k_a = (lane < MLA_NOPE_DIM) | ((lane >= LANES) & (lane < LANES + EXTRA_GROUP))
    k_b = (((lane >= MLA_NOPE_DIM) & (lane < LANES))
           | ((lane >= LANES + EXTRA_GROUP) & (lane < LANES + 2 * EXTRA_GROUP)))
    v_a = lane < LANES
    zero_k, zero_v = jnp.zeros_like(k_meta), jnp.zeros_like(v_meta)
    pad = ((0, LANES - 2 * n), (0, 0))
    kp = jnp.concatenate([jnp.where(k_a, k_meta, zero_k), jnp.where(k_b, k_meta, zero_k)])
    vp = jnp.concatenate([jnp.where(v_a, v_meta, zero_v), jnp.where(v_a, zero_v, v_meta)])
    return jnp.pad(kp, pad), jnp.pad(vp, pad)


def _attn_call(q, k, v, prefix, tk, rb):
    nb, seq, _ = q.shape
    n_prefix = 0 if prefix is None else N_META
    in_specs = [pl.BlockSpec((1, seq, PAIR_QK), lambda b, p: (b, 0, p)),
                pl.BlockSpec((1, seq, PAIR_QK), lambda b, p: (b, 0, p)),
                pl.BlockSpec((1, seq, PAIR_QK), lambda b, p: (b, 0, p))]
    args = [q, k, v]
    if n_prefix:
        in_specs += [pl.BlockSpec((LANES, PAIR_QK), lambda b, p: (0, p)),
                     pl.BlockSpec((LANES, PAIR_QK), lambda b, p: (0, p))]
        args += list(_pair_prefix(*prefix))
    return pl.pallas_call(
        functools.partial(_attn_kernel, tk=tk, rb=rb, n_prefix=n_prefix),
        grid=(nb, N_PAIRS), in_specs=in_specs,
        out_specs=pl.BlockSpec((1, seq, PAIR_V), lambda b, p: (b, 0, p)),
        out_shape=jax.ShapeDtypeStruct((nb, seq, V_WIDTH), BF16),
        scratch_shapes=[pltpu.VMEM((2, seq, LANES), F32), pltpu.VMEM((2, seq, PAIR_V), F32)],
        compiler_params=_params(),
        name="attn",
    )(*args)


def _merge_kernel(x_ref, o_ref, g_ref, lng_ref, lnb_ref, wbm_ref, wbf_ref, wout_ref, mixg_ref,
                  mixb_ref, h_ref):
    half = V_WIDTH // 2
    n_rows = x_ref.shape[1]
    sub = min(n_rows, SUB_ROWS)
    parts = [slice(r, r + sub) for r in range(0, n_rows, sub)]

    def branches(rows):
        return (_dot(o_ref[0, rows, :half], wbm_ref[...]),
                _dot(o_ref[0, rows, half:], wbf_ref[...]))

    def mix(rows, o_mla, o_fox):
        merged = (g_ref[0, rows, :D_MODEL].astype(F32) * o_mla
                  + g_ref[0, rows, D_MODEL:].astype(F32) * o_fox)
        return _dot(merged.astype(BF16), wout_ref[...])

    def finish(rows, mixed):
        h0 = _layer_norm(x_ref[0, rows, :], lng_ref[...], lnb_ref[...])
        h_ref[0, rows, :] = _layer_norm(DN_ALPHA * h0 + mixed, mixg_ref[...], mixb_ref[...])

    projected = [branches(rows) for rows in parts]
    mixed = [mix(rows, *pr) for rows, pr in zip(parts, projected)]
    for rows, mx in zip(parts, mixed):
        finish(rows, mx)


def _merge_call(x3, o, g, w, tm):
    nb, seq, _ = x3.shape
    row_blk = lambda width: pl.BlockSpec((1, tm, width), lambda b, j: (b, j, 0))
    consts = [w["ln_emb_g"], w["ln_emb_b"], w["w_bm"], w["w_bf"], w["w_out"], w["ln_mix_g"],
              w["ln_mix_b"]]
    return pl.pallas_call(
        _merge_kernel, grid=(nb, seq // tm),
        in_specs=[row_blk(D_MODEL), row_blk(V_WIDTH), row_blk(2 * D_MODEL)]
        + [_resident(a.shape) for a in consts],
        out_specs=row_blk(D_MODEL), out_shape=jax.ShapeDtypeStruct((nb, seq, D_MODEL), F32),
        compiler_params=_params(), name="merge",
    )(x3, o, g, *consts)


def _ffn_kernel(h_ref, halo0_ref, wup_ref, cw_ref, cb_ref, wdown_ref, lng_ref, lnb_ref,
                out_ref, tail_ref, halo_ref, gate_ref, act_ref):
    @pl.when(pl.program_id(1) == 0)
    def _():
        halo_ref[...] = halo0_ref[...]

    n_rows = h_ref.shape[1]
    sub = min(n_rows, SUB_ROWS)

    def gated(r0):
        rows = slice(r0, r0 + sub)
        hb = h_ref[0, rows, :].astype(BF16)
        for c in range(N_FF_CHUNKS):
            sl = slice(c * FF_CHUNK, (c + 1) * FF_CHUNK)
            gate = _dot(hb, wup_ref[:, sl])
            val = _dot(hb, wup_ref[:, D_FF + c * FF_CHUNK:D_FF + (c + 1) * FF_CHUNK])
            gate_ref[c, :HALO_ROWS, :] = halo_ref[:, sl]
            gate_ref[c, HALO_ROWS:, :] = gate
            halo_ref[:, sl] = gate[sub - HALO_ROWS:, :]
            back1 = gate_ref[c, HALO_ROWS - 1:HALO_ROWS - 1 + sub, :]
            back2 = gate_ref[c, HALO_ROWS - 2:HALO_ROWS - 2 + sub, :]
            conv = (cw_ref[0:1, sl] * back2 + cw_ref[1:2, sl] * back1 + cw_ref[2:3, sl] * gate
                    + cb_ref[:, sl])
            act_ref[rows, sl] = (conv * (1.0 / (1.0 + jnp.exp(-conv))) * val).astype(BF16)

    starts = list(range(0, n_rows, sub))
    for r0 in starts:
        gated(r0)
    down = [_dot(act_ref[r0:r0 + sub, :], wdown_ref[...]) for r0 in starts]
    for r0, ffn in zip(starts, down):
        rows = slice(r0, r0 + sub)
        out_ref[0, rows, :] = _layer_norm(DN_ALPHA * h_ref[0, rows, :] + ffn, lng_ref[...],
                                          lnb_ref[...])
    tail_ref[0] = halo_ref[...]


def _ffn_call(h1, halo0, w, tm):
    nb, seq, _ = h1.shape
    row_blk = pl.BlockSpec((1, tm, D_MODEL), lambda b, j: (b, j, 0))
    consts = [halo0, w["w_up"], w["conv_w"], w["conv_b"], w["w_down"], w["ln_ffn_g"],
              w["ln_ffn_b"]]
    return pl.pallas_call(
        _ffn_kernel, grid=(nb, seq // tm),
        in_specs=[row_blk] + [_resident(a.shape) for a in consts],
        out_specs=(row_blk, pl.BlockSpec((1, HALO_ROWS, D_FF), lambda b, j: (b, 0, 0))),
        out_shape=(jax.ShapeDtypeStruct((nb, seq, D_MODEL), F32),
                   jax.ShapeDtypeStruct((nb, HALO_ROWS, D_FF), F32)),
        scratch_shapes=[pltpu.VMEM((HALO_ROWS, D_FF), F32),
                        pltpu.VMEM((N_FF_CHUNKS, HALO_ROWS + min(tm, SUB_ROWS), FF_CHUNK), F32),
                        pltpu.VMEM((tm, D_FF), BF16)],
        compiler_params=_params(), name="ffn",
    )(h1, *consts)


def _rope_tables(length):
    half = MLA_ROPE_DIM // 2
    inv_freq = ROPE_THETA ** (-jnp.arange(half, dtype=F32) / half)
    ang = jnp.arange(length).astype(F32)[:, None] * inv_freq[None, :]
    cos, sin = jnp.cos(ang), jnp.sin(ang)
    zero = jnp.zeros_like(cos)
    widen = lambda a, b: jnp.pad(jnp.tile(jnp.concatenate([a, b], axis=1), (1, 2)),
                                 ((0, 0), (0, LANES - 2 * MLA_ROPE_DIM)))
    return widen(cos, cos), widen(zero, sin), widen(-sin, zero)


def _bias_selectors():
    half = (N_PAIRS // 2) * LANES
    sel = np.zeros((LANES, 2 * half), np.float32)
    const = np.zeros((1, 2 * half), np.float32)
    for h in range(FOX_HEADS):
        base = LANES * (h // 2) + EXTRA_GROUP * (h % 2)
        for piece in range(3):
            sel[piece * FOX_HEADS + h, base + piece] = 1.0
            const[0, base + 3 + piece] = 1.0
            const[0, half + base + piece] = 1.0
            sel[piece * FOX_HEADS + h, half + base + 3 + piece] = -1.0
    return jnp.asarray(sel, BF16), jnp.asarray(const)


def _w_in_kernel(w_ref, o_ref):
    w = w_ref[...]
    lane = lax.broadcasted_iota(jnp.int32, (1, LANES), 1)
    o_krope = MLA_Q_RANK + MLA_KV_RANK
    o_fq = o_krope + MLA_ROPE_DIM
    o_flogit = o_fq + 3 * FOX_WIDTH
    o_gate = o_flogit + FOX_HEADS
    o_ref[:, C_QLAT:C_KROPE] = w[:, :o_krope].astype(BF16)
    rope_tile = w[:, o_krope:o_krope + LANES]
    rope_pair = jnp.where(lane < MLA_ROPE_DIM, rope_tile,
                          pltpu.roll(rope_tile, MLA_ROPE_DIM, 1))
    o_ref[:, C_KROPE:C_FLOGIT] = jnp.where(lane < 2 * MLA_ROPE_DIM, rope_pair,
                                           0.0).astype(BF16)
    tile0 = (o_flogit // LANES) * LANES
    logit_tile = pltpu.roll(w[:, tile0:tile0 + LANES], LANES - (o_flogit - tile0), 1)
    o_ref[:, C_FLOGIT:C_FQ] = jnp.where(lane < FOX_HEADS, logit_tile, 0.0).astype(BF16)
    o_ref[:, C_FQ:C_GATE] = w[:, o_fq:o_flogit].astype(BF16)
    o_ref[:, C_GATE:] = w[:, o_gate:].astype(BF16)


def _regroup_w_in(w_in):
    rows = 256
    return pl.pallas_call(
        _w_in_kernel, grid=(D_MODEL // rows,),
        in_specs=[pl.BlockSpec((rows, w_in.shape[1]), lambda i: (i, 0))],
        out_specs=pl.BlockSpec((rows, C_TOTAL), lambda i: (i, 0)),
        out_shape=jax.ShapeDtypeStruct((D_MODEL, C_TOTAL), BF16),
        compiler_params=pltpu.CompilerParams(dimension_semantics=("arbitrary",),
                                             vmem_limit_bytes=VMEM_LIMIT),
        name="w_in_regroup",
    )(w_in)


def _prepare_weights(w_in, b_gate, b_forget, q_norm_g, w_q_up, kv_norm_g, w_kv_up, w_branch_mla,
                     w_branch_fox, w_out, w_ffn_up, conv_w, conv_b, w_ffn_down):
    n_half = N_PAIRS // 2
    w_in_p = _regroup_w_in(w_in)
    wq = w_q_up.reshape(MLA_Q_RANK, n_half, 2, MLA_QK_DIM)
    wq_p = jnp.concatenate([
        wq[..., :MLA_NOPE_DIM].reshape(MLA_Q_RANK, n_half, LANES),
        wq[..., MLA_NOPE_DIM:].reshape(MLA_Q_RANK, n_half, 2 * MLA_ROPE_DIM),
        jnp.zeros((MLA_Q_RANK, n_half, LANES - 2 * MLA_ROPE_DIM), F32)], axis=-1)
    wkv = w_kv_up.reshape(MLA_KV_RANK, MLA_HEADS, MLA_NOPE_DIM + MLA_V_DIM)
    return {
        "w_in": w_in_p,
        "b_gate": b_gate.reshape(1, -1),
        "b_forget": jnp.pad(b_forget.reshape(1, -1), ((0, 0), (0, LANES - FOX_HEADS))),
        "q_norm_g": q_norm_g.reshape(1, -1),
        "w_q": wq_p.reshape(MLA_Q_RANK, n_half * PAIR_QK).astype(BF16),
        "kv_norm_g": kv_norm_g.reshape(1, -1),
        "w_k": wkv[..., :MLA_NOPE_DIM].reshape(MLA_KV_RANK, -1).astype(BF16),
        "w_v": wkv[..., MLA_NOPE_DIM:].reshape(MLA_KV_RANK, -1).astype(BF16),
        "w_bm": w_branch_mla.astype(BF16),
        "w_bf": w_branch_fox.astype(BF16),
        "w_out": w_out.astype(BF16),
        "w_up": w_ffn_up.astype(BF16),
        "conv_w": conv_w,
        "conv_b": conv_b.reshape(1, -1),
        "w_down": w_ffn_down.astype(BF16),
    }


def _layer(x3, tabs, cum0, halo0, prefix, w, tm, tk, rb):
    q, k, v, gates, cum_end = _proj_call(x3, tabs, cum0, w, tm)
    o = _attn_call(q, k, v, prefix, tk, rb)
    h1 = _merge_call(x3, o, gates, w, min(2 * tm, x3.shape[1]))
    out, tail = _ffn_call(h1, halo0, w, min(2 * tm, x3.shape[1]))
    return out, (k[0], v[0]), cum_end[0], tail[0]


def kernel(x, meta_tokens, ln_emb_g, ln_emb_b, w_in, b_gate, b_forget, q_norm_g, w_q_up, kv_norm_g, w_kv_up, w_branch_mla, w_branch_fox, w_out, ln_mix_g, ln_mix_b, w_ffn_up, conv_w, conv_b, w_ffn_down, ln_ffn_g, ln_ffn_b):
    seq = x.shape[1]
    w = _prepare_weights(w_in[0], b_gate[0], b_forget[0], q_norm_g[0], w_q_up[0], kv_norm_g[0],
                         w_kv_up[0], w_branch_mla[0], w_branch_fox[0], w_out[0], w_ffn_up[0],
                         conv_w[0], conv_b[0], w_ffn_down[0])
    w.update(ln_emb_g=ln_emb_g.reshape(1, -1), ln_emb_b=ln_emb_b.reshape(1, -1),
             ln_mix_g=ln_mix_g[0].reshape(1, -1), ln_mix_b=ln_mix_b[0].reshape(1, -1),
             ln_ffn_g=ln_ffn_g[0].reshape(1, -1), ln_ffn_b=ln_ffn_b[0].reshape(1, -1))
    w["sel"], w["sel_const"] = _bias_selectors()
    tabs = _rope_tables(N_META + seq)
    meta_tabs = tuple(t[:N_META] for t in tabs)
    real_tabs = tuple(t[N_META:] for t in tabs)

    _, meta_kv, meta_cum, meta_tail = _layer(
        meta_tokens[None].astype(x.dtype), meta_tabs, jnp.zeros((HALO_ROWS, LANES), F32),
        jnp.zeros((HALO_ROWS, D_FF), F32), None, w, N_META, N_META, N_META)
    out, _, _, _ = _layer(x, real_tabs, meta_cum, meta_tail, meta_kv, w, 512, 256, 128)
    return out
```

```python
import functools

import numpy as np
import jax
import jax.numpy as jnp
from jax import lax
from jax.experimental import pallas as pl
from jax.experimental.pallas import tpu as pltpu

D_MODEL = 1024
N_META = 16
MLA_HEADS = 8
MLA_Q_RANK = 384
MLA_KV_RANK = 128
MLA_NOPE_DIM = 64
MLA_ROPE_DIM = 32
MLA_QK_DIM = MLA_NOPE_DIM + MLA_ROPE_DIM
MLA_V_DIM = 64
ROPE_THETA = 10000.0
FOX_HEADS = 8
FOX_HEAD_DIM = 64
FOX_WIDTH = FOX_HEADS * FOX_HEAD_DIM
D_FF = 2816
LN_EPS = 1e-5
RMS_EPS = 1e-6
DEPTH = 1
DN_ALPHA = (2 * DEPTH) ** 0.25
NEG_INF = -1e30
LOG2E = 1.4426950408889634

LANES = 128
N_PAIRS = 8
PAIR_QK = 2 * LANES
PAIR_V = 2 * MLA_V_DIM
QK_WIDTH = N_PAIRS * PAIR_QK
V_WIDTH = N_PAIRS * PAIR_V
VX_WIDTH = N_PAIRS * PAIR_QK
EXTRA_GROUP = 32
FF_CHUNK = 256
N_FF_CHUNKS = D_FF // FF_CHUNK
HALO_ROWS = 8
SUB_ROWS = 256
VMEM_LIMIT = 56 * 1024 * 1024

C_QLAT = 0
C_KVLAT = C_QLAT + MLA_Q_RANK
C_KROPE = C_KVLAT + MLA_KV_RANK
C_FLOGIT = C_KROPE + LANES
C_FQ = C_FLOGIT + LANES
C_FK = C_FQ + FOX_WIDTH
C_FV = C_FK + FOX_WIDTH
C_GATE = C_FV + FOX_WIDTH
C_TOTAL = C_GATE + 2 * D_MODEL

F32 = jnp.float32
BF16 = jnp.bfloat16


def _dot(a, b):
    return jnp.dot(a, b, preferred_element_type=F32)


def _dot_nt(a, b):
    return lax.dot_general(a, b, (((1,), (1,)), ((), ())), preferred_element_type=F32)


def _layer_norm(x, g, b):
    mu = jnp.mean(x, axis=-1, keepdims=True)
    xc = x - mu
    var = jnp.mean(xc * xc, axis=-1, keepdims=True)
    return xc * lax.rsqrt(var + LN_EPS) * g + b


def _rms_norm(x, g):
    return x * lax.rsqrt(jnp.mean(x * x, axis=-1, keepdims=True) + RMS_EPS) * g


def _resident(shape):
    zeros = (0,) * len(shape)
    return pl.BlockSpec(shape, lambda *_: zeros, pipeline_mode=pl.Buffered(1))


def _params():
    return pltpu.CompilerParams(dimension_semantics=("arbitrary", "arbitrary"),
                                vmem_limit_bytes=VMEM_LIMIT)


def _rope(t, cos_t, sa_t, sb_t):
    return (t * cos_t + pltpu.roll(t, MLA_ROPE_DIM // 2, 1) * sa_t
            + pltpu.roll(t, LANES - MLA_ROPE_DIM // 2, 1) * sb_t)


def _cumsum_rows(x):
    rows = x.shape[0]
    row = lax.broadcasted_iota(jnp.int32, x.shape, 0)
    step = 1
    while step < rows:
        x = x + jnp.where(row >= step, pltpu.roll(x, step, 0), 0.0)
        step *= 2
    return x


def _store_values(v_ref, first_pair, v):
    low_half = lax.broadcasted_iota(jnp.int32, (1, LANES), 1) < MLA_V_DIM
    one = jnp.ones((v.shape[0], LANES), v.dtype)
    for p in range(N_PAIRS // 2):
        pair = v[:, p * LANES:(p + 1) * LANES]
        col = (first_pair + p) * PAIR_QK
        v_ref[0, :, col:col + LANES] = jnp.where(low_half, pair, one)
        v_ref[0, :, col + LANES:col + PAIR_QK] = jnp.where(low_half, one, pair)


def _proj_kernel(x_ref, lng_ref, lnb_ref, win_ref, bgate_ref, bforget_ref, qng_ref, wq_ref,
                 kvng_ref, wk_ref, wv_ref, cos_ref, sa_ref, sb_ref, sel_ref, selc_ref, cum0_ref,
                 q_ref, k_ref, v_ref, g_ref, cumend_ref,
                 carry_ref):
    @pl.when(pl.program_id(1) == 0)
    def _():
        carry_ref[...] = cum0_ref[...]

    hb = _layer_norm(x_ref[0], lng_ref[...], lnb_ref[...]).astype(BF16)
    cos_t, sa_t, sb_t = cos_ref[...], sa_ref[...], sb_ref[...]
    mla_scale = MLA_QK_DIM ** -0.5 * LOG2E
    fox_scale = FOX_HEAD_DIM ** -0.5 * LOG2E
    n_half = N_PAIRS // 2

    lat = _dot_nt(hb, win_ref[C_QLAT:C_KROPE, :])
    small = _dot_nt(hb, win_ref[C_KROPE:C_FQ, :])
    gate_chunk = 512
    for c in range(2 * D_MODEL // gate_chunk):
        sl = slice(c * gate_chunk, (c + 1) * gate_chunk)
        zg = _dot_nt(hb, win_ref[C_GATE + c * gate_chunk:C_GATE + (c + 1) * gate_chunk, :])
        zg = zg + bgate_ref[:, sl]
        g_ref[0, :, sl] = (1.0 / (1.0 + jnp.exp(-zg))).astype(BF16)

    qn = _rms_norm(lat[:, :MLA_Q_RANK], qng_ref[...]).astype(BF16)
    kvn = _rms_norm(lat[:, MLA_Q_RANK:], kvng_ref[...]).astype(BF16)
    for p in range(n_half):
        qp = _dot(qn, wq_ref[:, p * PAIR_QK:(p + 1) * PAIR_QK]) * mla_scale
        q_ref[0, :, p * PAIR_QK:p * PAIR_QK + LANES] = qp[:, :LANES].astype(BF16)
        q_ref[0, :, p * PAIR_QK + LANES:(p + 1) * PAIR_QK] = _rope(
            qp[:, LANES:], cos_t, sa_t, sb_t).astype(BF16)
    k_pe = _rope(small[:, :LANES], cos_t, sa_t, sb_t).astype(BF16)
    k_nope = _dot(kvn, wk_ref[...]).astype(BF16)
    for p in range(n_half):
        k_ref[0, :, p * PAIR_QK:p * PAIR_QK + LANES] = k_nope[:, p * LANES:(p + 1) * LANES]
        k_ref[0, :, p * PAIR_QK + LANES:(p + 1) * PAIR_QK] = k_pe
    _store_values(v_ref, 0, _dot(kvn, wv_ref[...]).astype(BF16))

    z = small[:, LANES:] + bforget_ref[...]
    log_f = -(jnp.maximum(-z, 0.0) + jnp.log1p(jnp.exp(-jnp.abs(z))))
    cum = _cumsum_rows(log_f) + carry_ref[0:1, :]
    rows = cum.shape[0]
    carry_ref[...] = jnp.broadcast_to(cum[rows - 1:rows, :], carry_ref.shape)
    cumend_ref[0] = jnp.broadcast_to(cum[rows - 1:rows, :], carry_ref.shape)
    head_lane = lax.broadcasted_iota(jnp.int32, (1, LANES), 1) < FOX_HEADS
    cum = jnp.where(head_lane, cum * LOG2E, 0.0)
    c_hi = cum.astype(BF16).astype(F32)
    r1 = cum - c_hi
    c_mid = r1.astype(BF16).astype(F32)
    c_lo = (r1 - c_mid).astype(BF16).astype(F32)
    pieces = (c_hi + pltpu.roll(c_mid, FOX_HEADS, 1) + pltpu.roll(c_lo, 2 * FOX_HEADS, 1))
    aug = (_dot(pieces.astype(BF16), sel_ref[...]) + selc_ref[...]).astype(BF16)
    base = n_half * PAIR_QK
    for p in range(n_half):
        lo_col = base + p * PAIR_QK + LANES
        q_ref[0, :, lo_col:lo_col + LANES] = aug[:, p * LANES:(p + 1) * LANES]
        k_ref[0, :, lo_col:lo_col + LANES] = aug[:, (n_half + p) * LANES:(n_half + p + 1) * LANES]

    fq = (_dot_nt(hb, win_ref[C_FQ:C_FQ + FOX_WIDTH, :]) * fox_scale).astype(BF16)
    fk = _dot_nt(hb, win_ref[C_FK:C_FK + FOX_WIDTH, :]).astype(BF16)
    for p in range(n_half):
        lo_col = base + p * PAIR_QK
        q_ref[0, :, lo_col:lo_col + LANES] = fq[:, p * LANES:(p + 1) * LANES]
        k_ref[0, :, lo_col:lo_col + LANES] = fk[:, p * LANES:(p + 1) * LANES]
    _store_values(v_ref, n_half, _dot_nt(hb, win_ref[C_FV:C_FV + FOX_WIDTH, :]).astype(BF16))


def _proj_call(x3, tabs, cum0, w, tm):
    nb, seq, _ = x3.shape
    row_blk = lambda width: pl.BlockSpec((1, tm, width), lambda b, j: (b, j, 0))
    tab_blk = pl.BlockSpec((tm, LANES), lambda b, j: (j, 0))
    consts = [w["ln_emb_g"], w["ln_emb_b"], w["w_in"], w["b_gate"], w["b_forget"], w["q_norm_g"],
              w["w_q"], w["kv_norm_g"], w["w_k"], w["w_v"]]
    sels = [w["sel"], w["sel_const"], cum0]
    in_specs = ([row_blk(D_MODEL)] + [_resident(a.shape) for a in consts] + [tab_blk] * 3
                + [_resident(a.shape) for a in sels])
    out_shape = (jax.ShapeDtypeStruct((nb, seq, QK_WIDTH), BF16),
                 jax.ShapeDtypeStruct((nb, seq, QK_WIDTH), BF16),
                 jax.ShapeDtypeStruct((nb, seq, VX_WIDTH), BF16),
                 jax.ShapeDtypeStruct((nb, seq, 2 * D_MODEL), BF16),
                 jax.ShapeDtypeStruct((nb, HALO_ROWS, LANES), F32))
    out_specs = (row_blk(QK_WIDTH), row_blk(QK_WIDTH), row_blk(VX_WIDTH), row_blk(2 * D_MODEL),
                 pl.BlockSpec((1, HALO_ROWS, LANES), lambda b, j: (b, 0, 0)))
    return pl.pallas_call(
        _proj_kernel, grid=(nb, seq // tm), in_specs=in_specs, out_specs=out_specs,
        out_shape=out_shape, scratch_shapes=[pltpu.VMEM((HALO_ROWS, LANES), F32)],
        compiler_params=_params(), name="proj",
    )(x3, *consts, *tabs, *sels)


def _attn_kernel(*refs, tk, rb, n_prefix, pairs):
    for pp in range(pairs):
        qk = slice(pp * PAIR_QK, (pp + 1) * PAIR_QK)
        views = [r.at[:, :, qk] for r in refs[:3]]
        if n_prefix:
            views += [r.at[:, qk] for r in refs[3:5]]
        o_ref, m_ref, acc_ref = refs[-3:]
        views += [o_ref.at[:, :, pp * PAIR_V:(pp + 1) * PAIR_V], m_ref.at[2 * pp:2 * pp + 2],
                  acc_ref.at[2 * pp:2 * pp + 2]]
        _attn_pair(*views, tk=tk, rb=rb, n_prefix=n_prefix)


def _attn_pair(*refs, tk, rb, n_prefix):
    if n_prefix:
        q_ref, k_ref, v_ref, kp_ref, vp_ref, o_ref, m_ref, acc_ref = refs
    else:
        q_ref, k_ref, v_ref, o_ref, m_ref, acc_ref = refs
    q = q_ref[0]
    tq = q.shape[0]
    lane = lax.broadcasted_iota(jnp.int32, (1, PAIR_QK), 1)
    head_a = (lane < MLA_NOPE_DIM) | ((lane >= LANES) & (lane < LANES + EXTRA_GROUP))
    head_b = (((lane >= MLA_NOPE_DIM) & (lane < LANES))
              | ((lane >= LANES + EXTRA_GROUP) & (lane < LANES + 2 * EXTRA_GROUP)))
    zero = jnp.zeros_like(q)
    qs = (jnp.where(head_a, q, zero), jnp.where(head_b, q, zero))
    nt_dims = (((1,), (1,)), ((), ()))
    key_lane = lax.broadcasted_iota(jnp.int32, (1, LANES), 1)

    if n_prefix:
        grp_a = key_lane < n_prefix
        grp_b = (key_lane >= n_prefix) & (key_lane < 2 * n_prefix)
        s_all = lax.dot_general(q, kp_ref[...], nt_dims, preferred_element_type=F32)
        for r0 in range(0, tq, rb):
            rows = slice(r0, r0 + rb)
            s = s_all[rows, :]
            m_a = jnp.max(jnp.where(grp_a, s, NEG_INF), axis=1, keepdims=True)
            m_b = jnp.max(jnp.where(grp_b, s, NEG_INF), axis=1, keepdims=True)
            x = jnp.where(grp_a, s - m_a, jnp.where(grp_b, s - m_b, NEG_INF))
            p = jnp.exp2(x.astype(BF16))
            pv = _dot(p, vp_ref[...])
            for h, m_h in ((0, m_a), (1, m_b)):
                m_ref[h, rows, :] = jnp.broadcast_to(m_h, (rb, LANES))
                acc_ref[h, rows, :] = pv[:, h * LANES:(h + 1) * LANES]
    else:
        m_ref[...] = jnp.full(m_ref.shape, NEG_INF, F32)
        acc_ref[...] = jnp.zeros(acc_ref.shape, F32)

    def scores(h, key0):
        row0 = (key0 // rb) * rb
        return lax.dot_general(qs[h][row0:, :], k_ref[0, key0:key0 + tk, :], nt_dims,
                               preferred_element_type=F32)

    def fold(h, r0, key0, s):
        width = s.shape[1]
        step = min(width, LANES)
        tiles = width // step
        rows = slice(r0, r0 + rb)
        m_cur = s[:, :step]
        for t in range(1, tiles):
            m_cur = jnp.maximum(m_cur, s[:, t * step:(t + 1) * step])
        m_prev = m_ref[h, rows, :]
        m_new = jnp.maximum(m_prev, jnp.max(m_cur, axis=1, keepdims=True))
        alpha = jnp.exp2(m_prev - m_new)
        m_wide = m_new[:, :step] if tiles == 1 else jnp.tile(m_new, (1, tiles))
        p = jnp.exp2((s - m_wide).astype(BF16))
        pv = _dot(p, v_ref[0, key0:key0 + width, h * LANES:(h + 1) * LANES])
        acc_ref[h, rows, :] = alpha * acc_ref[h, rows, :] + pv
        m_ref[h, rows, :] = m_new

    def update(keys, s_tiles):
        first, last = keys[0], keys[-1]
        for r0 in range((first // rb) * rb, tq, rb):
            if r0 + rb <= first:
                continue
            seen = [i for i, key0 in enumerate(keys) if r0 + rb > key0]
            width = len(seen) * tk
            for h in range(2):
                parts = []
                for i in seen:
                    row0 = (keys[i] // rb) * rb
                    parts.append(s_tiles[i][h][r0 - row0:r0 - row0 + rb, :])
                s = parts[0] if len(parts) == 1 else jnp.concatenate(parts, axis=1)
                if r0 < first + width - 1:
                    row = lax.broadcasted_iota(jnp.int32, (rb, width), 0) + r0
                    col = lax.broadcasted_iota(jnp.int32, (rb, width), 1) + first
                    s = jnp.where(col <= row, s, NEG_INF)
                fold(h, r0, first, s)

    key_tiles = list(range(0, tq, tk))
    per_group = 2 if len(key_tiles) % 2 == 0 else 1
    groups = [key_tiles[i:i + per_group] for i in range(0, len(key_tiles), per_group)]
    group_scores = lambda keys: [[scores(h, key0) for h in range(2)] for key0 in keys]
    ahead = 2
    pending = [group_scores(keys) for keys in groups[:ahead]]
    for c, keys in enumerate(groups):
        s_cur = pending.pop(0)
        if c + ahead < len(groups):
            pending.append(group_scores(groups[c + ahead]))
        update(keys, s_cur)

    acc_a, acc_b = acc_ref[0], acc_ref[1]
    o = jnp.where(key_lane < MLA_V_DIM, acc_a / pltpu.roll(acc_a, MLA_V_DIM, 1),
                  acc_b / pltpu.roll(acc_b, MLA_V_DIM, 1))
    o_ref[0] = o.astype(BF16)


def _pair_prefix(k_meta, v_meta):
    n = k_meta.shape[0]
    lane = jnp.arange(QK_WIDTH) % PAIR_QK
    k_a = (lane < MLA_NOPE_DIM) | ((lane >= LANES) & (lane < LANES + EXTRA_GROUP))
    k_b = (((lane >= MLA_NOPE_DIM) & (lane < LANES))
           | ((lane >= LANES + EXTRA_GROUP) & (lane < LANES + 2 * EXTRA_GROUP)))
    v_a = lane < LANES
    zero_k, zero_v = jnp.zeros_like(k_meta), jnp.zeros_like(v_meta)
    pad = ((0, LANES - 2 * n), (0, 0))
    kp = jnp.concatenate([jnp.where(k_a, k_meta, zero_k), jnp.where(k_b, k_meta, zero_k)])
    vp = jnp.concatenate([jnp.where(v_a, v_meta, zero_v), jnp.where(v_a, zero_v, v_meta)])
    return jnp.pad(kp, pad), jnp.pad(vp, pad)


def _attn_call(q, k, v, prefix, tk, rb, pairs):
    nb, seq, _ = q.shape
    n_prefix = 0 if prefix is None else N_META
    wide = pl.BlockSpec((1, seq, pairs * PAIR_QK), lambda b, p: (b, 0, p))
    in_specs = [wide, wide, wide]
    args = [q, k, v]
    if n_prefix:
        in_specs += [pl.BlockSpec((LANES, pairs * PAIR_QK), lambda b, p: (0, p))] * 2
        args += list(_pair_prefix(*prefix))
    return pl.pallas_call(
        functools.partial(_attn_kernel, tk=tk, rb=rb, n_prefix=n_prefix, pairs=pairs),
        grid=(nb, N_PAIRS // pairs), in_specs=in_specs,
        out_specs=pl.BlockSpec((1, seq, pairs * PAIR_V), lambda b, p: (b, 0, p)),
        out_shape=jax.ShapeDtypeStruct((nb, seq, V_WIDTH), BF16),
        scratch_shapes=[pltpu.VMEM((2 * pairs, seq, LANES), F32),
                        pltpu.VMEM((2 * pairs, seq, PAIR_V), F32)],
        compiler_params=_params(),
        name="attn",
    )(*args)


def _merge_kernel(x_ref, o_ref, g_ref, lng_ref, lnb_ref, wbm_ref, wbf_ref, wout_ref, mixg_ref,
                  mixb_ref, h_ref):
    half = V_WIDTH // 2
    n_rows = x_ref.shape[1]
    sub = min(n_rows, SUB_ROWS)
    parts = [slice(r, r + sub) for r in range(0, n_rows, sub)]

    def branches(rows):
        return (_dot(o_ref[0, rows, :half], wbm_ref[...]),
                _dot(o_ref[0, rows, half:], wbf_ref[...]))

    def mix(rows, o_mla, o_fox):
        merged = (g_ref[0, rows, :D_MODEL].astype(F32) * o_mla
                  + g_ref[0, rows, D_MODEL:].astype(F32) * o_fox)
        return _dot(merged.astype(BF16), wout_ref[...])

    def finish(rows, mixed):
        h0 = _layer_norm(x_ref[0, rows, :], lng_ref[...], lnb_ref[...])
        h_ref[0, rows, :] = _layer_norm(DN_ALPHA * h0 + mixed, mixg_ref[...], mixb_ref[...])

    projected = [branches(rows) for rows in parts]
    mixed = [mix(rows, *pr) for rows, pr in zip(parts, projected)]
    for rows, mx in zip(parts, mixed):
        finish(rows, mx)


def _merge_call(x3, o, g, w, tm):
    nb, seq, _ = x3.shape
    row_blk = lambda width: pl.BlockSpec((1, tm, width), lambda b, j: (b, j, 0))
    consts = [w["ln_emb_g"], w["ln_emb_b"], w["w_bm"], w["w_bf"], w["w_out"], w["ln_mix_g"],
              w["ln_mix_b"]]
    return pl.pallas_call(
        _merge_kernel, grid=(nb, seq // tm),
        in_specs=[row_blk(D_MODEL), row_blk(V_WIDTH), row_blk(2 * D_MODEL)]
        + [_resident(a.shape) for a in consts],
        out_specs=row_blk(D_MODEL), out_shape=jax.ShapeDtypeStruct((nb, seq, D_MODEL), F32),
        compiler_params=_params(), name="merge",
    )(x3, o, g, *consts)


def _ffn_kernel(h_ref, halo0_ref, wup_ref, cw_ref, cb_ref, wdown_ref, lng_ref, lnb_ref,
                out_ref, halo_ref, gate_ref, act_ref):
    @pl.when(pl.program_id(1) == 0)
    def _():
        halo_ref[...] = halo0_ref[...]

    n_rows = h_ref.shape[1]
    sub = min(n_rows, SUB_ROWS)

    def gated(r0):
        rows = slice(r0, r0 + sub)
        hb = h_ref[0, rows, :].astype(BF16)
        for c in range(N_FF_CHUNKS):
            sl = slice(c * FF_CHUNK, (c + 1) * FF_CHUNK)
            gate = _dot(hb, wup_ref[:, sl])
            val = _dot(hb, wup_ref[:, D_FF + c * FF_CHUNK:D_FF + (c + 1) * FF_CHUNK])
            gate_ref[c, :HALO_ROWS, :] = halo_ref[:, sl]
            gate_ref[c, HALO_ROWS:, :] = gate
            halo_ref[:, sl] = gate[sub - HALO_ROWS:, :]
            back1 = gate_ref[c, HALO_ROWS - 1:HALO_ROWS - 1 + sub, :]
            back2 = gate_ref[c, HALO_ROWS - 2:HALO_ROWS - 2 + sub, :]
            conv = (cw_ref[0:1, sl] * back2 + cw_ref[1:2, sl] * back1 + cw_ref[2:3, sl] * gate
                    + cb_ref[:, sl])
            act_ref[rows, sl] = (conv * (1.0 / (1.0 + jnp.exp(-conv))) * val).astype(BF16)

    starts = list(range(0, n_rows, sub))
    for r0 in starts:
        gated(r0)
    down = [_dot(act_ref[r0:r0 + sub, :], wdown_ref[...]) for r0 in starts]
    for r0, ffn in zip(starts, down):
        rows = slice(r0, r0 + sub)
        out_ref[0, rows, :] = _layer_norm(DN_ALPHA * h_ref[0, rows, :] + ffn, lng_ref[...],
                                          lnb_ref[...])


def _ffn_call(h1, halo0, w, tm):
    nb, seq, _ = h1.shape
    row_blk = pl.BlockSpec((1, tm, D_MODEL), lambda b, j: (b, j, 0))
    consts = [halo0, w["w_up"], w["conv_w"], w["conv_b"], w["w_down"], w["ln_ffn_g"],
              w["ln_ffn_b"]]
    return pl.pallas_call(
        _ffn_kernel, grid=(nb, seq // tm),
        in_specs=[row_blk] + [_resident(a.shape) for a in consts],
        out_specs=row_blk, out_shape=jax.ShapeDtypeStruct((nb, seq, D_MODEL), F32),
        scratch_shapes=[pltpu.VMEM((HALO_ROWS, D_FF), F32),
                        pltpu.VMEM((N_FF_CHUNKS, HALO_ROWS + min(tm, SUB_ROWS), FF_CHUNK), F32),
                        pltpu.VMEM((tm, D_FF), BF16)],
        compiler_params=_params(), name="ffn",
    )(h1, *consts)


def _gate_tail_kernel(h_ref, wup_ref, tail_ref):
    gate = _dot(h_ref[0].astype(BF16), wup_ref[...])
    tail_ref[...] = gate[gate.shape[0] - HALO_ROWS:, :]


def _gate_tail_call(h1, w_up):
    return pl.pallas_call(
        _gate_tail_kernel, grid=(1,),
        in_specs=[pl.BlockSpec(h1.shape, lambda i: (0, 0, 0)),
                  pl.BlockSpec((D_MODEL, D_FF), lambda i: (0, 0))],
        out_specs=pl.BlockSpec((HALO_ROWS, D_FF), lambda i: (0, 0)),
        out_shape=jax.ShapeDtypeStruct((HALO_ROWS, D_FF), F32),
        compiler_params=pltpu.CompilerParams(dimension_semantics=("arbitrary",),
                                             vmem_limit_bytes=VMEM_LIMIT),
        name="gate_tail",
    )(h1, w_up)


def _rope_tables(length):
    half = MLA_ROPE_DIM // 2
    inv_freq = ROPE_THETA ** (-jnp.arange(half, dtype=F32) / half)
    ang = jnp.arange(length).astype(F32)[:, None] * inv_freq[None, :]
    cos, sin = jnp.cos(ang), jnp.sin(ang)
    zero = jnp.zeros_like(cos)
    widen = lambda a, b: jnp.pad(jnp.tile(jnp.concatenate([a, b], axis=1), (1, 2)),
                                 ((0, 0), (0, LANES - 2 * MLA_ROPE_DIM)))
    return widen(cos, cos), widen(zero, sin), widen(-sin, zero)


def _bias_selectors():
    half = (N_PAIRS // 2) * LANES
    sel = np.zeros((LANES, 2 * half), np.float32)
    const = np.zeros((1, 2 * half), np.float32)
    for h in range(FOX_HEADS):
        base = LANES * (h // 2) + EXTRA_GROUP * (h % 2)
        for piece in range(3):
            sel[piece * FOX_HEADS + h, base + piece] = 1.0
            const[0, base + 3 + piece] = 1.0
            const[0, half + base + piece] = 1.0
            sel[piece * FOX_HEADS + h, half + base + 3 + piece] = -1.0
    return jnp.asarray(sel, BF16), jnp.asarray(const)


def _w_in_kernel(w_ref, o_ref):
    o_krope = MLA_Q_RANK + MLA_KV_RANK
    o_fq = o_krope + MLA_ROPE_DIM
    o_flogit = o_fq + 3 * FOX_WIDTH
    o_gate = o_flogit + FOX_HEADS
    cols = w_ref.shape[1]
    o_ref[C_QLAT:C_KROPE, :] = w_ref[:o_krope, :].astype(BF16)
    k_rope = w_ref[o_krope:o_fq, :]
    o_ref[C_KROPE:C_FLOGIT, :] = jnp.concatenate(
        [k_rope, k_rope, jnp.zeros((LANES - 2 * MLA_ROPE_DIM, cols), F32)], axis=0).astype(BF16)
    o_ref[C_FLOGIT:C_FQ, :] = jnp.concatenate(
        [w_ref[o_flogit:o_gate, :], jnp.zeros((LANES - FOX_HEADS, cols), F32)],
        axis=0).astype(BF16)
    o_ref[C_FQ:C_GATE, :] = w_ref[o_fq:o_flogit, :].astype(BF16)
    o_ref[C_GATE:, :] = w_ref[o_gate:, :].astype(BF16)


def _regroup_w_in(w_in_t):
    cols = 256
    return pl.pallas_call(
        _w_in_kernel, grid=(D_MODEL // cols,),
        in_specs=[pl.BlockSpec((w_in_t.shape[0], cols), lambda i: (0, i))],
        out_specs=pl.BlockSpec((C_TOTAL, cols), lambda i: (0, i)),
        out_shape=jax.ShapeDtypeStruct((C_TOTAL, D_MODEL), BF16),
        compiler_params=pltpu.CompilerParams(dimension_semantics=("arbitrary",),
                                             vmem_limit_bytes=VMEM_LIMIT),
        name="w_in_regroup",
    )(w_in_t)


def _prepare_weights(w_in, b_gate, b_forget, q_norm_g, w_q_up, kv_norm_g, w_kv_up, w_branch_mla,
                     w_branch_fox, w_out, w_ffn_up, conv_w, conv_b, w_ffn_down):
    n_half = N_PAIRS // 2
    w_in_p = _regroup_w_in(w_in.T)
    wq = w_q_up.reshape(MLA_Q_RANK, n_half, 2, MLA_QK_DIM)
    wq_p = jnp.concatenate([
        wq[..., :MLA_NOPE_DIM].reshape(MLA_Q_RANK, n_half, LANES),
        wq[..., MLA_NOPE_DIM:].reshape(MLA_Q_RANK, n_half, 2 * MLA_ROPE_DIM),
        jnp.zeros((MLA_Q_RANK, n_half, LANES - 2 * MLA_ROPE_DIM), F32)], axis=-1)
    wkv = w_kv_up.reshape(MLA_KV_RANK, MLA_HEADS, MLA_NOPE_DIM + MLA_V_DIM)
    return {
        "w_in": w_in_p,
        "b_gate": b_gate.reshape(1, -1),
        "b_forget": jnp.pad(b_forget.reshape(1, -1), ((0, 0), (0, LANES - FOX_HEADS))),
        "q_norm_g": q_norm_g.reshape(1, -1),
        "w_q": wq_p.reshape(MLA_Q_RANK, n_half * PAIR_QK).astype(BF16),
        "kv_norm_g": kv_norm_g.reshape(1, -1),
        "w_k": wkv[..., :MLA_NOPE_DIM].reshape(MLA_KV_RANK, -1).astype(BF16),
        "w_v": wkv[..., MLA_NOPE_DIM:].reshape(MLA_KV_RANK, -1).astype(BF16),
        "w_bm": w_branch_mla.astype(BF16),
        "w_bf": w_branch_fox.astype(BF16),
        "w_out": w_out.astype(BF16),
        "w_up": w_ffn_up.astype(BF16),
        "conv_w": conv_w,
        "conv_b": conv_b.reshape(1, -1),
        "w_down": w_ffn_down.astype(BF16),
    }


def _layer(x3, tabs, cum0, halo0, prefix, w, tm, tk, rb, pairs):
    q, k, v, gates, cum_end = _proj_call(x3, tabs, cum0, w, tm)
    o = _attn_call(q, k, v, prefix, tk, rb, pairs)
    h1 = _merge_call(x3, o, gates, w, min(2 * tm, x3.shape[1]))
    if halo0 is None:
        return None, (k[0], v[0]), cum_end[0], _gate_tail_call(h1, w["w_up"])
    out = _ffn_call(h1, halo0, w, min(2 * tm, x3.shape[1]))
    return out, None, None, None


def kernel(x, meta_tokens, ln_emb_g, ln_emb_b, w_in, b_gate, b_forget, q_norm_g, w_q_up, kv_norm_g, w_kv_up, w_branch_mla, w_branch_fox, w_out, ln_mix_g, ln_mix_b, w_ffn_up, conv_w, conv_b, w_ffn_down, ln_ffn_g, ln_ffn_b):
    seq = x.shape[1]
    w = _prepare_weights(w_in[0], b_gate[0], b_forget[0], q_norm_g[0], w_q_up[0], kv_norm_g[0],
                         w_kv_up[0], w_branch_mla[0], w_branch_fox[0], w_out[0], w_ffn_up[0],
                         conv_w[0], conv_b[0], w_ffn_down[0])
    w.update(ln_emb_g=ln_emb_g.reshape(1, -1), ln_emb_b=ln_emb_b.reshape(1, -1),
             ln_mix_g=ln_mix_g[0].reshape(1, -1), ln_mix_b=ln_mix_b[0].reshape(1, -1),
             ln_ffn_g=ln_ffn_g[0].reshape(1, -1), ln_ffn_b=ln_ffn_b[0].reshape(1, -1))
    w["sel"], w["sel_const"] = _bias_selectors()
    tabs = _rope_tables(N_META + seq)
    meta_tabs = tuple(t[:N_META] for t in tabs)
    real_tabs = tuple(t[N_META:] for t in tabs)

    _, meta_kv, meta_cum, meta_tail = _layer(
        meta_tokens[None].astype(x.dtype), meta_tabs, jnp.zeros((HALO_ROWS, LANES), F32),
        None, None, w, N_META, N_META, N_META, N_PAIRS)
    out, _, _, _ = _layer(x, real_tabs, meta_cum, meta_tail, meta_kv, w, 512, 256, 128, 1)
    return out
```

```python
import functools

import numpy as np
import jax
import jax.numpy as jnp
from jax import lax
from jax.experimental import pallas as pl
from jax.experimental.pallas import tpu as pltpu

D_MODEL = 1024
N_META = 16
MLA_HEADS = 8
MLA_Q_RANK = 384
MLA_KV_RANK = 128
MLA_NOPE_DIM = 64
MLA_ROPE_DIM = 32
MLA_QK_DIM = MLA_NOPE_DIM + MLA_ROPE_DIM
MLA_V_DIM = 64
ROPE_THETA = 10000.0
FOX_HEADS = 8
FOX_HEAD_DIM = 64
FOX_WIDTH = FOX_HEADS * FOX_HEAD_DIM
D_FF = 2816
LN_EPS = 1e-5
RMS_EPS = 1e-6
DEPTH = 1
DN_ALPHA = (2 * DEPTH) ** 0.25
NEG_INF = -1e30
LOG2E = 1.4426950408889634

LANES = 128
N_PAIRS = 8
PAIR_QK = 2 * LANES
PAIR_V = 2 * MLA_V_DIM
QK_WIDTH = N_PAIRS * PAIR_QK
V_WIDTH = N_PAIRS * PAIR_V
VX_WIDTH = N_PAIRS * PAIR_QK
EXTRA_GROUP = 32
FF_CHUNK = 256
N_FF_CHUNKS = D_FF // FF_CHUNK
HALO_ROWS = 8
SUB_ROWS = 256
VMEM_LIMIT = 56 * 1024 * 1024

C_QLAT = 0
C_KVLAT = C_QLAT + MLA_Q_RANK
C_KROPE = C_KVLAT + MLA_KV_RANK
C_FLOGIT = C_KROPE + LANES
C_FQ = C_FLOGIT + LANES
C_FK = C_FQ + FOX_WIDTH
C_FV = C_FK + FOX_WIDTH
C_GATE = C_FV + FOX_WIDTH
C_TOTAL = C_GATE + 2 * D_MODEL

F32 = jnp.float32
BF16 = jnp.bfloat16


def _dot(a, b):
    return jnp.dot(a, b, preferred_element_type=F32)


def _dot_nt(a, b):
    return lax.dot_general(a, b, (((1,), (1,)), ((), ())), preferred_element_type=F32)


def _layer_norm(x, g, b):
    mu = jnp.mean(x, axis=-1, keepdims=True)
    xc = x - mu
    var = jnp.mean(xc * xc, axis=-1, keepdims=True)
    return xc * lax.rsqrt(var + LN_EPS) * g + b


def _rms_norm(x, g):
    return x * lax.rsqrt(jnp.mean(x * x, axis=-1, keepdims=True) + RMS_EPS) * g


def _resident(shape):
    zeros = (0,) * len(shape)
    return pl.BlockSpec(shape, lambda *_: zeros, pipeline_mode=pl.Buffered(1))


def _params():
    return pltpu.CompilerParams(dimension_semantics=("arbitrary", "arbitrary"),
                                vmem_limit_bytes=VMEM_LIMIT)


def _rope(t, cos_t, sa_t, sb_t):
    return (t * cos_t + pltpu.roll(t, MLA_ROPE_DIM // 2, 1) * sa_t
            + pltpu.roll(t, LANES - MLA_ROPE_DIM // 2, 1) * sb_t)


def _cumsum_rows(x):
    rows = x.shape[0]
    row = lax.broadcasted_iota(jnp.int32, x.shape, 0)
    step = 1
    while step < rows:
        x = x + jnp.where(row >= step, pltpu.roll(x, step, 0), 0.0)
        step *= 2
    return x


def _store_values(v_ref, rows, first_pair, v):
    low_half = lax.broadcasted_iota(jnp.int32, (1, LANES), 1) < MLA_V_DIM
    one = jnp.ones((v.shape[0], LANES), v.dtype)
    for p in range(N_PAIRS // 2):
        pair = v[:, p * LANES:(p + 1) * LANES]
        col = (first_pair + p) * PAIR_QK
        v_ref[0, rows, col:col + LANES] = jnp.where(low_half, pair, one)
        v_ref[0, rows, col + LANES:col + PAIR_QK] = jnp.where(low_half, one, pair)


def _proj_kernel(x_ref, lng_ref, lnb_ref, win_ref, bgate_ref, bforget_ref, qng_ref, wq_ref,
                 kvng_ref, wk_ref, wv_ref, cos_ref, sa_ref, sb_ref, sel_ref, selc_ref, cum0_ref,
                 q_ref, k_ref, v_ref, g_ref, cumend_ref,
                 carry_ref):
    @pl.when(pl.program_id(1) == 0)
    def _():
        carry_ref[...] = cum0_ref[...]

    tile = slice(None)
    hb = _layer_norm(x_ref[0], lng_ref[...], lnb_ref[...]).astype(BF16)
    cos_t, sa_t, sb_t = cos_ref[...], sa_ref[...], sb_ref[...]
    mla_scale = MLA_QK_DIM ** -0.5 * LOG2E
    fox_scale = FOX_HEAD_DIM ** -0.5 * LOG2E
    n_half = N_PAIRS // 2

    lat = _dot_nt(hb, win_ref[C_QLAT:C_KROPE, :])
    small = _dot_nt(hb, win_ref[C_KROPE:C_FQ, :])
    gate_chunk = 512
    for c in range(2 * D_MODEL // gate_chunk):
        sl = slice(c * gate_chunk, (c + 1) * gate_chunk)
        zg = _dot_nt(hb, win_ref[C_GATE + c * gate_chunk:C_GATE + (c + 1) * gate_chunk, :])
        zg = zg + bgate_ref[:, sl]
        g_ref[0, tile, sl] = (1.0 / (1.0 + jnp.exp(-zg))).astype(BF16)

    qn = _rms_norm(lat[:, :MLA_Q_RANK], qng_ref[...]).astype(BF16)
    kvn = _rms_norm(lat[:, MLA_Q_RANK:], kvng_ref[...]).astype(BF16)
    for p in range(n_half):
        qp = _dot(qn, wq_ref[:, p * PAIR_QK:(p + 1) * PAIR_QK]) * mla_scale
        q_ref[0, tile, p * PAIR_QK:p * PAIR_QK + LANES] = qp[:, :LANES].astype(BF16)
        q_ref[0, tile, p * PAIR_QK + LANES:(p + 1) * PAIR_QK] = _rope(
            qp[:, LANES:], cos_t, sa_t, sb_t).astype(BF16)
    k_pe = _rope(small[:, :LANES], cos_t, sa_t, sb_t).astype(BF16)
    k_nope = _dot(kvn, wk_ref[...]).astype(BF16)
    for p in range(n_half):
        k_ref[0, tile, p * PAIR_QK:p * PAIR_QK + LANES] = k_nope[:, p * LANES:(p + 1) * LANES]
        k_ref[0, tile, p * PAIR_QK + LANES:(p + 1) * PAIR_QK] = k_pe
    _store_values(v_ref, tile, 0, _dot(kvn, wv_ref[...]).astype(BF16))

    z = small[:, LANES:] + bforget_ref[...]
    log_f = -(jnp.maximum(-z, 0.0) + jnp.log1p(jnp.exp(-jnp.abs(z))))
    cum = _cumsum_rows(log_f) + carry_ref[0:1, :]
    rows = cum.shape[0]
    carry_ref[...] = jnp.broadcast_to(cum[rows - 1:rows, :], carry_ref.shape)
    cumend_ref[0] = jnp.broadcast_to(cum[rows - 1:rows, :], carry_ref.shape)
    head_lane = lax.broadcasted_iota(jnp.int32, (1, LANES), 1) < FOX_HEADS
    cum = jnp.where(head_lane, cum * LOG2E, 0.0)
    c_hi = cum.astype(BF16).astype(F32)
    r1 = cum - c_hi
    c_mid = r1.astype(BF16).astype(F32)
    c_lo = (r1 - c_mid).astype(BF16).astype(F32)
    pieces = (c_hi + pltpu.roll(c_mid, FOX_HEADS, 1) + pltpu.roll(c_lo, 2 * FOX_HEADS, 1))
    aug = (_dot(pieces.astype(BF16), sel_ref[...]) + selc_ref[...]).astype(BF16)
    base = n_half * PAIR_QK
    for p in range(n_half):
        lo_col = base + p * PAIR_QK + LANES
        q_ref[0, tile, lo_col:lo_col + LANES] = aug[:, p * LANES:(p + 1) * LANES]
        k_ref[0, tile, lo_col:lo_col + LANES] = aug[:, (n_half + p) * LANES:(n_half + p + 1) * LANES]

    fq = (_dot_nt(hb, win_ref[C_FQ:C_FQ + FOX_WIDTH, :]) * fox_scale).astype(BF16)
    fk = _dot_nt(hb, win_ref[C_FK:C_FK + FOX_WIDTH, :]).astype(BF16)
    for p in range(n_half):
        lo_col = base + p * PAIR_QK
        q_ref[0, tile, lo_col:lo_col + LANES] = fq[:, p * LANES:(p + 1) * LANES]
        k_ref[0, tile, lo_col:lo_col + LANES] = fk[:, p * LANES:(p + 1) * LANES]
    _store_values(v_ref, tile, n_half, _dot_nt(hb, win_ref[C_FV:C_FV + FOX_WIDTH, :]).astype(BF16))


def _proj_call(x3, tabs, cum0, w, tm):
    nb, seq, _ = x3.shape
    row_blk = lambda width: pl.BlockSpec((1, tm, width), lambda b, j: (b, j, 0))
    tab_blk = pl.BlockSpec((tm, LANES), lambda b, j: (j, 0))
    consts = [w["ln_emb_g"], w["ln_emb_b"], w["w_in"], w["b_gate"], w["b_forget"], w["q_norm_g"],
              w["w_q"], w["kv_norm_g"], w["w_k"], w["w_v"]]
    sels = [w["sel"], w["sel_const"], cum0]
    in_specs = ([row_blk(D_MODEL)] + [_resident(a.shape) for a in consts] + [tab_blk] * 3
                + [_resident(a.shape) for a in sels])
    out_shape = (jax.ShapeDtypeStruct((nb, seq, QK_WIDTH), BF16),
                 jax.ShapeDtypeStruct((nb, seq, QK_WIDTH), BF16),
                 jax.ShapeDtypeStruct((nb, seq, VX_WIDTH), BF16),
                 jax.ShapeDtypeStruct((nb, seq, 2 * D_MODEL), BF16),
                 jax.ShapeDtypeStruct((nb, HALO_ROWS, LANES), F32))
    out_specs = (row_blk(QK_WIDTH), row_blk(QK_WIDTH), row_blk(VX_WIDTH), row_blk(2 * D_MODEL),
                 pl.BlockSpec((1, HALO_ROWS, LANES), lambda b, j: (b, 0, 0)))
    return pl.pallas_call(
        _proj_kernel, grid=(nb, seq // tm), in_specs=in_specs, out_specs=out_specs,
        out_shape=out_shape, scratch_shapes=[pltpu.VMEM((HALO_ROWS, LANES), F32)],
        compiler_params=_params(), name="proj",
    )(x3, *consts, *tabs, *sels)


def _attn_kernel(*refs, tk, rb, n_prefix, pairs):
    for pp in range(pairs):
        qk = slice(pp * PAIR_QK, (pp + 1) * PAIR_QK)
        views = [r.at[:, :, qk] for r in refs[:3]]
        if n_prefix:
            views += [r.at[:, qk] for r in refs[3:5]]
        o_ref, m_ref, acc_ref = refs[-3:]
        views += [o_ref.at[:, :, pp * PAIR_V:(pp + 1) * PAIR_V], m_ref.at[2 * pp:2 * pp + 2],
                  acc_ref.at[2 * pp:2 * pp + 2]]
        _attn_pair(*views, tk=tk, rb=rb, n_prefix=n_prefix)


def _attn_pair(*refs, tk, rb, n_prefix):
    if n_prefix:
        q_ref, k_ref, v_ref, kp_ref, vp_ref, o_ref, m_ref, acc_ref = refs
    else:
        q_ref, k_ref, v_ref, o_ref, m_ref, acc_ref = refs
    q = q_ref[0]
    tq = q.shape[0]
    lane = lax.broadcasted_iota(jnp.int32, (1, PAIR_QK), 1)
    head_a = (lane < MLA_NOPE_DIM) | ((lane >= LANES) & (lane < LANES + EXTRA_GROUP))
    head_b = (((lane >= MLA_NOPE_DIM) & (lane < LANES))
              | ((lane >= LANES + EXTRA_GROUP) & (lane < LANES + 2 * EXTRA_GROUP)))
    zero = jnp.zeros_like(q)
    qs = (jnp.where(head_a, q, zero), jnp.where(head_b, q, zero))
    nt_dims = (((1,), (1,)), ((), ()))
    key_lane = lax.broadcasted_iota(jnp.int32, (1, LANES), 1)

    if n_prefix:
        grp_a = key_lane < n_prefix
        grp_b = (key_lane >= n_prefix) & (key_lane < 2 * n_prefix)
        s_all = lax.dot_general(q, kp_ref[...], nt_dims, preferred_element_type=F32)
        for r0 in range(0, tq, rb):
            rows = slice(r0, r0 + rb)
            s = s_all[rows, :]
            m_a = jnp.max(jnp.where(grp_a, s, NEG_INF), axis=1, keepdims=True)
            m_b = jnp.max(jnp.where(grp_b, s, NEG_INF), axis=1, keepdims=True)
            x = jnp.where(grp_a, s - m_a, jnp.where(grp_b, s - m_b, NEG_INF))
            p = jnp.exp2(x.astype(BF16))
            pv = _dot(p, vp_ref[...])
            for h, m_h in ((0, m_a), (1, m_b)):
                m_ref[h, rows, :] = jnp.broadcast_to(m_h, (rb, LANES))
                acc_ref[h, rows, :] = pv[:, h * LANES:(h + 1) * LANES]
    else:
        m_ref[...] = jnp.full(m_ref.shape, NEG_INF, F32)
        acc_ref[...] = jnp.zeros(acc_ref.shape, F32)

    def scores(h, key0):
        row0 = (key0 // rb) * rb
        return lax.dot_general(qs[h][row0:, :], k_ref[0, key0:key0 + tk, :], nt_dims,
                               preferred_element_type=F32)

    def fold(h, r0, key0, s):
        width = s.shape[1]
        step = min(width, LANES)
        tiles = width // step
        rows = slice(r0, r0 + rb)
        m_cur = s[:, :step]
        for t in range(1, tiles):
            m_cur = jnp.maximum(m_cur, s[:, t * step:(t + 1) * step])
        m_prev = m_ref[h, rows, :]
        m_new = jnp.maximum(m_prev, jnp.max(m_cur, axis=1, keepdims=True))
        alpha = jnp.exp2(m_prev - m_new)
        m_wide = m_new[:, :step] if tiles == 1 else jnp.tile(m_new, (1, tiles))
        p = jnp.exp2((s - m_wide).astype(BF16))
        pv = _dot(p, v_ref[0, key0:key0 + width, h * LANES:(h + 1) * LANES])
        acc_ref[h, rows, :] = alpha * acc_ref[h, rows, :] + pv
        m_ref[h, rows, :] = m_new

    def update(keys, s_tiles):
        first, last = keys[0], keys[-1]
        for r0 in range((first // rb) * rb, tq, rb):
            if r0 + rb <= first:
                continue
            seen = [i for i, key0 in enumerate(keys) if r0 + rb > key0]
            width = len(seen) * tk
            for h in range(2):
                parts = []
                for i in seen:
                    row0 = (keys[i] // rb) * rb
                    parts.append(s_tiles[i][h][r0 - row0:r0 - row0 + rb, :])
                s = parts[0] if len(parts) == 1 else jnp.concatenate(parts, axis=1)
                if r0 < first + width - 1:
                    row = lax.broadcasted_iota(jnp.int32, (rb, width), 0) + r0
                    col = lax.broadcasted_iota(jnp.int32, (rb, width), 1) + first
                    s = jnp.where(col <= row, s, NEG_INF)
                fold(h, r0, first, s)

    key_tiles = list(range(0, tq, tk))
    per_group = 2 if len(key_tiles) % 2 == 0 else 1
    groups = [key_tiles[i:i + per_group] for i in range(0, len(key_tiles), per_group)]
    group_scores = lambda keys: [[scores(h, key0) for h in range(2)] for key0 in keys]
    ahead = 3
    pending = [group_scores(keys) for keys in groups[:ahead]]
    for c, keys in enumerate(groups):
        s_cur = pending.pop(0)
        if c + ahead < len(groups):
            pending.append(group_scores(groups[c + ahead]))
        update(keys, s_cur)

    acc_a, acc_b = acc_ref[0], acc_ref[1]
    o = jnp.where(key_lane < MLA_V_DIM, acc_a / pltpu.roll(acc_a, MLA_V_DIM, 1),
                  acc_b / pltpu.roll(acc_b, MLA_V_DIM, 1))
    o_ref[0] = o.astype(BF16)


def _pair_prefix(k_meta, v_meta):
    n = k_meta.shape[0]
    lane = jnp.arange(QK_WIDTH) % PAIR_QK
    k_a = (lane < MLA_NOPE_DIM) | ((lane >= LANES) & (lane < LANES + EXTRA_GROUP))
    k_b = (((lane >= MLA_NOPE_DIM) & (lane < LANES))
           | ((lane >= LANES + EXTRA_GROUP) & (lane < LANES + 2 * EXTRA_GROUP)))
    v_a = lane < LANES
    zero_k, zero_v = jnp.zeros_like(k_meta), jnp.zeros_like(v_meta)
    pad = ((0, LANES - 2 * n), (0, 0))
    kp = jnp.concatenate([jnp.where(k_a, k_meta, zero_k), jnp.where(k_b, k_meta, zero_k)])
    vp = jnp.concatenate([jnp.where(v_a, v_meta, zero_v), jnp.where(v_a, zero_v, v_meta)])
    return jnp.pad(kp, pad), jnp.pad(vp, pad)


def _attn_call(q, k, v, prefix, tk, rb, pairs):
    nb, seq, _ = q.shape
    n_prefix = 0 if prefix is None else N_META
    wide = pl.BlockSpec((1, seq, pairs * PAIR_QK), lambda b, p: (b, 0, p))
    in_specs = [wide, wide, wide]
    args = [q, k, v]
    if n_prefix:
        in_specs += [pl.BlockSpec((LANES, pairs * PAIR_QK), lambda b, p: (0, p))] * 2
        args += list(_pair_prefix(*prefix))
    return pl.pallas_call(
        functools.partial(_attn_kernel, tk=tk, rb=rb, n_prefix=n_prefix, pairs=pairs),
        grid=(nb, N_PAIRS // pairs), in_specs=in_specs,
        out_specs=pl.BlockSpec((1, seq, pairs * PAIR_V), lambda b, p: (b, 0, p)),
        out_shape=jax.ShapeDtypeStruct((nb, seq, V_WIDTH), BF16),
        scratch_shapes=[pltpu.VMEM((2 * pairs, seq, LANES), F32),
                        pltpu.VMEM((2 * pairs, seq, PAIR_V), F32)],
        compiler_params=_params(),
        name="attn",
    )(*args)


def _merge_kernel(x_ref, o_ref, g_ref, lng_ref, lnb_ref, wbm_ref, wbf_ref, wout_ref, mixg_ref,
                  mixb_ref, h_ref):
    half = V_WIDTH // 2
    n_rows = x_ref.shape[1]
    sub = min(n_rows, SUB_ROWS)
    parts = [slice(r, r + sub) for r in range(0, n_rows, sub)]

    def branches(rows):
        return (_dot(o_ref[0, rows, :half], wbm_ref[...]),
                _dot(o_ref[0, rows, half:], wbf_ref[...]))

    def mix(rows, o_mla, o_fox):
        merged = (g_ref[0, rows, :D_MODEL].astype(F32) * o_mla
                  + g_ref[0, rows, D_MODEL:].astype(F32) * o_fox)
        return _dot(merged.astype(BF16), wout_ref[...])

    def finish(rows, mixed):
        h0 = _layer_norm(x_ref[0, rows, :], lng_ref[...], lnb_ref[...])
        h_ref[0, rows, :] = _layer_norm(DN_ALPHA * h0 + mixed, mixg_ref[...], mixb_ref[...])

    projected = [branches(rows) for rows in parts]
    mixed = [mix(rows, *pr) for rows, pr in zip(parts, projected)]
    for rows, mx in zip(parts, mixed):
        finish(rows, mx)


def _merge_call(x3, o, g, w, tm):
    nb, seq, _ = x3.shape
    row_blk = lambda width: pl.BlockSpec((1, tm, width), lambda b, j: (b, j, 0))
    consts = [w["ln_emb_g"], w["ln_emb_b"], w["w_bm"], w["w_bf"], w["w_out"], w["ln_mix_g"],
              w["ln_mix_b"]]
    return pl.pallas_call(
        _merge_kernel, grid=(nb, seq // tm),
        in_specs=[row_blk(D_MODEL), row_blk(V_WIDTH), row_blk(2 * D_MODEL)]
        + [_resident(a.shape) for a in consts],
        out_specs=row_blk(D_MODEL), out_shape=jax.ShapeDtypeStruct((nb, seq, D_MODEL), F32),
        compiler_params=_params(), name="merge",
    )(x3, o, g, *consts)


def _ffn_kernel(h_ref, halo0_ref, wup_ref, cw_ref, cb_ref, wdown_ref, lng_ref, lnb_ref,
                out_ref, halo_ref, gate_ref, act_ref):
    @pl.when(pl.program_id(1) == 0)
    def _():
        halo_ref[...] = halo0_ref[...]

    n_rows = h_ref.shape[1]
    sub = min(n_rows, SUB_ROWS)

    def gated(r0):
        rows = slice(r0, r0 + sub)
        hb = h_ref[0, rows, :].astype(BF16)
        for c in range(N_FF_CHUNKS):
            sl = slice(c * FF_CHUNK, (c + 1) * FF_CHUNK)
            gate = _dot(hb, wup_ref[:, sl])
            val = _dot(hb, wup_ref[:, D_FF + c * FF_CHUNK:D_FF + (c + 1) * FF_CHUNK])
            gate_ref[c, :HALO_ROWS, :] = halo_ref[:, sl]
            gate_ref[c, HALO_ROWS:, :] = gate
            halo_ref[:, sl] = gate[sub - HALO_ROWS:, :]
            back1 = gate_ref[c, HALO_ROWS - 1:HALO_ROWS - 1 + sub, :]
            back2 = gate_ref[c, HALO_ROWS - 2:HALO_ROWS - 2 + sub, :]
            conv = (cw_ref[0:1, sl] * back2 + cw_ref[1:2, sl] * back1 + cw_ref[2:3, sl] * gate
                    + cb_ref[:, sl])
            act_ref[rows, sl] = (conv * (1.0 / (1.0 + jnp.exp(-conv))) * val).astype(BF16)

    starts = list(range(0, n_rows, sub))
    for r0 in starts:
        gated(r0)
    down = [_dot(act_ref[r0:r0 + sub, :], wdown_ref[...]) for r0 in starts]
    for r0, ffn in zip(starts, down):
        rows = slice(r0, r0 + sub)
        out_ref[0, rows, :] = _layer_norm(DN_ALPHA * h_ref[0, rows, :] + ffn, lng_ref[...],
                                          lnb_ref[...])


def _ffn_call(h1, halo0, w, tm):
    nb, seq, _ = h1.shape
    row_blk = pl.BlockSpec((1, tm, D_MODEL), lambda b, j: (b, j, 0))
    consts = [halo0, w["w_up"], w["conv_w"], w["conv_b"], w["w_down"], w["ln_ffn_g"],
              w["ln_ffn_b"]]
    return pl.pallas_call(
        _ffn_kernel, grid=(nb, seq // tm),
        in_specs=[row_blk] + [_resident(a.shape) for a in consts],
        out_specs=row_blk, out_shape=jax.ShapeDtypeStruct((nb, seq, D_MODEL), F32),
        scratch_shapes=[pltpu.VMEM((HALO_ROWS, D_FF), F32),
                        pltpu.VMEM((N_FF_CHUNKS, HALO_ROWS + min(tm, SUB_ROWS), FF_CHUNK), F32),
                        pltpu.VMEM((tm, D_FF), BF16)],
        compiler_params=_params(), name="ffn",
    )(h1, *consts)


def _gate_tail_kernel(h_ref, wup_ref, tail_ref):
    gate = _dot(h_ref[0].astype(BF16), wup_ref[...])
    tail_ref[...] = gate[gate.shape[0] - HALO_ROWS:, :]


def _gate_tail_call(h1, w_up):
    return pl.pallas_call(
        _gate_tail_kernel, grid=(1,),
        in_specs=[pl.BlockSpec(h1.shape, lambda i: (0, 0, 0)),
                  pl.BlockSpec((D_MODEL, D_FF), lambda i: (0, 0))],
        out_specs=pl.BlockSpec((HALO_ROWS, D_FF), lambda i: (0, 0)),
        out_shape=jax.ShapeDtypeStruct((HALO_ROWS, D_FF), F32),
        compiler_params=pltpu.CompilerParams(dimension_semantics=("arbitrary",),
                                             vmem_limit_bytes=VMEM_LIMIT),
        name="gate_tail",
    )(h1, w_up)


def _rope_tables(length):
    half = MLA_ROPE_DIM // 2
    inv_freq = ROPE_THETA ** (-jnp.arange(half, dtype=F32) / half)
    ang = jnp.arange(length).astype(F32)[:, None] * inv_freq[None, :]
    cos, sin = jnp.cos(ang), jnp.sin(ang)
    zero = jnp.zeros_like(cos)
    widen = lambda a, b: jnp.pad(jnp.tile(jnp.concatenate([a, b], axis=1), (1, 2)),
                                 ((0, 0), (0, LANES - 2 * MLA_ROPE_DIM)))
    return widen(cos, cos), widen(zero, sin), widen(-sin, zero)


def _bias_selectors():
    half = (N_PAIRS // 2) * LANES
    sel = np.zeros((LANES, 2 * half), np.float32)
    const = np.zeros((1, 2 * half), np.float32)
    for h in range(FOX_HEADS):
        base = LANES * (h // 2) + EXTRA_GROUP * (h % 2)
        for piece in range(3):
            sel[piece * FOX_HEADS + h, base + piece] = 1.0
            const[0, base + 3 + piece] = 1.0
            const[0, half + base + piece] = 1.0
            sel[piece * FOX_HEADS + h, half + base + 3 + piece] = -1.0
    return jnp.asarray(sel, BF16), jnp.asarray(const)


def _w_in_kernel(w_ref, o_ref):
    o_krope = MLA_Q_RANK + MLA_KV_RANK
    o_fq = o_krope + MLA_ROPE_DIM
    o_flogit = o_fq + 3 * FOX_WIDTH
    o_gate = o_flogit + FOX_HEADS
    cols = w_ref.shape[1]
    o_ref[C_QLAT:C_KROPE, :] = w_ref[:o_krope, :].astype(BF16)
    k_rope = w_ref[o_krope:o_fq, :]
    o_ref[C_KROPE:C_FLOGIT, :] = jnp.concatenate(
        [k_rope, k_rope, jnp.zeros((LANES - 2 * MLA_ROPE_DIM, cols), F32)], axis=0).astype(BF16)
    o_ref[C_FLOGIT:C_FQ, :] = jnp.concatenate(
        [w_ref[o_flogit:o_gate, :], jnp.zeros((LANES - FOX_HEADS, cols), F32)],
        axis=0).astype(BF16)
    o_ref[C_FQ:C_GATE, :] = w_ref[o_fq:o_flogit, :].astype(BF16)
    o_ref[C_GATE:, :] = w_ref[o_gate:, :].astype(BF16)


def _regroup_w_in(w_in_t):
    cols = 256
    return pl.pallas_call(
        _w_in_kernel, grid=(D_MODEL // cols,),
        in_specs=[pl.BlockSpec((w_in_t.shape[0], cols), lambda i: (0, i))],
        out_specs=pl.BlockSpec((C_TOTAL, cols), lambda i: (0, i)),
        out_shape=jax.ShapeDtypeStruct((C_TOTAL, D_MODEL), BF16),
        compiler_params=pltpu.CompilerParams(dimension_semantics=("arbitrary",),
                                             vmem_limit_bytes=VMEM_LIMIT),
        name="w_in_regroup",
    )(w_in_t)


def _prepare_weights(w_in, b_gate, b_forget, q_norm_g, w_q_up, kv_norm_g, w_kv_up, w_branch_mla,
                     w_branch_fox, w_out, w_ffn_up, conv_w, conv_b, w_ffn_down):
    n_half = N_PAIRS // 2
    w_in_p = _regroup_w_in(w_in.T)
    wq = w_q_up.reshape(MLA_Q_RANK, n_half, 2, MLA_QK_DIM)
    wq_p = jnp.concatenate([
        wq[..., :MLA_NOPE_DIM].reshape(MLA_Q_RANK, n_half, LANES),
        wq[..., MLA_NOPE_DIM:].reshape(MLA_Q_RANK, n_half, 2 * MLA_ROPE_DIM),
        jnp.zeros((MLA_Q_RANK, n_half, LANES - 2 * MLA_ROPE_DIM), F32)], axis=-1)
    wkv = w_kv_up.reshape(MLA_KV_RANK, MLA_HEADS, MLA_NOPE_DIM + MLA_V_DIM)
    return {
        "w_in": w_in_p,
        "b_gate": b_gate.reshape(1, -1),
        "b_forget": jnp.pad(b_forget.reshape(1, -1), ((0, 0), (0, LANES - FOX_HEADS))),
        "q_norm_g": q_norm_g.reshape(1, -1),
        "w_q": wq_p.reshape(MLA_Q_RANK, n_half * PAIR_QK).astype(BF16),
        "kv_norm_g": kv_norm_g.reshape(1, -1),
        "w_k": wkv[..., :MLA_NOPE_DIM].reshape(MLA_KV_RANK, -1).astype(BF16),
        "w_v": wkv[..., MLA_NOPE_DIM:].reshape(MLA_KV_RANK, -1).astype(BF16),
        "w_bm": w_branch_mla.astype(BF16),
        "w_bf": w_branch_fox.astype(BF16),
        "w_out": w_out.astype(BF16),
        "w_up": w_ffn_up.astype(BF16),
        "conv_w": conv_w,
        "conv_b": conv_b.reshape(1, -1),
        "w_down": w_ffn_down.astype(BF16),
    }


def _layer(x3, tabs, cum0, halo0, prefix, w, tm, tk, rb, pairs):
    q, k, v, gates, cum_end = _proj_call(x3, tabs, cum0, w, tm)
    o = _attn_call(q, k, v, prefix, tk, rb, pairs)
    h1 = _merge_call(x3, o, gates, w, min(2 * tm, x3.shape[1]))
    if halo0 is None:
        return None, (k[0], v[0]), cum_end[0], _gate_tail_call(h1, w["w_up"])
    out = _ffn_call(h1, halo0, w, min(2 * tm, x3.shape[1]))
    return out, None, None, None


def kernel(x, meta_tokens, ln_emb_g, ln_emb_b, w_in, b_gate, b_forget, q_norm_g, w_q_up, kv_norm_g, w_kv_up, w_branch_mla, w_branch_fox, w_out, ln_mix_g, ln_mix_b, w_ffn_up, conv_w, conv_b, w_ffn_down, ln_ffn_g, ln_ffn_b):
    seq = x.shape[1]
    w = _prepare_weights(w_in[0], b_gate[0], b_forget[0], q_norm_g[0], w_q_up[0], kv_norm_g[0],
                         w_kv_up[0], w_branch_mla[0], w_branch_fox[0], w_out[0], w_ffn_up[0],
                         conv_w[0], conv_b[0], w_ffn_down[0])
    w.update(ln_emb_g=ln_emb_g.reshape(1, -1), ln_emb_b=ln_emb_b.reshape(1, -1),
             ln_mix_g=ln_mix_g[0].reshape(1, -1), ln_mix_b=ln_mix_b[0].reshape(1, -1),
             ln_ffn_g=ln_ffn_g[0].reshape(1, -1), ln_ffn_b=ln_ffn_b[0].reshape(1, -1))
    w["sel"], w["sel_const"] = _bias_selectors()
    tabs = _rope_tables(N_META + seq)
    meta_tabs = tuple(t[:N_META] for t in tabs)
    real_tabs = tuple(t[N_META:] for t in tabs)

    _, meta_kv, meta_cum, meta_tail = _layer(
        meta_tokens[None].astype(x.dtype), meta_tabs, jnp.zeros((HALO_ROWS, LANES), F32),
        None, None, w, N_META, N_META, N_META, N_PAIRS)
    out, _, _, _ = _layer(x, real_tabs, meta_cum, meta_tail, meta_kv, w, 512, 256, 128, 2)
    return out
```

```python
import functools

import numpy as np
import jax
import jax.numpy as jnp
from jax import lax
from jax.experimental import pallas as pl
from jax.experimental.pallas import tpu as pltpu

D_MODEL = 1024
N_META = 16
MLA_HEADS = 8
MLA_Q_RANK = 384
MLA_KV_RANK = 128
MLA_NOPE_DIM = 64
MLA_ROPE_DIM = 32
MLA_QK_DIM = MLA_NOPE_DIM + MLA_ROPE_DIM
MLA_V_DIM = 64
ROPE_THETA = 10000.0
FOX_HEADS = 8
FOX_HEAD_DIM = 64
FOX_WIDTH = FOX_HEADS * FOX_HEAD_DIM
D_FF = 2816
LN_EPS = 1e-5
RMS_EPS = 1e-6
DEPTH = 1
DN_ALPHA = (2 * DEPTH) ** 0.25
NEG_INF = -1e30
LOG2E = 1.4426950408889634

LANES = 128
N_PAIRS = 8
PAIR_QK = 2 * LANES
PAIR_V = 2 * MLA_V_DIM
QK_WIDTH = N_PAIRS * PAIR_QK
V_WIDTH = N_PAIRS * PAIR_V
VX_WIDTH = N_PAIRS * PAIR_QK
EXTRA_GROUP = 32
FF_CHUNK = 256
N_FF_CHUNKS = D_FF // FF_CHUNK
HALO_ROWS = 8
SUB_ROWS = 256
PROJ_ROWS = 512
WIDE_ROWS = 1024
ATTN_KEYS = 256
ATTN_ROWS = 128
ATTN_PAIRS = 2
VMEM_LIMIT = 56 * 1024 * 1024

C_QLAT = 0
C_KVLAT = C_QLAT + MLA_Q_RANK
C_KROPE = C_KVLAT + MLA_KV_RANK
C_FLOGIT = C_KROPE + LANES
C_FQ = C_FLOGIT + LANES
C_FK = C_FQ + FOX_WIDTH
C_FV = C_FK + FOX_WIDTH
C_GATE = C_FV + FOX_WIDTH
C_TOTAL = C_GATE + 2 * D_MODEL

F32 = jnp.float32
BF16 = jnp.bfloat16


def _dot(a, b):
    return jnp.dot(a, b, preferred_element_type=F32)


def _dot_nt(a, b):
    return lax.dot_general(a, b, (((1,), (1,)), ((), ())), preferred_element_type=F32)


def _layer_norm(x, g, b):
    mu = jnp.mean(x, axis=-1, keepdims=True)
    xc = x - mu
    var = jnp.mean(xc * xc, axis=-1, keepdims=True)
    return xc * lax.rsqrt(var + LN_EPS) * g + b


def _rms_norm(x, g):
    return x * lax.rsqrt(jnp.mean(x * x, axis=-1, keepdims=True) + RMS_EPS) * g


def _resident(shape):
    zeros = (0,) * len(shape)
    return pl.BlockSpec(shape, lambda *_: zeros, pipeline_mode=pl.Buffered(1))


def _params():
    return pltpu.CompilerParams(dimension_semantics=("arbitrary", "arbitrary"),
                                vmem_limit_bytes=VMEM_LIMIT)


def _rope(t, cos_t, sa_t, sb_t):
    return (t * cos_t + pltpu.roll(t, MLA_ROPE_DIM // 2, 1) * sa_t
            + pltpu.roll(t, LANES - MLA_ROPE_DIM // 2, 1) * sb_t)


def _cumsum_rows(x):
    rows = x.shape[0]
    row = lax.broadcasted_iota(jnp.int32, x.shape, 0)
    step = 1
    while step < rows:
        x = x + jnp.where(row >= step, pltpu.roll(x, step, 0), 0.0)
        step *= 2
    return x


def _store_values(v_ref, first_pair, v):
    low_half = lax.broadcasted_iota(jnp.int32, (1, LANES), 1) < MLA_V_DIM
    one = jnp.ones((v.shape[0], LANES), v.dtype)
    for p in range(N_PAIRS // 2):
        pair = v[:, p * LANES:(p + 1) * LANES]
        col = (first_pair + p) * PAIR_QK
        v_ref[0, :, col:col + LANES] = jnp.where(low_half, pair, one)
        v_ref[0, :, col + LANES:col + PAIR_QK] = jnp.where(low_half, one, pair)


def _proj_kernel(x_ref, lng_ref, lnb_ref, win_ref, bgate_ref, bforget_ref, qng_ref, wq_ref,
                 kvng_ref, wk_ref, wv_ref, cos_ref, sa_ref, sb_ref, sel_ref, selc_ref, cum0_ref,
                 q_ref, k_ref, v_ref, g_ref, cumend_ref,
                 carry_ref):
    @pl.when(pl.program_id(1) == 0)
    def _():
        carry_ref[...] = cum0_ref[...]

    hb = _layer_norm(x_ref[0], lng_ref[...], lnb_ref[...]).astype(BF16)
    cos_t, sa_t, sb_t = cos_ref[...], sa_ref[...], sb_ref[...]
    mla_scale = MLA_QK_DIM ** -0.5 * LOG2E
    fox_scale = FOX_HEAD_DIM ** -0.5 * LOG2E
    n_half = N_PAIRS // 2

    lat = _dot_nt(hb, win_ref[C_QLAT:C_KROPE, :])
    small = _dot_nt(hb, win_ref[C_KROPE:C_FQ, :])
    gate_chunk = 512
    for c in range(2 * D_MODEL // gate_chunk):
        sl = slice(c * gate_chunk, (c + 1) * gate_chunk)
        zg = _dot_nt(hb, win_ref[C_GATE + c * gate_chunk:C_GATE + (c + 1) * gate_chunk, :])
        zg = zg + bgate_ref[:, sl]
        g_ref[0, :, sl] = (1.0 / (1.0 + jnp.exp(-zg))).astype(BF16)

    qn = _rms_norm(lat[:, :MLA_Q_RANK], qng_ref[...]).astype(BF16)
    kvn = _rms_norm(lat[:, MLA_Q_RANK:], kvng_ref[...]).astype(BF16)
    for p in range(n_half):
        qp = _dot(qn, wq_ref[:, p * PAIR_QK:(p + 1) * PAIR_QK]) * mla_scale
        q_ref[0, :, p* PAIR_QK:p * PAIR_QK + LANES] = qp[:, :LANES].astype(BF16)
        q_ref[0, :, p* PAIR_QK + LANES:(p + 1) * PAIR_QK] = _rope(
            qp[:, LANES:], cos_t, sa_t, sb_t).astype(BF16)
    k_pe = _rope(small[:, :LANES], cos_t, sa_t, sb_t).astype(BF16)
    k_nope = _dot(kvn, wk_ref[...]).astype(BF16)
    for p in range(n_half):
        k_ref[0, :, p* PAIR_QK:p * PAIR_QK + LANES] = k_nope[:, p * LANES:(p + 1) * LANES]
        k_ref[0, :, p* PAIR_QK + LANES:(p + 1) * PAIR_QK] = k_pe
    _store_values(v_ref, 0, _dot(kvn, wv_ref[...]).astype(BF16))

    z = small[:, LANES:] + bforget_ref[...]
    log_f = -(jnp.maximum(-z, 0.0) + jnp.log1p(jnp.exp(-jnp.abs(z))))
    cum = _cumsum_rows(log_f) + carry_ref[0:1, :]
    rows = cum.shape[0]
    carry_ref[...] = jnp.broadcast_to(cum[rows - 1:rows, :], carry_ref.shape)
    cumend_ref[0] = jnp.broadcast_to(cum[rows - 1:rows, :], carry_ref.shape)
    head_lane = lax.broadcasted_iota(jnp.int32, (1, LANES), 1) < FOX_HEADS
    cum = jnp.where(head_lane, cum * LOG2E, 0.0)
    c_hi = cum.astype(BF16).astype(F32)
    r1 = cum - c_hi
    c_mid = r1.astype(BF16).astype(F32)
    c_lo = (r1 - c_mid).astype(BF16).astype(F32)
    pieces = (c_hi + pltpu.roll(c_mid, FOX_HEADS, 1) + pltpu.roll(c_lo, 2 * FOX_HEADS, 1))
    aug = (_dot(pieces.astype(BF16), sel_ref[...]) + selc_ref[...]).astype(BF16)
    base = n_half * PAIR_QK
    for p in range(n_half):
        lo_col = base + p * PAIR_QK + LANES
        q_ref[0, :, lo_col:lo_col + LANES] = aug[:, p * LANES:(p + 1) * LANES]
        k_ref[0, :, lo_col:lo_col + LANES] = aug[:, (n_half + p) * LANES:(n_half + p + 1) * LANES]

    fq = (_dot_nt(hb, win_ref[C_FQ:C_FQ + FOX_WIDTH, :]) * fox_scale).astype(BF16)
    fk = _dot_nt(hb, win_ref[C_FK:C_FK + FOX_WIDTH, :]).astype(BF16)
    for p in range(n_half):
        lo_col = base + p * PAIR_QK
        q_ref[0, :, lo_col:lo_col + LANES] = fq[:, p * LANES:(p + 1) * LANES]
        k_ref[0, :, lo_col:lo_col + LANES] = fk[:, p * LANES:(p + 1) * LANES]
    _store_values(v_ref, n_half, _dot_nt(hb, win_ref[C_FV:C_FV + FOX_WIDTH, :]).astype(BF16))


def _proj_call(x3, tabs, cum0, w, tm):
    nb, seq, _ = x3.shape
    row_blk = lambda width: pl.BlockSpec((1, tm, width), lambda b, j: (b, j, 0))
    tab_blk = pl.BlockSpec((tm, LANES), lambda b, j: (j, 0))
    consts = [w["ln_emb_g"], w["ln_emb_b"], w["w_in"], w["b_gate"], w["b_forget"], w["q_norm_g"],
              w["w_q"], w["kv_norm_g"], w["w_k"], w["w_v"]]
    sels = [w["sel"], w["sel_const"], cum0]
    in_specs = ([row_blk(D_MODEL)] + [_resident(a.shape) for a in consts] + [tab_blk] * 3
                + [_resident(a.shape) for a in sels])
    out_shape = (jax.ShapeDtypeStruct((nb, seq, QK_WIDTH), BF16),
                 jax.ShapeDtypeStruct((nb, seq, QK_WIDTH), BF16),
                 jax.ShapeDtypeStruct((nb, seq, VX_WIDTH), BF16),
                 jax.ShapeDtypeStruct((nb, seq, 2 * D_MODEL), BF16),
                 jax.ShapeDtypeStruct((nb, HALO_ROWS, LANES), F32))
    out_specs = (row_blk(QK_WIDTH), row_blk(QK_WIDTH), row_blk(VX_WIDTH), row_blk(2 * D_MODEL),
                 pl.BlockSpec((1, HALO_ROWS, LANES), lambda b, j: (b, 0, 0)))
    return pl.pallas_call(
        _proj_kernel, grid=(nb, seq // tm), in_specs=in_specs, out_specs=out_specs,
        out_shape=out_shape, scratch_shapes=[pltpu.VMEM((HALO_ROWS, LANES), F32)],
        compiler_params=_params(), name="proj",
    )(x3, *consts, *tabs, *sels)


def _attn_kernel(*refs, tk, rb, n_prefix, pairs):
    for pp in range(pairs):
        qk = slice(pp * PAIR_QK, (pp + 1) * PAIR_QK)
        views = [r.at[:, :, qk] for r in refs[:3]]
        if n_prefix:
            views += [r.at[:, qk] for r in refs[3:5]]
        o_ref, m_ref, acc_ref = refs[-3:]
        views += [o_ref.at[:, :, pp * PAIR_V:(pp + 1) * PAIR_V], m_ref.at[2 * pp:2 * pp + 2],
                  acc_ref.at[2 * pp:2 * pp + 2]]
        _attn_pair(*views, tk=tk, rb=rb, n_prefix=n_prefix)


def _attn_pair(*refs, tk, rb, n_prefix):
    if n_prefix:
        q_ref, k_ref, v_ref, kp_ref, vp_ref, o_ref, m_ref, acc_ref = refs
    else:
        q_ref, k_ref, v_ref, o_ref, m_ref, acc_ref = refs
    q = q_ref[0]
    tq = q.shape[0]
    lane = lax.broadcasted_iota(jnp.int32, (1, PAIR_QK), 1)
    head_a = (lane < MLA_NOPE_DIM) | ((lane >= LANES) & (lane < LANES + EXTRA_GROUP))
    head_b = (((lane >= MLA_NOPE_DIM) & (lane < LANES))
              | ((lane >= LANES + EXTRA_GROUP) & (lane < LANES + 2 * EXTRA_GROUP)))
    zero = jnp.zeros_like(q)
    qs = (jnp.where(head_a, q, zero), jnp.where(head_b, q, zero))
    nt_dims = (((1,), (1,)), ((), ()))
    key_lane = lax.broadcasted_iota(jnp.int32, (1, LANES), 1)

    if n_prefix:
        grp_a = key_lane < n_prefix
        grp_b = (key_lane >= n_prefix) & (key_lane < 2 * n_prefix)
        s_all = lax.dot_general(q, kp_ref[...], nt_dims, preferred_element_type=F32)
        for r0 in range(0, tq, rb):
            rows = slice(r0, r0 + rb)
            s = s_all[rows, :]
            m_a = jnp.max(jnp.where(grp_a, s, NEG_INF), axis=1, keepdims=True)
            m_b = jnp.max(jnp.where(grp_b, s, NEG_INF), axis=1, keepdims=True)
            x = jnp.where(grp_a, s - m_a, jnp.where(grp_b, s - m_b, NEG_INF))
            p = jnp.exp2(x.astype(BF16))
            pv = _dot(p, vp_ref[...])
            for h, m_h in ((0, m_a), (1, m_b)):
                m_ref[h, rows, :] = jnp.broadcast_to(m_h, (rb, LANES))
                acc_ref[h, rows, :] = pv[:, h * LANES:(h + 1) * LANES]
    else:
        m_ref[...] = jnp.full(m_ref.shape, NEG_INF, F32)
        acc_ref[...] = jnp.zeros(acc_ref.shape, F32)

    def scores(h, key0):
        row0 = (key0 // rb) * rb
        return lax.dot_general(qs[h][row0:, :], k_ref[0, key0:key0 + tk, :], nt_dims,
                               preferred_element_type=F32)

    def fold(h, r0, key0, s):
        width = s.shape[1]
        step = min(width, LANES)
        tiles = width // step
        rows = slice(r0, r0 + rb)
        m_cur = s[:, :step]
        for t in range(1, tiles):
            m_cur = jnp.maximum(m_cur, s[:, t * step:(t + 1) * step])
        m_prev = m_ref[h, rows, :]
        m_new = jnp.maximum(m_prev, jnp.max(m_cur, axis=1, keepdims=True))
        alpha = jnp.exp2(m_prev - m_new)
        m_wide = m_new[:, :step] if tiles == 1 else jnp.tile(m_new, (1, tiles))
        p = jnp.exp2((s - m_wide).astype(BF16))
        pv = _dot(p, v_ref[0, key0:key0 + width, h * LANES:(h + 1) * LANES])
        acc_ref[h, rows, :] = alpha * acc_ref[h, rows, :] + pv
        m_ref[h, rows, :] = m_new

    def update(keys, s_tiles):
        first = keys[0]
        for r0 in range((first // rb) * rb, tq, rb):
            if r0 + rb <= first:
                continue
            seen = [i for i, key0 in enumerate(keys) if r0 + rb > key0]
            width = len(seen) * tk
            for h in range(2):
                parts = []
                for i in seen:
                    row0 = (keys[i] // rb) * rb
                    parts.append(s_tiles[i][h][r0 - row0:r0 - row0 + rb, :])
                s = parts[0] if len(parts) == 1 else jnp.concatenate(parts, axis=1)
                if r0 < first + width - 1:
                    row = lax.broadcasted_iota(jnp.int32, (rb, width), 0) + r0
                    col = lax.broadcasted_iota(jnp.int32, (rb, width), 1) + first
                    s = jnp.where(col <= row, s, NEG_INF)
                fold(h, r0, first, s)

    key_tiles = list(range(0, tq, tk))
    per_group = 2 if len(key_tiles) % 2 == 0 else 1
    groups = [key_tiles[i:i + per_group] for i in range(0, len(key_tiles), per_group)]
    group_scores = lambda keys: [[scores(h, key0) for h in range(2)] for key0 in keys]
    ahead = 3
    pending = [group_scores(keys) for keys in groups[:ahead]]
    for c, keys in enumerate(groups):
        s_cur = pending.pop(0)
        if c + ahead < len(groups):
            pending.append(group_scores(groups[c + ahead]))
        update(keys, s_cur)

    acc_a, acc_b = acc_ref[0], acc_ref[1]
    o = jnp.where(key_lane < MLA_V_DIM, acc_a / pltpu.roll(acc_a, MLA_V_DIM, 1),
                  acc_b / pltpu.roll(acc_b, MLA_V_DIM, 1))
    o_ref[0] = o.astype(BF16)


def _pair_prefix(k_meta, v_meta):
    n = k_meta.shape[0]
    lane = jnp.arange(QK_WIDTH) % PAIR_QK
    k_a = (lane < MLA_NOPE_DIM) | ((lane >= LANES) & (lane < LANES + EXTRA_GROUP))
    k_b = (((lane >= MLA_NOPE_DIM) & (lane < LANES))
           | ((lane >= LANES + EXTRA_GROUP) & (lane < LANES + 2 * EXTRA_GROUP)))
    v_a = lane < LANES
    zero_k, zero_v = jnp.zeros_like(k_meta), jnp.zeros_like(v_meta)
    pad = ((0, LANES - 2 * n), (0, 0))
    kp = jnp.concatenate([jnp.where(k_a, k_meta, zero_k), jnp.where(k_b, k_meta, zero_k)])
    vp = jnp.concatenate([jnp.where(v_a, v_meta, zero_v), jnp.where(v_a, zero_v, v_meta)])
    return jnp.pad(kp, pad), jnp.pad(vp, pad)


def _attn_call(q, k, v, prefix, tk, rb, pairs):
    nb, seq, _ = q.shape
    n_prefix = 0 if prefix is None else N_META
    wide = pl.BlockSpec((1, seq, pairs * PAIR_QK), lambda b, p: (b, 0, p))
    in_specs = [wide, wide, wide]
    args = [q, k, v]
    if n_prefix:
        in_specs += [pl.BlockSpec((LANES, pairs * PAIR_QK), lambda b, p: (0, p))] * 2
        args += list(_pair_prefix(*prefix))
    return pl.pallas_call(
        functools.partial(_attn_kernel, tk=tk, rb=rb, n_prefix=n_prefix, pairs=pairs),
        grid=(nb, N_PAIRS // pairs), in_specs=in_specs,
        out_specs=pl.BlockSpec((1, seq, pairs * PAIR_V), lambda b, p: (b, 0, p)),
        out_shape=jax.ShapeDtypeStruct((nb, seq, V_WIDTH), BF16),
        scratch_shapes=[pltpu.VMEM((2 * pairs, seq, LANES), F32),
                        pltpu.VMEM((2 * pairs, seq, PAIR_V), F32)],
        compiler_params=_params(),
        name="attn",
    )(*args)


def _merge_kernel(x_ref, o_ref, g_ref, lng_ref, lnb_ref, wbm_ref, wbf_ref, wout_ref, mixg_ref,
                  mixb_ref, h_ref):
    half = V_WIDTH // 2
    n_rows = x_ref.shape[1]
    sub = min(n_rows, SUB_ROWS)
    parts = [slice(r, r + sub) for r in range(0, n_rows, sub)]

    def branches(rows):
        return (_dot(o_ref[0, rows, :half], wbm_ref[...]),
                _dot(o_ref[0, rows, half:], wbf_ref[...]))

    def mix(rows, o_mla, o_fox):
        merged = (g_ref[0, rows, :D_MODEL].astype(F32) * o_mla
                  + g_ref[0, rows, D_MODEL:].astype(F32) * o_fox)
        return _dot(merged.astype(BF16), wout_ref[...])

    def finish(rows, mixed):
        h0 = _layer_norm(x_ref[0, rows, :], lng_ref[...], lnb_ref[...])
        h_ref[0, rows, :] = _layer_norm(DN_ALPHA * h0 + mixed, mixg_ref[...], mixb_ref[...])

    projected = [branches(rows) for rows in parts]
    mixed = [mix(rows, *pr) for rows, pr in zip(parts, projected)]
    for rows, mx in zip(parts, mixed):
        finish(rows, mx)


def _merge_call(x3, o, g, w, tm):
    nb, seq, _ = x3.shape
    row_blk = lambda width: pl.BlockSpec((1, tm, width), lambda b, j: (b, j, 0))
    consts = [w["ln_emb_g"], w["ln_emb_b"], w["w_bm"], w["w_bf"], w["w_out"], w["ln_mix_g"],
              w["ln_mix_b"]]
    return pl.pallas_call(
        _merge_kernel, grid=(nb, seq // tm),
        in_specs=[row_blk(D_MODEL), row_blk(V_WIDTH), row_blk(2 * D_MODEL)]
        + [_resident(a.shape) for a in consts],
        out_specs=row_blk(D_MODEL), out_shape=jax.ShapeDtypeStruct((nb, seq, D_MODEL), F32),
        compiler_params=_params(), name="merge",
    )(x3, o, g, *consts)


def _ffn_kernel(h_ref, halo0_ref, wup_ref, cw_ref, cb_ref, wdown_ref, lng_ref, lnb_ref,
                out_ref, halo_ref, gate_ref, act_ref):
    @pl.when(pl.program_id(1) == 0)
    def _():
        halo_ref[...] = halo0_ref[...]

    n_rows = h_ref.shape[1]
    sub = min(n_rows, SUB_ROWS)

    def gated(r0):
        rows = slice(r0, r0 + sub)
        hb = h_ref[0, rows, :].astype(BF16)
        for c in range(N_FF_CHUNKS):
            sl = slice(c * FF_CHUNK, (c + 1) * FF_CHUNK)
            gate = _dot(hb, wup_ref[:, sl])
            val = _dot(hb, wup_ref[:, D_FF + c * FF_CHUNK:D_FF + (c + 1) * FF_CHUNK])
            gate_ref[c, :HALO_ROWS, :] = halo_ref[:, sl]
            gate_ref[c, HALO_ROWS:, :] = gate
            halo_ref[:, sl] = gate[sub - HALO_ROWS:, :]
            back1 = gate_ref[c, HALO_ROWS - 1:HALO_ROWS - 1 + sub, :]
            back2 = gate_ref[c, HALO_ROWS - 2:HALO_ROWS - 2 + sub, :]
            conv = (cw_ref[0:1, sl] * back2 + cw_ref[1:2, sl] * back1 + cw_ref[2:3, sl] * gate
                    + cb_ref[:, sl])
            act_ref[rows, sl] = (conv * (1.0 / (1.0 + jnp.exp(-conv))) * val).astype(BF16)

    starts = list(range(0, n_rows, sub))
    for r0 in starts:
        gated(r0)
    down = [_dot(act_ref[r0:r0 + sub, :], wdown_ref[...]) for r0 in starts]
    for r0, ffn in zip(starts, down):
        rows = slice(r0, r0 + sub)
        out_ref[0, rows, :] = _layer_norm(DN_ALPHA * h_ref[0, rows, :] + ffn, lng_ref[...],
                                          lnb_ref[...])


def _ffn_call(h1, halo0, w, tm):
    nb, seq, _ = h1.shape
    row_blk = pl.BlockSpec((1, tm, D_MODEL), lambda b, j: (b, j, 0))
    consts = [halo0, w["w_up"], w["conv_w"], w["conv_b"], w["w_down"], w["ln_ffn_g"],
              w["ln_ffn_b"]]
    return pl.pallas_call(
        _ffn_kernel, grid=(nb, seq // tm),
        in_specs=[row_blk] + [_resident(a.shape) for a in consts],
        out_specs=row_blk, out_shape=jax.ShapeDtypeStruct((nb, seq, D_MODEL), F32),
        scratch_shapes=[pltpu.VMEM((HALO_ROWS, D_FF), F32),
                        pltpu.VMEM((N_FF_CHUNKS, HALO_ROWS + min(tm, SUB_ROWS), FF_CHUNK), F32),
                        pltpu.VMEM((tm, D_FF), BF16)],
        compiler_params=_params(), name="ffn",
    )(h1, *consts)


def _gate_tail_kernel(h_ref, wup_ref, tail_ref):
    gate = _dot(h_ref[0].astype(BF16), wup_ref[...])
    tail_ref[...] = gate[gate.shape[0] - HALO_ROWS:, :]


def _gate_tail_call(h1, w_up):
    return pl.pallas_call(
        _gate_tail_kernel, grid=(1,),
        in_specs=[pl.BlockSpec(h1.shape, lambda i: (0, 0, 0)),
                  pl.BlockSpec((D_MODEL, D_FF), lambda i: (0, 0))],
        out_specs=pl.BlockSpec((HALO_ROWS, D_FF), lambda i: (0, 0)),
        out_shape=jax.ShapeDtypeStruct((HALO_ROWS, D_FF), F32),
        compiler_params=pltpu.CompilerParams(dimension_semantics=("arbitrary",),
                                             vmem_limit_bytes=VMEM_LIMIT),
        name="gate_tail",
    )(h1, w_up)


def _rope_tables(length):
    half = MLA_ROPE_DIM // 2
    inv_freq = ROPE_THETA ** (-jnp.arange(half, dtype=F32) / half)
    ang = jnp.arange(length).astype(F32)[:, None] * inv_freq[None, :]
    cos, sin = jnp.cos(ang), jnp.sin(ang)
    zero = jnp.zeros_like(cos)
    widen = lambda a, b: jnp.pad(jnp.tile(jnp.concatenate([a, b], axis=1), (1, 2)),
                                 ((0, 0), (0, LANES - 2 * MLA_ROPE_DIM)))
    return widen(cos, cos), widen(zero, sin), widen(-sin, zero)


def _bias_selectors():
    half = (N_PAIRS // 2) * LANES
    sel = np.zeros((LANES, 2 * half), np.float32)
    const = np.zeros((1, 2 * half), np.float32)
    for h in range(FOX_HEADS):
        base = LANES * (h // 2) + EXTRA_GROUP * (h % 2)
        for piece in range(3):
            sel[piece * FOX_HEADS + h, base + piece] = 1.0
            const[0, base + 3 + piece] = 1.0
            const[0, half + base + piece] = 1.0
            sel[piece * FOX_HEADS + h, half + base + 3 + piece] = -1.0
    return jnp.asarray(sel, BF16), jnp.asarray(const)


def _w_in_kernel(w_ref, o_ref):
    o_krope = MLA_Q_RANK + MLA_KV_RANK
    o_fq = o_krope + MLA_ROPE_DIM
    o_flogit = o_fq + 3 * FOX_WIDTH
    o_gate = o_flogit + FOX_HEADS
    cols = w_ref.shape[1]
    o_ref[C_QLAT:C_KROPE, :] = w_ref[:o_krope, :].astype(BF16)
    k_rope = w_ref[o_krope:o_fq, :]
    o_ref[C_KROPE:C_FLOGIT, :] = jnp.concatenate(
        [k_rope, k_rope, jnp.zeros((LANES - 2 * MLA_ROPE_DIM, cols), F32)], axis=0).astype(BF16)
    o_ref[C_FLOGIT:C_FQ, :] = jnp.concatenate(
        [w_ref[o_flogit:o_gate, :], jnp.zeros((LANES - FOX_HEADS, cols), F32)],
        axis=0).astype(BF16)
    o_ref[C_FQ:C_GATE, :] = w_ref[o_fq:o_flogit, :].astype(BF16)
    o_ref[C_GATE:, :] = w_ref[o_gate:, :].astype(BF16)


def _regroup_w_in(w_in_t):
    cols = 256
    return pl.pallas_call(
        _w_in_kernel, grid=(D_MODEL // cols,),
        in_specs=[pl.BlockSpec((w_in_t.shape[0], cols), lambda i: (0, i))],
        out_specs=pl.BlockSpec((C_TOTAL, cols), lambda i: (0, i)),
        out_shape=jax.ShapeDtypeStruct((C_TOTAL, D_MODEL), BF16),
        compiler_params=pltpu.CompilerParams(dimension_semantics=("arbitrary",),
                                             vmem_limit_bytes=VMEM_LIMIT),
        name="w_in_regroup",
    )(w_in_t)


def _prepare_weights(w_in, b_gate, b_forget, q_norm_g, w_q_up, kv_norm_g, w_kv_up, w_branch_mla,
                     w_branch_fox, w_out, w_ffn_up, conv_w, conv_b, w_ffn_down):
    n_half = N_PAIRS // 2
    w_in_p = _regroup_w_in(w_in.T)
    wq = w_q_up.reshape(MLA_Q_RANK, n_half, 2, MLA_QK_DIM)
    wq_p = jnp.concatenate([
        wq[..., :MLA_NOPE_DIM].reshape(MLA_Q_RANK, n_half, LANES),
        wq[..., MLA_NOPE_DIM:].reshape(MLA_Q_RANK, n_half, 2 * MLA_ROPE_DIM),
        jnp.zeros((MLA_Q_RANK, n_half, LANES - 2 * MLA_ROPE_DIM), F32)], axis=-1)
    wkv = w_kv_up.reshape(MLA_KV_RANK, MLA_HEADS, MLA_NOPE_DIM + MLA_V_DIM)
    return {
        "w_in": w_in_p,
        "b_gate": b_gate.reshape(1, -1),
        "b_forget": jnp.pad(b_forget.reshape(1, -1), ((0, 0), (0, LANES - FOX_HEADS))),
        "q_norm_g": q_norm_g.reshape(1, -1),
        "w_q": wq_p.reshape(MLA_Q_RANK, n_half * PAIR_QK).astype(BF16),
        "kv_norm_g": kv_norm_g.reshape(1, -1),
        "w_k": wkv[..., :MLA_NOPE_DIM].reshape(MLA_KV_RANK, -1).astype(BF16),
        "w_v": wkv[..., MLA_NOPE_DIM:].reshape(MLA_KV_RANK, -1).astype(BF16),
        "w_bm": w_branch_mla.astype(BF16),
        "w_bf": w_branch_fox.astype(BF16),
        "w_out": w_out.astype(BF16),
        "w_up": w_ffn_up.astype(BF16),
        "conv_w": conv_w,
        "conv_b": conv_b.reshape(1, -1),
        "w_down": w_ffn_down.astype(BF16),
    }


def _layer(x3, tabs, cum0, halo0, prefix, w):
    seq = x3.shape[1]
    is_prefix = halo0 is None
    q, k, v, gates, cum_end = _proj_call(x3, tabs, cum0, w, min(PROJ_ROWS, seq))
    o = _attn_call(q, k, v, prefix, min(ATTN_KEYS, seq), min(ATTN_ROWS, seq),
                   N_PAIRS if is_prefix else ATTN_PAIRS)
    h1 = _merge_call(x3, o, gates, w, min(WIDE_ROWS, seq))
    if is_prefix:
        return None, (k[0], v[0]), cum_end[0], _gate_tail_call(h1, w["w_up"])
    return _ffn_call(h1, halo0, w, min(WIDE_ROWS, seq)), None, None, None


def kernel(x, meta_tokens, ln_emb_g, ln_emb_b, w_in, b_gate, b_forget, q_norm_g, w_q_up, kv_norm_g, w_kv_up, w_branch_mla, w_branch_fox, w_out, ln_mix_g, ln_mix_b, w_ffn_up, conv_w, conv_b, w_ffn_down, ln_ffn_g, ln_ffn_b):
    seq = x.shape[1]
    w = _prepare_weights(w_in[0], b_gate[0], b_forget[0], q_norm_g[0], w_q_up[0], kv_norm_g[0],
                         w_kv_up[0], w_branch_mla[0], w_branch_fox[0], w_out[0], w_ffn_up[0],
                         conv_w[0], conv_b[0], w_ffn_down[0])
    w.update(ln_emb_g=ln_emb_g.reshape(1, -1), ln_emb_b=ln_emb_b.reshape(1, -1),
             ln_mix_g=ln_mix_g[0].reshape(1, -1), ln_mix_b=ln_mix_b[0].reshape(1, -1),
             ln_ffn_g=ln_ffn_g[0].reshape(1, -1), ln_ffn_b=ln_ffn_b[0].reshape(1, -1))
    w["sel"], w["sel_const"] = _bias_selectors()
    tabs = _rope_tables(N_META + seq)
    meta_tabs = tuple(t[:N_META] for t in tabs)
    real_tabs = tuple(t[N_META:] for t in tabs)

    _, meta_kv, meta_cum, meta_tail = _layer(
        meta_tokens[None].astype(x.dtype), meta_tabs, jnp.zeros((HALO_ROWS, LANES), F32),
        None, None, w)
    out, _, _, _ = _layer(x, real_tabs, meta_cum, meta_tail, meta_kv, w)
    return out
```

```python
import functools

import numpy as np
import jax
import jax.numpy as jnp
from jax import lax
from jax.experimental import pallas as pl
from jax.experimental.pallas import tpu as pltpu

D_MODEL = 1024
N_META = 16
MLA_HEADS = 8
MLA_Q_RANK = 384
MLA_KV_RANK = 128
MLA_NOPE_DIM = 64
MLA_ROPE_DIM = 32
MLA_QK_DIM = MLA_NOPE_DIM + MLA_ROPE_DIM
MLA_V_DIM = 64
ROPE_THETA = 10000.0
FOX_HEADS = 8
FOX_HEAD_DIM = 64
FOX_WIDTH = FOX_HEADS * FOX_HEAD_DIM
D_FF = 2816
LN_EPS = 1e-5
RMS_EPS = 1e-6
DEPTH = 1
DN_ALPHA = (2 * DEPTH) ** 0.25
NEG_INF = -1e30
LOG2E = 1.4426950408889634

LANES = 128
N_PAIRS = 8
PAIR_QK = 2 * LANES
PAIR_V = 2 * MLA_V_DIM
QK_WIDTH = N_PAIRS * PAIR_QK
V_WIDTH = N_PAIRS * PAIR_V
VX_WIDTH = N_PAIRS * PAIR_QK
EXTRA_GROUP = 32
FF_CHUNK = 256
N_FF_CHUNKS = D_FF // FF_CHUNK
HALO_ROWS = 8
SUB_ROWS = 256
PROJ_ROWS = 512
WIDE_ROWS = 1024
ATTN_KEYS = 256
ATTN_ROWS = 128
ATTN_PAIRS = 2
VMEM_LIMIT = 56 * 1024 * 1024

C_QLAT = 0
C_KVLAT = C_QLAT + MLA_Q_RANK
C_KROPE = C_KVLAT + MLA_KV_RANK
C_FLOGIT = C_KROPE + LANES
C_FQ = C_FLOGIT + LANES
C_FK = C_FQ + FOX_WIDTH
C_FV = C_FK + FOX_WIDTH
C_GATE = C_FV + FOX_WIDTH
C_TOTAL = C_GATE + 2 * D_MODEL

F32 = jnp.float32
BF16 = jnp.bfloat16


def _dot(a, b):
    return jnp.dot(a, b, preferred_element_type=F32)


def _layer_norm(x, g, b):
    mu = jnp.mean(x, axis=-1, keepdims=True)
    xc = x - mu
    var = jnp.mean(xc * xc, axis=-1, keepdims=True)
    return xc * lax.rsqrt(var + LN_EPS) * g + b


def _rms_norm(x, g):
    return x * lax.rsqrt(jnp.mean(x * x, axis=-1, keepdims=True) + RMS_EPS) * g


def _resident(shape):
    zeros = (0,) * len(shape)
    return pl.BlockSpec(shape, lambda *_: zeros, pipeline_mode=pl.Buffered(1))


def _params():
    return pltpu.CompilerParams(dimension_semantics=("arbitrary", "arbitrary"),
                                vmem_limit_bytes=VMEM_LIMIT)


def _rope(t, cos_t, sa_t, sb_t):
    return (t * cos_t + pltpu.roll(t, MLA_ROPE_DIM // 2, 1) * sa_t
            + pltpu.roll(t, LANES - MLA_ROPE_DIM // 2, 1) * sb_t)


def _cumsum_rows(x):
    rows = x.shape[0]
    row = lax.broadcasted_iota(jnp.int32, x.shape, 0)
    step = 1
    while step < rows:
        x = x + jnp.where(row >= step, pltpu.roll(x, step, 0), 0.0)
        step *= 2
    return x


def _store_values(v_ref, first_pair, v):
    low_half = lax.broadcasted_iota(jnp.int32, (1, LANES), 1) < MLA_V_DIM
    one = jnp.ones((v.shape[0], LANES), v.dtype)
    for p in range(N_PAIRS // 2):
        pair = v[:, p * LANES:(p + 1) * LANES]
        col = (first_pair + p) * PAIR_QK
        v_ref[0, :, col:col + LANES] = jnp.where(low_half, pair, one)
        v_ref[0, :, col + LANES:col + PAIR_QK] = jnp.where(low_half, one, pair)


def _proj_kernel(x_ref, lng_ref, lnb_ref, win_ref, bgate_ref, bforget_ref, qng_ref, wq_ref,
                 kvng_ref, wk_ref, wv_ref, cos_ref, sa_ref, sb_ref, sel_ref, selc_ref, cum0_ref,
                 q_ref, k_ref, v_ref, g_ref, cumend_ref,
                 carry_ref):
    @pl.when(pl.program_id(1) == 0)
    def _():
        carry_ref[...] = cum0_ref[...]

    hb = _layer_norm(x_ref[0], lng_ref[...], lnb_ref[...]).astype(BF16)
    cos_t, sa_t, sb_t = cos_ref[...], sa_ref[...], sb_ref[...]
    mla_scale = MLA_QK_DIM ** -0.5 * LOG2E
    fox_scale = FOX_HEAD_DIM ** -0.5 * LOG2E
    n_half = N_PAIRS // 2

    lat = _dot(hb, win_ref[:, C_QLAT:C_KROPE])
    small = _dot(hb, win_ref[:, C_KROPE:C_FQ])
    gate_chunk = 512
    for c in range(2 * D_MODEL // gate_chunk):
        sl = slice(c * gate_chunk, (c + 1) * gate_chunk)
        zg = _dot(hb, win_ref[:, C_GATE + c * gate_chunk:C_GATE + (c + 1) * gate_chunk])
        zg = zg + bgate_ref[:, sl]
        g_ref[0, :, sl] = (1.0 / (1.0 + jnp.exp(-zg))).astype(BF16)

    qn = _rms_norm(lat[:, :MLA_Q_RANK], qng_ref[...]).astype(BF16)
    kvn = _rms_norm(lat[:, MLA_Q_RANK:], kvng_ref[...]).astype(BF16)
    for p in range(n_half):
        qp = _dot(qn, wq_ref[:, p * PAIR_QK:(p + 1) * PAIR_QK]) * mla_scale
        q_ref[0, :, p* PAIR_QK:p * PAIR_QK + LANES] = qp[:, :LANES].astype(BF16)
        q_ref[0, :, p* PAIR_QK + LANES:(p + 1) * PAIR_QK] = _rope(
            qp[:, LANES:], cos_t, sa_t, sb_t).astype(BF16)
    k_pe = _rope(small[:, :LANES], cos_t, sa_t, sb_t).astype(BF16)
    k_nope = _dot(kvn, wk_ref[...]).astype(BF16)
    for p in range(n_half):
        k_ref[0, :, p* PAIR_QK:p * PAIR_QK + LANES] = k_nope[:, p * LANES:(p + 1) * LANES]
        k_ref[0, :, p* PAIR_QK + LANES:(p + 1) * PAIR_QK] = k_pe
    _store_values(v_ref, 0, _dot(kvn, wv_ref[...]).astype(BF16))

    z = small[:, LANES:] + bforget_ref[...]
    log_f = -(jnp.maximum(-z, 0.0) + jnp.log1p(jnp.exp(-jnp.abs(z))))
    cum = _cumsum_rows(log_f) + carry_ref[0:1, :]
    rows = cum.shape[0]
    carry_ref[...] = jnp.broadcast_to(cum[rows - 1:rows, :], carry_ref.shape)
    cumend_ref[0] = jnp.broadcast_to(cum[rows - 1:rows, :], carry_ref.shape)
    head_lane = lax.broadcasted_iota(jnp.int32, (1, LANES), 1) < FOX_HEADS
    cum = jnp.where(head_lane, cum * LOG2E, 0.0)
    c_hi = cum.astype(BF16).astype(F32)
    r1 = cum - c_hi
    c_mid = r1.astype(BF16).astype(F32)
    c_lo = (r1 - c_mid).astype(BF16).astype(F32)
    pieces = (c_hi + pltpu.roll(c_mid, FOX_HEADS, 1) + pltpu.roll(c_lo, 2 * FOX_HEADS, 1))
    aug = (_dot(pieces.astype(BF16), sel_ref[...]) + selc_ref[...]).astype(BF16)
    base = n_half * PAIR_QK
    for p in range(n_half):
        lo_col = base + p * PAIR_QK + LANES
        q_ref[0, :, lo_col:lo_col + LANES] = aug[:, p * LANES:(p + 1) * LANES]
        k_ref[0, :, lo_col:lo_col + LANES] = aug[:, (n_half + p) * LANES:(n_half + p + 1) * LANES]

    fq = (_dot(hb, win_ref[:, C_FQ:C_FQ + FOX_WIDTH]) * fox_scale).astype(BF16)
    fk = _dot(hb, win_ref[:, C_FK:C_FK + FOX_WIDTH]).astype(BF16)
    for p in range(n_half):
        lo_col = base + p * PAIR_QK
        q_ref[0, :, lo_col:lo_col + LANES] = fq[:, p * LANES:(p + 1) * LANES]
        k_ref[0, :, lo_col:lo_col + LANES] = fk[:, p * LANES:(p + 1) * LANES]
    _store_values(v_ref, n_half, _dot(hb, win_ref[:, C_FV:C_FV + FOX_WIDTH]).astype(BF16))


def _proj_call(x3, tabs, cum0, w, tm):
    nb, seq, _ = x3.shape
    row_blk = lambda width: pl.BlockSpec((1, tm, width), lambda b, j: (b, j, 0))
    tab_blk = pl.BlockSpec((tm, LANES), lambda b, j: (j, 0))
    consts = [w["ln_emb_g"], w["ln_emb_b"], w["w_in"], w["b_gate"], w["b_forget"], w["q_norm_g"],
              w["w_q"], w["kv_norm_g"], w["w_k"], w["w_v"]]
    sels = [w["sel"], w["sel_const"], cum0]
    in_specs = ([row_blk(D_MODEL)] + [_resident(a.shape) for a in consts] + [tab_blk] * 3
                + [_resident(a.shape) for a in sels])
    out_shape = (jax.ShapeDtypeStruct((nb, seq, QK_WIDTH), BF16),
                 jax.ShapeDtypeStruct((nb, seq, QK_WIDTH), BF16),
                 jax.ShapeDtypeStruct((nb, seq, VX_WIDTH), BF16),
                 jax.ShapeDtypeStruct((nb, seq, 2 * D_MODEL), BF16),
                 jax.ShapeDtypeStruct((nb, HALO_ROWS, LANES), F32))
    out_specs = (row_blk(QK_WIDTH), row_blk(QK_WIDTH), row_blk(VX_WIDTH), row_blk(2 * D_MODEL),
                 pl.BlockSpec((1, HALO_ROWS, LANES), lambda b, j: (b, 0, 0)))
    return pl.pallas_call(
        _proj_kernel, grid=(nb, seq // tm), in_specs=in_specs, out_specs=out_specs,
        out_shape=out_shape, scratch_shapes=[pltpu.VMEM((HALO_ROWS, LANES), F32)],
        compiler_params=_params(), name="proj",
    )(x3, *consts, *tabs, *sels)


def _attn_kernel(*refs, tk, rb, n_prefix, pairs):
    for pp in range(pairs):
        qk = slice(pp * PAIR_QK, (pp + 1) * PAIR_QK)
        views = [r.at[:, :, qk] for r in refs[:3]]
        if n_prefix:
            views += [r.at[:, qk] for r in refs[3:5]]
        o_ref, m_ref, acc_ref = refs[-3:]
        views += [o_ref.at[:, :, pp * PAIR_V:(pp + 1) * PAIR_V], m_ref.at[2 * pp:2 * pp + 2],
                  acc_ref.at[2 * pp:2 * pp + 2]]
        _attn_pair(*views, tk=tk, rb=rb, n_prefix=n_prefix)


def _attn_pair(*refs, tk, rb, n_prefix):
    if n_prefix:
        q_ref, k_ref, v_ref, kp_ref, vp_ref, o_ref, m_ref, acc_ref = refs
    else:
        q_ref, k_ref, v_ref, o_ref, m_ref, acc_ref = refs
    q = q_ref[0]
    tq = q.shape[0]
    lane = lax.broadcasted_iota(jnp.int32, (1, PAIR_QK), 1)
    head_a = (lane < MLA_NOPE_DIM) | ((lane >= LANES) & (lane < LANES + EXTRA_GROUP))
    head_b = (((lane >= MLA_NOPE_DIM) & (lane < LANES))
              | ((lane >= LANES + EXTRA_GROUP) & (lane < LANES + 2 * EXTRA_GROUP)))
    zero = jnp.zeros_like(q)
    qs = (jnp.where(head_a, q, zero), jnp.where(head_b, q, zero))
    nt_dims = (((1,), (1,)), ((), ()))
    key_lane = lax.broadcasted_iota(jnp.int32, (1, LANES), 1)

    if n_prefix:
        grp_a = key_lane < n_prefix
        grp_b = (key_lane >= n_prefix) & (key_lane < 2 * n_prefix)
        s_all = lax.dot_general(q, kp_ref[...], nt_dims, preferred_element_type=F32)
        for r0 in range(0, tq, rb):
            rows = slice(r0, r0 + rb)
            s = s_all[rows, :]
            m_a = jnp.max(jnp.where(grp_a, s, NEG_INF), axis=1, keepdims=True)
            m_b = jnp.max(jnp.where(grp_b, s, NEG_INF), axis=1, keepdims=True)
            x = jnp.where(grp_a, s - m_a, jnp.where(grp_b, s - m_b, NEG_INF))
            p = jnp.exp2(x.astype(BF16))
            pv = _dot(p, vp_ref[...])
            for h, m_h in ((0, m_a), (1, m_b)):
                m_ref[h, rows, :] = jnp.broadcast_to(m_h, (rb, LANES))
                acc_ref[h, rows, :] = pv[:, h * LANES:(h + 1) * LANES]
    else:
        m_ref[...] = jnp.full(m_ref.shape, NEG_INF, F32)
        acc_ref[...] = jnp.zeros(acc_ref.shape, F32)

    def scores(h, key0):
        row0 = (key0 // rb) * rb
        return lax.dot_general(qs[h][row0:, :], k_ref[0, key0:key0 + tk, :], nt_dims,
                               preferred_element_type=F32)

    def fold(h, r0, key0, s):
        width = s.shape[1]
        step = min(width, LANES)
        tiles = width // step
        rows = slice(r0, r0 + rb)
        m_cur = s[:, :step]
        for t in range(1, tiles):
            m_cur = jnp.maximum(m_cur, s[:, t * step:(t + 1) * step])
        m_prev = m_ref[h, rows, :]
        m_new = jnp.maximum(m_prev, jnp.max(m_cur, axis=1, keepdims=True))
        alpha = jnp.exp2(m_prev - m_new)
        m_wide = m_new[:, :step] if tiles == 1 else jnp.tile(m_new, (1, tiles))
        p = jnp.exp2((s - m_wide).astype(BF16))
        pv = _dot(p, v_ref[0, key0:key0 + width, h * LANES:(h + 1) * LANES])
        acc_ref[h, rows, :] = alpha * acc_ref[h, rows, :] + pv
        m_ref[h, rows, :] = m_new

    def update(keys, s_tiles):
        first = keys[0]
        for r0 in range((first // rb) * rb, tq, rb):
            if r0 + rb <= first:
                continue
            seen = [i for i, key0 in enumerate(keys) if r0 + rb > key0]
            width = len(seen) * tk
            for h in range(2):
                parts = []
                for i in seen:
                    row0 = (keys[i] // rb) * rb
                    parts.append(s_tiles[i][h][r0 - row0:r0 - row0 + rb, :])
                s = parts[0] if len(parts) == 1 else jnp.concatenate(parts, axis=1)
                if r0 < first + width - 1:
                    row = lax.broadcasted_iota(jnp.int32, (rb, width), 0) + r0
                    col = lax.broadcasted_iota(jnp.int32, (rb, width), 1) + first
                    s = jnp.where(col <= row, s, NEG_INF)
                fold(h, r0, first, s)

    key_tiles = list(range(0, tq, tk))
    per_group = 2 if len(key_tiles) % 2 == 0 else 1
    groups = [key_tiles[i:i + per_group] for i in range(0, len(key_tiles), per_group)]
    group_scores = lambda keys: [[scores(h, key0) for h in range(2)] for key0 in keys]
    ahead = 3
    pending = [group_scores(keys) for keys in groups[:ahead]]
    for c, keys in enumerate(groups):
        s_cur = pending.pop(0)
        if c + ahead < len(groups):
            pending.append(group_scores(groups[c + ahead]))
        update(keys, s_cur)

    acc_a, acc_b = acc_ref[0], acc_ref[1]
    o = jnp.where(key_lane < MLA_V_DIM, acc_a / pltpu.roll(acc_a, MLA_V_DIM, 1),
                  acc_b / pltpu.roll(acc_b, MLA_V_DIM, 1))
    o_ref[0] = o.astype(BF16)


def _pair_prefix(k_meta, v_meta):
    n = k_meta.shape[0]
    lane = jnp.arange(QK_WIDTH) % PAIR_QK
    k_a = (lane < MLA_NOPE_DIM) | ((lane >= LANES) & (lane < LANES + EXTRA_GROUP))
    k_b = (((lane >= MLA_NOPE_DIM) & (lane < LANES))
           | ((lane >= LANES + EXTRA_GROUP) & (lane < LANES + 2 * EXTRA_GROUP)))
    v_a = lane < LANES
    zero_k, zero_v = jnp.zeros_like(k_meta), jnp.zeros_like(v_meta)
    pad = ((0, LANES - 2 * n), (0, 0))
    kp = jnp.concatenate([jnp.where(k_a, k_meta, zero_k), jnp.where(k_b, k_meta, zero_k)])
    vp = jnp.concatenate([jnp.where(v_a, v_meta, zero_v), jnp.where(v_a, zero_v, v_meta)])
    return jnp.pad(kp, pad), jnp.pad(vp, pad)


def _attn_call(q, k, v, prefix, tk, rb, pairs):
    nb, seq, _ = q.shape
    n_prefix = 0 if prefix is None else N_META
    wide = pl.BlockSpec((1, seq, pairs * PAIR_QK), lambda b, p: (b, 0, p))
    in_specs = [wide, wide, wide]
    args = [q, k, v]
    if n_prefix:
        in_specs += [pl.BlockSpec((LANES, pairs * PAIR_QK), lambda b, p: (0, p))] * 2
        args += list(_pair_prefix(*prefix))
    return pl.pallas_call(
        functools.partial(_attn_kernel, tk=tk, rb=rb, n_prefix=n_prefix, pairs=pairs),
        grid=(nb, N_PAIRS // pairs), in_specs=in_specs,
        out_specs=pl.BlockSpec((1, seq, pairs * PAIR_V), lambda b, p: (b, 0, p)),
        out_shape=jax.ShapeDtypeStruct((nb, seq, V_WIDTH), BF16),
        scratch_shapes=[pltpu.VMEM((2 * pairs, seq, LANES), F32),
                        pltpu.VMEM((2 * pairs, seq, PAIR_V), F32)],
        compiler_params=_params(),
        name="attn",
    )(*args)


def _merge_kernel(x_ref, o_ref, g_ref, lng_ref, lnb_ref, wbm_ref, wbf_ref, wout_ref, mixg_ref,
                  mixb_ref, h_ref):
    half = V_WIDTH // 2
    n_rows = x_ref.shape[1]
    sub = min(n_rows, SUB_ROWS)
    parts = [slice(r, r + sub) for r in range(0, n_rows, sub)]

    def branches(rows):
        return (_dot(o_ref[0, rows, :half], wbm_ref[...]),
                _dot(o_ref[0, rows, half:], wbf_ref[...]))

    def mix(rows, o_mla, o_fox):
        merged = (g_ref[0, rows, :D_MODEL].astype(F32) * o_mla
                  + g_ref[0, rows, D_MODEL:].astype(F32) * o_fox)
        return _dot(merged.astype(BF16), wout_ref[...])

    def finish(rows, mixed):
        h0 = _layer_norm(x_ref[0, rows, :], lng_ref[...], lnb_ref[...])
        h_ref[0, rows, :] = _layer_norm(DN_ALPHA * h0 + mixed, mixg_ref[...], mixb_ref[...])

    projected = [branches(rows) for rows in parts]
    mixed = [mix(rows, *pr) for rows, pr in zip(parts, projected)]
    for rows, mx in zip(parts, mixed):
        finish(rows, mx)


def _merge_call(x3, o, g, w, tm):
    nb, seq, _ = x3.shape
    row_blk = lambda width: pl.BlockSpec((1, tm, width), lambda b, j: (b, j, 0))
    consts = [w["ln_emb_g"], w["ln_emb_b"], w["w_bm"], w["w_bf"], w["w_out"], w["ln_mix_g"],
              w["ln_mix_b"]]
    return pl.pallas_call(
        _merge_kernel, grid=(nb, seq // tm),
        in_specs=[row_blk(D_MODEL), row_blk(V_WIDTH), row_blk(2 * D_MODEL)]
        + [_resident(a.shape) for a in consts],
        out_specs=row_blk(D_MODEL), out_shape=jax.ShapeDtypeStruct((nb, seq, D_MODEL), F32),
        compiler_params=_params(), name="merge",
    )(x3, o, g, *consts)


def _ffn_kernel(h_ref, halo0_ref, wup_ref, cw_ref, cb_ref, wdown_ref, lng_ref, lnb_ref,
                out_ref, halo_ref, gate_ref, act_ref):
    @pl.when(pl.program_id(1) == 0)
    def _():
        halo_ref[...] = halo0_ref[...]

    n_rows = h_ref.shape[1]
    sub = min(n_rows, SUB_ROWS)

    def gated(r0):
        rows = slice(r0, r0 + sub)
        hb = h_ref[0, rows, :].astype(BF16)
        for c in range(N_FF_CHUNKS):
            sl = slice(c * FF_CHUNK, (c + 1) * FF_CHUNK)
            gate = _dot(hb, wup_ref[:, sl])
            val = _dot(hb, wup_ref[:, D_FF + c * FF_CHUNK:D_FF + (c + 1) * FF_CHUNK])
            gate_ref[c, :HALO_ROWS, :] = halo_ref[:, sl]
            gate_ref[c, HALO_ROWS:, :] = gate
            halo_ref[:, sl] = gate[sub - HALO_ROWS:, :]
            back1 = gate_ref[c, HALO_ROWS - 1:HALO_ROWS - 1 + sub, :]
            back2 = gate_ref[c, HALO_ROWS - 2:HALO_ROWS - 2 + sub, :]
            conv = (cw_ref[0:1, sl] * back2 + cw_ref[1:2, sl] * back1 + cw_ref[2:3, sl] * gate
                    + cb_ref[:, sl])
            act_ref[rows, sl] = (conv * (1.0 / (1.0 + jnp.exp(-conv))) * val).astype(BF16)

    starts = list(range(0, n_rows, sub))
    for r0 in starts:
        gated(r0)
    down = [_dot(act_ref[r0:r0 + sub, :], wdown_ref[...]) for r0 in starts]
    for r0, ffn in zip(starts, down):
        rows = slice(r0, r0 + sub)
        out_ref[0, rows, :] = _layer_norm(DN_ALPHA * h_ref[0, rows, :] + ffn, lng_ref[...],
                                          lnb_ref[...])


def _ffn_call(h1, halo0, w, tm):
    nb, seq, _ = h1.shape
    row_blk = pl.BlockSpec((1, tm, D_MODEL), lambda b, j: (b, j, 0))
    consts = [halo0, w["w_up"], w["conv_w"], w["conv_b"], w["w_down"], w["ln_ffn_g"],
              w["ln_ffn_b"]]
    return pl.pallas_call(
        _ffn_kernel, grid=(nb, seq // tm),
        in_specs=[row_blk] + [_resident(a.shape) for a in consts],
        out_specs=row_blk, out_shape=jax.ShapeDtypeStruct((nb, seq, D_MODEL), F32),
        scratch_shapes=[pltpu.VMEM((HALO_ROWS, D_FF), F32),
                        pltpu.VMEM((N_FF_CHUNKS, HALO_ROWS + min(tm, SUB_ROWS), FF_CHUNK), F32),
                        pltpu.VMEM((tm, D_FF), BF16)],
        compiler_params=_params(), name="ffn",
    )(h1, *consts)


def _gate_tail_kernel(h_ref, wup_ref, tail_ref):
    gate = _dot(h_ref[0].astype(BF16), wup_ref[...])
    tail_ref[...] = gate[gate.shape[0] - HALO_ROWS:, :]


def _gate_tail_call(h1, w_up):
    return pl.pallas_call(
        _gate_tail_kernel, grid=(1,),
        in_specs=[pl.BlockSpec(h1.shape, lambda i: (0, 0, 0)),
                  pl.BlockSpec((D_MODEL, D_FF), lambda i: (0, 0))],
        out_specs=pl.BlockSpec((HALO_ROWS, D_FF), lambda i: (0, 0)),
        out_shape=jax.ShapeDtypeStruct((HALO_ROWS, D_FF), F32),
        compiler_params=pltpu.CompilerParams(dimension_semantics=("arbitrary",),
                                             vmem_limit_bytes=VMEM_LIMIT),
        name="gate_tail",
    )(h1, w_up)


def _rope_tables(length):
    half = MLA_ROPE_DIM // 2
    inv_freq = ROPE_THETA ** (-jnp.arange(half, dtype=F32) / half)
    ang = jnp.arange(length).astype(F32)[:, None] * inv_freq[None, :]
    cos, sin = jnp.cos(ang), jnp.sin(ang)
    zero = jnp.zeros_like(cos)
    widen = lambda a, b: jnp.pad(jnp.tile(jnp.concatenate([a, b], axis=1), (1, 2)),
                                 ((0, 0), (0, LANES - 2 * MLA_ROPE_DIM)))
    return widen(cos, cos), widen(zero, sin), widen(-sin, zero)


def _bias_selectors():
    half = (N_PAIRS // 2) * LANES
    sel = np.zeros((LANES, 2 * half), np.float32)
    const = np.zeros((1, 2 * half), np.float32)
    for h in range(FOX_HEADS):
        base = LANES * (h // 2) + EXTRA_GROUP * (h % 2)
        for piece in range(3):
            sel[piece * FOX_HEADS + h, base + piece] = 1.0
            const[0, base + 3 + piece] = 1.0
            const[0, half + base + piece] = 1.0
            sel[piece * FOX_HEADS + h, half + base + 3 + piece] = -1.0
    return jnp.asarray(sel, BF16), jnp.asarray(const)


def _w_in_kernel(w_ref, o_ref):
    o_krope = MLA_Q_RANK + MLA_KV_RANK
    o_fq = o_krope + MLA_ROPE_DIM
    o_flogit = o_fq + 3 * FOX_WIDTH
    o_gate = o_flogit + FOX_HEADS
    cols = w_ref.shape[1]

    def put(dst, rows):
        o_ref[:, dst:dst + rows.shape[0]] = rows.T.astype(BF16)

    step = 512
    for src0, dst0, n in ((0, C_QLAT, o_krope), (o_fq, C_FQ, 3 * FOX_WIDTH),
                          (o_gate, C_GATE, 2 * D_MODEL)):
        for r in range(0, n, step):
            put(dst0 + r, w_ref[src0 + r:src0 + r + step, :])
    k_rope = w_ref[o_krope:o_fq, :]
    put(C_KROPE, jnp.concatenate(
        [k_rope, k_rope, jnp.zeros((LANES - 2 * MLA_ROPE_DIM, cols), F32)], axis=0))
    put(C_FLOGIT, jnp.concatenate(
        [w_ref[o_flogit:o_gate, :], jnp.zeros((LANES - FOX_HEADS, cols), F32)], axis=0))


def _regroup_w_in(w_in_t):
    cols = 256
    return pl.pallas_call(
        _w_in_kernel, grid=(D_MODEL // cols,),
        in_specs=[pl.BlockSpec((w_in_t.shape[0], cols), lambda i: (0, i))],
        out_specs=pl.BlockSpec((cols, C_TOTAL), lambda i: (i, 0)),
        out_shape=jax.ShapeDtypeStruct((D_MODEL, C_TOTAL), BF16),
        compiler_params=pltpu.CompilerParams(dimension_semantics=("arbitrary",),
                                             vmem_limit_bytes=VMEM_LIMIT),
        name="w_in_regroup",
    )(w_in_t)


def _prepare_weights(w_in, b_gate, b_forget, q_norm_g, w_q_up, kv_norm_g, w_kv_up, w_branch_mla,
                     w_branch_fox, w_out, w_ffn_up, conv_w, conv_b, w_ffn_down):
    n_half = N_PAIRS // 2
    w_in_p = _regroup_w_in(w_in.T)
    wq = w_q_up.reshape(MLA_Q_RANK, n_half, 2, MLA_QK_DIM)
    wq_p = jnp.concatenate([
        wq[..., :MLA_NOPE_DIM].reshape(MLA_Q_RANK, n_half, LANES),
        wq[..., MLA_NOPE_DIM:].reshape(MLA_Q_RANK, n_half, 2 * MLA_ROPE_DIM),
        jnp.zeros((MLA_Q_RANK, n_half, LANES - 2 * MLA_ROPE_DIM), F32)], axis=-1)
    wkv = w_kv_up.reshape(MLA_KV_RANK, MLA_HEADS, MLA_NOPE_DIM + MLA_V_DIM)
    return {
        "w_in": w_in_p,
        "b_gate": b_gate.reshape(1, -1),
        "b_forget": jnp.pad(b_forget.reshape(1, -1), ((0, 0), (0, LANES - FOX_HEADS))),
        "q_norm_g": q_norm_g.reshape(1, -1),
        "w_q": wq_p.reshape(MLA_Q_RANK, n_half * PAIR_QK).astype(BF16),
        "kv_norm_g": kv_norm_g.reshape(1, -1),
        "w_k": wkv[..., :MLA_NOPE_DIM].reshape(MLA_KV_RANK, -1).astype(BF16),
        "w_v": wkv[..., MLA_NOPE_DIM:].reshape(MLA_KV_RANK, -1).astype(BF16),
        "w_bm": w_branch_mla.astype(BF16),
        "w_bf": w_branch_fox.astype(BF16),
        "w_out": w_out.astype(BF16),
        "w_up": w_ffn_up.astype(BF16),
        "conv_w": conv_w,
        "conv_b": conv_b.reshape(1, -1),
        "w_down": w_ffn_down.astype(BF16),
    }


def _layer(x3, tabs, cum0, halo0, prefix, w):
    seq = x3.shape[1]
    is_prefix = halo0 is None
    q, k, v, gates, cum_end = _proj_call(x3, tabs, cum0, w, min(PROJ_ROWS, seq))
    o = _attn_call(q, k, v, prefix, min(ATTN_KEYS, seq), min(ATTN_ROWS, seq),
                   N_PAIRS if is_prefix else ATTN_PAIRS)
    h1 = _merge_call(x3, o, gates, w, min(WIDE_ROWS, seq))
    if is_prefix:
        return None, (k[0], v[0]), cum_end[0], _gate_tail_call(h1, w["w_up"])
    return _ffn_call(h1, halo0, w, min(WIDE_ROWS, seq)), None, None, None


def kernel(x, meta_tokens, ln_emb_g, ln_emb_b, w_in, b_gate, b_forget, q_norm_g, w_q_up, kv_norm_g, w_kv_up, w_branch_mla, w_branch_fox, w_out, ln_mix_g, ln_mix_b, w_ffn_up, conv_w, conv_b, w_ffn_down, ln_ffn_g, ln_ffn_b):
    seq = x.shape[1]
    w = _prepare_weights(w_in[0], b_gate[0], b_forget[0], q_norm_g[0], w_q_up[0], kv_norm_g[0],
                         w_kv_up[0], w_branch_mla[0], w_branch_fox[0], w_out[0], w_ffn_up[0],
                         conv_w[0], conv_b[0], w_ffn_down[0])
    w.update(ln_emb_g=ln_emb_g.reshape(1, -1), ln_emb_b=ln_emb_b.reshape(1, -1),
             ln_mix_g=ln_mix_g[0].reshape(1, -1), ln_mix_b=ln_mix_b[0].reshape(1, -1),
             ln_ffn_g=ln_ffn_g[0].reshape(1, -1), ln_ffn_b=ln_ffn_b[0].reshape(1, -1))
    w["sel"], w["sel_const"] = _bias_selectors()
    tabs = _rope_tables(N_META + seq)
    meta_tabs = tuple(t[:N_META] for t in tabs)
    real_tabs = tuple(t[N_META:] for t in tabs)

    _, meta_kv, meta_cum, meta_tail = _layer(
        meta_tokens[None].astype(x.dtype), meta_tabs, jnp.zeros((HALO_ROWS, LANES), F32),
        None, None, w)
    out, _, _, _ = _layer(x, real_tabs, meta_cum, meta_tail, meta_kv, w)
    return out
```

```python
import functools

import numpy as np
import jax
import jax.numpy as jnp
from jax import lax
from jax.experimental import pallas as pl
from jax.experimental.pallas import tpu as pltpu

D_MODEL = 1024
N_META = 16
MLA_HEADS = 8
MLA_Q_RANK = 384
MLA_KV_RANK = 128
MLA_NOPE_DIM = 64
MLA_ROPE_DIM = 32
MLA_QK_DIM = MLA_NOPE_DIM + MLA_ROPE_DIM
MLA_V_DIM = 64
ROPE_THETA = 10000.0
FOX_HEADS = 8
FOX_HEAD_DIM = 64
FOX_WIDTH = FOX_HEADS * FOX_HEAD_DIM
D_FF = 2816
LN_EPS = 1e-5
RMS_EPS = 1e-6
DEPTH = 1
DN_ALPHA = (2 * DEPTH) ** 0.25
NEG_INF = -1e30
LOG2E = 1.4426950408889634

LANES = 128
N_PAIRS = 8
PAIR_QK = 2 * LANES
PAIR_V = 2 * MLA_V_DIM
QK_WIDTH = N_PAIRS * PAIR_QK
V_WIDTH = N_PAIRS * PAIR_V
VX_WIDTH = N_PAIRS * PAIR_QK
EXTRA_GROUP = 32
FF_CHUNK = 256
N_FF_CHUNKS = D_FF // FF_CHUNK
HALO_ROWS = 8
SUB_ROWS = 512
PROJ_ROWS = 512
WIDE_ROWS = 1024
ATTN_KEYS = 256
ATTN_ROWS = 128
ATTN_PAIRS = 2
VMEM_LIMIT = 56 * 1024 * 1024

C_QLAT = 0
C_KVLAT = C_QLAT + MLA_Q_RANK
C_KROPE = C_KVLAT + MLA_KV_RANK
C_FLOGIT = C_KROPE + LANES
C_FQ = C_FLOGIT + LANES
C_FK = C_FQ + FOX_WIDTH
C_FV = C_FK + FOX_WIDTH
C_GATE = C_FV + FOX_WIDTH
C_TOTAL = C_GATE + 2 * D_MODEL

F32 = jnp.float32
BF16 = jnp.bfloat16


def _dot(a, b):
    return jnp.dot(a, b, preferred_element_type=F32)


def _layer_norm(x, g, b):
    mu = jnp.mean(x, axis=-1, keepdims=True)
    xc = x - mu
    var = jnp.mean(xc * xc, axis=-1, keepdims=True)
    return xc * lax.rsqrt(var + LN_EPS) * g + b


def _rms_norm(x, g):
    return x * lax.rsqrt(jnp.mean(x * x, axis=-1, keepdims=True) + RMS_EPS) * g


def _resident(shape):
    zeros = (0,) * len(shape)
    return pl.BlockSpec(shape, lambda *_: zeros, pipeline_mode=pl.Buffered(1))


def _params():
    return pltpu.CompilerParams(dimension_semantics=("arbitrary", "arbitrary"),
                                vmem_limit_bytes=VMEM_LIMIT)


def _rope(t, cos_t, sa_t, sb_t):
    return (t * cos_t + pltpu.roll(t, MLA_ROPE_DIM // 2, 1) * sa_t
            + pltpu.roll(t, LANES - MLA_ROPE_DIM // 2, 1) * sb_t)


def _cumsum_rows(x):
    rows = x.shape[0]
    row = lax.broadcasted_iota(jnp.int32, x.shape, 0)
    step = 1
    while step < rows:
        x = x + jnp.where(row >= step, pltpu.roll(x, step, 0), 0.0)
        step *= 2
    return x


def _store_values(v_ref, first_pair, v):
    low_half = lax.broadcasted_iota(jnp.int32, (1, LANES), 1) < MLA_V_DIM
    one = jnp.ones((v.shape[0], LANES), v.dtype)
    for p in range(N_PAIRS // 2):
        pair = v[:, p * LANES:(p + 1) * LANES]
        col = (first_pair + p) * PAIR_QK
        v_ref[0, :, col:col + LANES] = jnp.where(low_half, pair, one)
        v_ref[0, :, col + LANES:col + PAIR_QK] = jnp.where(low_half, one, pair)


def _proj_kernel(x_ref, lng_ref, lnb_ref, win_ref, bgate_ref, bforget_ref, qng_ref, wq_ref,
                 kvng_ref, wk_ref, wv_ref, cos_ref, sa_ref, sb_ref, sel_ref, selc_ref, cum0_ref,
                 q_ref, k_ref, v_ref, g_ref, cumend_ref,
                 carry_ref):
    @pl.when(pl.program_id(1) == 0)
    def _():
        carry_ref[...] = cum0_ref[...]

    hb = _layer_norm(x_ref[0], lng_ref[...], lnb_ref[...]).astype(BF16)
    cos_t, sa_t, sb_t = cos_ref[...], sa_ref[...], sb_ref[...]
    mla_scale = MLA_QK_DIM ** -0.5 * LOG2E
    fox_scale = FOX_HEAD_DIM ** -0.5 * LOG2E
    n_half = N_PAIRS // 2

    lat = _dot(hb, win_ref[:, C_QLAT:C_KROPE])
    small = _dot(hb, win_ref[:, C_KROPE:C_FQ])
    gate_chunk = 512
    for c in range(2 * D_MODEL // gate_chunk):
        sl = slice(c * gate_chunk, (c + 1) * gate_chunk)
        zg = _dot(hb, win_ref[:, C_GATE + c * gate_chunk:C_GATE + (c + 1) * gate_chunk])
        zg = zg + bgate_ref[:, sl]
        g_ref[0, :, sl] = (1.0 / (1.0 + jnp.exp(-zg))).astype(BF16)

    qn = _rms_norm(lat[:, :MLA_Q_RANK], qng_ref[...]).astype(BF16)
    kvn = _rms_norm(lat[:, MLA_Q_RANK:], kvng_ref[...]).astype(BF16)
    for p in range(n_half):
        qp = _dot(qn, wq_ref[:, p * PAIR_QK:(p + 1) * PAIR_QK]) * mla_scale
        q_ref[0, :, p* PAIR_QK:p * PAIR_QK + LANES] = qp[:, :LANES].astype(BF16)
        q_ref[0, :, p* PAIR_QK + LANES:(p + 1) * PAIR_QK] = _rope(
            qp[:, LANES:], cos_t, sa_t, sb_t).astype(BF16)
    k_pe = _rope(small[:, :LANES], cos_t, sa_t, sb_t).astype(BF16)
    k_nope = _dot(kvn, wk_ref[...]).astype(BF16)
    for p in range(n_half):
        k_ref[0, :, p* PAIR_QK:p * PAIR_QK + LANES] = k_nope[:, p * LANES:(p + 1) * LANES]
        k_ref[0, :, p* PAIR_QK + LANES:(p + 1) * PAIR_QK] = k_pe
    _store_values(v_ref, 0, _dot(kvn, wv_ref[...]).astype(BF16))

    z = small[:, LANES:] + bforget_ref[...]
    log_f = -(jnp.maximum(-z, 0.0) + jnp.log1p(jnp.exp(-jnp.abs(z))))
    cum = _cumsum_rows(log_f) + carry_ref[0:1, :]
    rows = cum.shape[0]
    carry_ref[...] = jnp.broadcast_to(cum[rows - 1:rows, :], carry_ref.shape)
    cumend_ref[0] = jnp.broadcast_to(cum[rows - 1:rows, :], carry_ref.shape)
    head_lane = lax.broadcasted_iota(jnp.int32, (1, LANES), 1) < FOX_HEADS
    cum = jnp.where(head_lane, cum * LOG2E, 0.0)
    c_hi = cum.astype(BF16).astype(F32)
    r1 = cum - c_hi
    c_mid = r1.astype(BF16).astype(F32)
    c_lo = (r1 - c_mid).astype(BF16).astype(F32)
    pieces = (c_hi + pltpu.roll(c_mid, FOX_HEADS, 1) + pltpu.roll(c_lo, 2 * FOX_HEADS, 1))
    aug = (_dot(pieces.astype(BF16), sel_ref[...]) + selc_ref[...]).astype(BF16)
    base = n_half * PAIR_QK
    for p in range(n_half):
        lo_col = base + p * PAIR_QK + LANES
        q_ref[0, :, lo_col:lo_col + LANES] = aug[:, p * LANES:(p + 1) * LANES]
        k_ref[0, :, lo_col:lo_col + LANES] = aug[:, (n_half + p) * LANES:(n_half + p + 1) * LANES]

    fq = (_dot(hb, win_ref[:, C_FQ:C_FQ + FOX_WIDTH]) * fox_scale).astype(BF16)
    fk = _dot(hb, win_ref[:, C_FK:C_FK + FOX_WIDTH]).astype(BF16)
    for p in range(n_half):
        lo_col = base + p * PAIR_QK
        q_ref[0, :, lo_col:lo_col + LANES] = fq[:, p * LANES:(p + 1) * LANES]
        k_ref[0, :, lo_col:lo_col + LANES] = fk[:, p * LANES:(p + 1) * LANES]
    _store_values(v_ref, n_half, _dot(hb, win_ref[:, C_FV:C_FV + FOX_WIDTH]).astype(BF16))


def _proj_call(x3, tabs, cum0, w, tm):
    nb, seq, _ = x3.shape
    row_blk = lambda width: pl.BlockSpec((1, tm, width), lambda b, j: (b, j, 0))
    tab_blk = pl.BlockSpec((tm, LANES), lambda b, j: (j, 0))
    consts = [w["ln_emb_g"], w["ln_emb_b"], w["w_in"], w["b_gate"], w["b_forget"], w["q_norm_g"],
              w["w_q"], w["kv_norm_g"], w["w_k"], w["w_v"]]
    sels = [w["sel"], w["sel_const"], cum0]
    in_specs = ([row_blk(D_MODEL)] + [_resident(a.shape) for a in consts] + [tab_blk] * 3
                + [_resident(a.shape) for a in sels])
    out_shape = (jax.ShapeDtypeStruct((nb, seq, QK_WIDTH), BF16),
                 jax.ShapeDtypeStruct((nb, seq, QK_WIDTH), BF16),
                 jax.ShapeDtypeStruct((nb, seq, VX_WIDTH), BF16),
                 jax.ShapeDtypeStruct((nb, seq, 2 * D_MODEL), BF16),
                 jax.ShapeDtypeStruct((nb, HALO_ROWS, LANES), F32))
    out_specs = (row_blk(QK_WIDTH), row_blk(QK_WIDTH), row_blk(VX_WIDTH), row_blk(2 * D_MODEL),
                 pl.BlockSpec((1, HALO_ROWS, LANES), lambda b, j: (b, 0, 0)))
    return pl.pallas_call(
        _proj_kernel, grid=(nb, seq // tm), in_specs=in_specs, out_specs=out_specs,
        out_shape=out_shape, scratch_shapes=[pltpu.VMEM((HALO_ROWS, LANES), F32)],
        compiler_params=_params(), name="proj",
    )(x3, *consts, *tabs, *sels)


def _attn_kernel(*refs, tk, rb, n_prefix, pairs):
    for pp in range(pairs):
        qk = slice(pp * PAIR_QK, (pp + 1) * PAIR_QK)
        views = [r.at[:, :, qk] for r in refs[:3]]
        if n_prefix:
            views += [r.at[:, qk] for r in refs[3:5]]
        o_ref, m_ref, acc_ref = refs[-3:]
        views += [o_ref.at[:, :, pp * PAIR_V:(pp + 1) * PAIR_V], m_ref.at[2 * pp:2 * pp + 2],
                  acc_ref.at[2 * pp:2 * pp + 2]]
        _attn_pair(*views, tk=tk, rb=rb, n_prefix=n_prefix)


def _attn_pair(*refs, tk, rb, n_prefix):
    if n_prefix:
        q_ref, k_ref, v_ref, kp_ref, vp_ref, o_ref, m_ref, acc_ref = refs
    else:
        q_ref, k_ref, v_ref, o_ref, m_ref, acc_ref = refs
    q = q_ref[0]
    tq = q.shape[0]
    lane = lax.broadcasted_iota(jnp.int32, (1, PAIR_QK), 1)
    head_a = (lane < MLA_NOPE_DIM) | ((lane >= LANES) & (lane < LANES + EXTRA_GROUP))
    head_b = (((lane >= MLA_NOPE_DIM) & (lane < LANES))
              | ((lane >= LANES + EXTRA_GROUP) & (lane < LANES + 2 * EXTRA_GROUP)))
    zero = jnp.zeros_like(q)
    qs = (jnp.where(head_a, q, zero), jnp.where(head_b, q, zero))
    nt_dims = (((1,), (1,)), ((), ()))
    key_lane = lax.broadcasted_iota(jnp.int32, (1, LANES), 1)

    if n_prefix:
        grp_a = key_lane < n_prefix
        grp_b = (key_lane >= n_prefix) & (key_lane < 2 * n_prefix)
        s_all = lax.dot_general(q, kp_ref[...], nt_dims, preferred_element_type=F32)
        for r0 in range(0, tq, rb):
            rows = slice(r0, r0 + rb)
            s = s_all[rows, :]
            m_a = jnp.max(jnp.where(grp_a, s, NEG_INF), axis=1, keepdims=True)
            m_b = jnp.max(jnp.where(grp_b, s, NEG_INF), axis=1, keepdims=True)
            x = jnp.where(grp_a, s - m_a, jnp.where(grp_b, s - m_b, NEG_INF))
            p = jnp.exp2(x.astype(BF16))
            pv = _dot(p, vp_ref[...])
            for h, m_h in ((0, m_a), (1, m_b)):
                m_ref[h, rows, :] = jnp.broadcast_to(m_h, (rb, LANES))
                acc_ref[h, rows, :] = pv[:, h * LANES:(h + 1) * LANES]
    else:
        m_ref[...] = jnp.full(m_ref.shape, NEG_INF, F32)
        acc_ref[...] = jnp.zeros(acc_ref.shape, F32)

    def scores(h, key0):
        row0 = (key0 // rb) * rb
        return lax.dot_general(qs[h][row0:, :], k_ref[0, key0:key0 + tk, :], nt_dims,
                               preferred_element_type=F32)

    def fold(h, r0, key0, s):
        width = s.shape[1]
        step = min(width, LANES)
        tiles = width // step
        rows = slice(r0, r0 + rb)
        m_cur = s[:, :step]
        for t in range(1, tiles):
            m_cur = jnp.maximum(m_cur, s[:, t * step:(t + 1) * step])
        m_prev = m_ref[h, rows, :]
        m_new = jnp.maximum(m_prev, jnp.max(m_cur, axis=1, keepdims=True))
        alpha = jnp.exp2(m_prev - m_new)
        m_wide = m_new[:, :step] if tiles == 1 else jnp.tile(m_new, (1, tiles))
        p = jnp.exp2((s - m_wide).astype(BF16))
        pv = _dot(p, v_ref[0, key0:key0 + width, h * LANES:(h + 1) * LANES])
        acc_ref[h, rows, :] = alpha * acc_ref[h, rows, :] + pv
        m_ref[h, rows, :] = m_new

    def update(keys, s_tiles):
        first = keys[0]
        for r0 in range((first // rb) * rb, tq, rb):
            if r0 + rb <= first:
                continue
            seen = [i for i, key0 in enumerate(keys) if r0 + rb > key0]
            width = len(seen) * tk
            for h in range(2):
                parts = []
                for i in seen:
                    row0 = (keys[i] // rb) * rb
                    parts.append(s_tiles[i][h][r0 - row0:r0 - row0 + rb, :])
                s = parts[0] if len(parts) == 1 else jnp.concatenate(parts, axis=1)
                if r0 < first + width - 1:
                    row = lax.broadcasted_iota(jnp.int32, (rb, width), 0) + r0
                    col = lax.broadcasted_iota(jnp.int32, (rb, width), 1) + first
                    s = jnp.where(col <= row, s, NEG_INF)
                fold(h, r0, first, s)

    key_tiles = list(range(0, tq, tk))
    per_group = 2 if len(key_tiles) % 2 == 0 else 1
    groups = [key_tiles[i:i + per_group] for i in range(0, len(key_tiles), per_group)]
    group_scores = lambda keys: [[scores(h, key0) for h in range(2)] for key0 in keys]
    ahead = 3
    pending = [group_scores(keys) for keys in groups[:ahead]]
    for c, keys in enumerate(groups):
        s_cur = pending.pop(0)
        if c + ahead < len(groups):
            pending.append(group_scores(groups[c + ahead]))
        update(keys, s_cur)

    acc_a, acc_b = acc_ref[0], acc_ref[1]
    o = jnp.where(key_lane < MLA_V_DIM, acc_a / pltpu.roll(acc_a, MLA_V_DIM, 1),
                  acc_b / pltpu.roll(acc_b, MLA_V_DIM, 1))
    o_ref[0] = o.astype(BF16)


def _pair_prefix(k_meta, v_meta):
    n = k_meta.shape[0]
    lane = jnp.arange(QK_WIDTH) % PAIR_QK
    k_a = (lane < MLA_NOPE_DIM) | ((lane >= LANES) & (lane < LANES + EXTRA_GROUP))
    k_b = (((lane >= MLA_NOPE_DIM) & (lane < LANES))
           | ((lane >= LANES + EXTRA_GROUP) & (lane < LANES + 2 * EXTRA_GROUP)))
    v_a = lane < LANES
    zero_k, zero_v = jnp.zeros_like(k_meta), jnp.zeros_like(v_meta)
    pad = ((0, LANES - 2 * n), (0, 0))
    kp = jnp.concatenate([jnp.where(k_a, k_meta, zero_k), jnp.where(k_b, k_meta, zero_k)])
    vp = jnp.concatenate([jnp.where(v_a, v_meta, zero_v), jnp.where(v_a, zero_v, v_meta)])
    return jnp.pad(kp, pad), jnp.pad(vp, pad)


def _attn_call(q, k, v, prefix, tk, rb, pairs):
    nb, seq, _ = q.shape
    n_prefix = 0 if prefix is None else N_META
    wide = pl.BlockSpec((1, seq, pairs * PAIR_QK), lambda b, p: (b, 0, p))
    in_specs = [wide, wide, wide]
    args = [q, k, v]
    if n_prefix:
        in_specs += [pl.BlockSpec((LANES, pairs * PAIR_QK), lambda b, p: (0, p))] * 2
        args += list(_pair_prefix(*prefix))
    return pl.pallas_call(
        functools.partial(_attn_kernel, tk=tk, rb=rb, n_prefix=n_prefix, pairs=pairs),
        grid=(nb, N_PAIRS // pairs), in_specs=in_specs,
        out_specs=pl.BlockSpec((1, seq, pairs * PAIR_V), lambda b, p: (b, 0, p)),
        out_shape=jax.ShapeDtypeStruct((nb, seq, V_WIDTH), BF16),
        scratch_shapes=[pltpu.VMEM((2 * pairs, seq, LANES), F32),
                        pltpu.VMEM((2 * pairs, seq, PAIR_V), F32)],
        compiler_params=_params(),
        name="attn",
    )(*args)


def _merge_kernel(x_ref, o_ref, g_ref, lng_ref, lnb_ref, wbm_ref, wbf_ref, wout_ref, mixg_ref,
                  mixb_ref, h_ref):
    half = V_WIDTH // 2
    n_rows = x_ref.shape[1]
    sub = min(n_rows, SUB_ROWS)
    parts = [slice(r, r + sub) for r in range(0, n_rows, sub)]

    def branches(rows):
        return (_dot(o_ref[0, rows, :half], wbm_ref[...]),
                _dot(o_ref[0, rows, half:], wbf_ref[...]))

    def mix(rows, o_mla, o_fox):
        merged = (g_ref[0, rows, :D_MODEL].astype(F32) * o_mla
                  + g_ref[0, rows, D_MODEL:].astype(F32) * o_fox)
        return _dot(merged.astype(BF16), wout_ref[...])

    def finish(rows, mixed):
        h0 = _layer_norm(x_ref[0, rows, :], lng_ref[...], lnb_ref[...])
        h_ref[0, rows, :] = _layer_norm(DN_ALPHA * h0 + mixed, mixg_ref[...], mixb_ref[...])

    projected = [branches(rows) for rows in parts]
    mixed = [mix(rows, *pr) for rows, pr in zip(parts, projected)]
    for rows, mx in zip(parts, mixed):
        finish(rows, mx)


def _merge_call(x3, o, g, w, tm):
    nb, seq, _ = x3.shape
    row_blk = lambda width: pl.BlockSpec((1, tm, width), lambda b, j: (b, j, 0))
    consts = [w["ln_emb_g"], w["ln_emb_b"], w["w_bm"], w["w_bf"], w["w_out"], w["ln_mix_g"],
              w["ln_mix_b"]]
    return pl.pallas_call(
        _merge_kernel, grid=(nb, seq // tm),
        in_specs=[row_blk(D_MODEL), row_blk(V_WIDTH), row_blk(2 * D_MODEL)]
        + [_resident(a.shape) for a in consts],
        out_specs=row_blk(D_MODEL), out_shape=jax.ShapeDtypeStruct((nb, seq, D_MODEL), F32),
        compiler_params=_params(), name="merge",
    )(x3, o, g, *consts)


def _ffn_kernel(h_ref, halo0_ref, wup_ref, cw_ref, cb_ref, wdown_ref, lng_ref, lnb_ref,
                out_ref, halo_ref, gate_ref, act_ref):
    @pl.when(pl.program_id(1) == 0)
    def _():
        halo_ref[...] = halo0_ref[...]

    n_rows = h_ref.shape[1]
    sub = min(n_rows, SUB_ROWS)

    def gated(r0):
        rows = slice(r0, r0 + sub)
        hb = h_ref[0, rows, :].astype(BF16)
        for c in range(N_FF_CHUNKS):
            sl = slice(c * FF_CHUNK, (c + 1) * FF_CHUNK)
            gate = _dot(hb, wup_ref[:, sl])
            val = _dot(hb, wup_ref[:, D_FF + c * FF_CHUNK:D_FF + (c + 1) * FF_CHUNK])
            gate_ref[c, :HALO_ROWS, :] = halo_ref[:, sl]
            gate_ref[c, HALO_ROWS:, :] = gate
            halo_ref[:, sl] = gate[sub - HALO_ROWS:, :]
            back1 = gate_ref[c, HALO_ROWS - 1:HALO_ROWS - 1 + sub, :]
            back2 = gate_ref[c, HALO_ROWS - 2:HALO_ROWS - 2 + sub, :]
            conv = (cw_ref[0:1, sl] * back2 + cw_ref[1:2, sl] * back1 + cw_ref[2:3, sl] * gate
                    + cb_ref[:, sl])
            act_ref[rows, sl] = (conv * (1.0 / (1.0 + jnp.exp(-conv))) * val).astype(BF16)

    starts = list(range(0, n_rows, sub))
    for r0 in starts:
        gated(r0)
    down = [_dot(act_ref[r0:r0 + sub, :], wdown_ref[...]) for r0 in starts]
    for r0, ffn in zip(starts, down):
        rows = slice(r0, r0 + sub)
        out_ref[0, rows, :] = _layer_norm(DN_ALPHA * h_ref[0, rows, :] + ffn, lng_ref[...],
                                          lnb_ref[...])


def _ffn_call(h1, halo0, w, tm):
    nb, seq, _ = h1.shape
    row_blk = pl.BlockSpec((1, tm, D_MODEL), lambda b, j: (b, j, 0))
    consts = [halo0, w["w_up"], w["conv_w"], w["conv_b"], w["w_down"], w["ln_ffn_g"],
              w["ln_ffn_b"]]
    return pl.pallas_call(
        _ffn_kernel, grid=(nb, seq // tm),
        in_specs=[row_blk] + [_resident(a.shape) for a in consts],
        out_specs=row_blk, out_shape=jax.ShapeDtypeStruct((nb, seq, D_MODEL), F32),
        scratch_shapes=[pltpu.VMEM((HALO_ROWS, D_FF), F32),
                        pltpu.VMEM((N_FF_CHUNKS, HALO_ROWS + min(tm, SUB_ROWS), FF_CHUNK), F32),
                        pltpu.VMEM((tm, D_FF), BF16)],
        compiler_params=_params(), name="ffn",
    )(h1, *consts)


def _gate_tail_kernel(h_ref, wup_ref, tail_ref):
    gate = _dot(h_ref[0].astype(BF16), wup_ref[...])
    tail_ref[...] = gate[gate.shape[0] - HALO_ROWS:, :]


def _gate_tail_call(h1, w_up):
    return pl.pallas_call(
        _gate_tail_kernel, grid=(1,),
        in_specs=[pl.BlockSpec(h1.shape, lambda i: (0, 0, 0)),
                  pl.BlockSpec((D_MODEL, D_FF), lambda i: (0, 0))],
        out_specs=pl.BlockSpec((HALO_ROWS, D_FF), lambda i: (0, 0)),
        out_shape=jax.ShapeDtypeStruct((HALO_ROWS, D_FF), F32),
        compiler_params=pltpu.CompilerParams(dimension_semantics=("arbitrary",),
                                             vmem_limit_bytes=VMEM_LIMIT),
        name="gate_tail",
    )(h1, w_up)


def _rope_tables(length):
    half = MLA_ROPE_DIM // 2
    inv_freq = ROPE_THETA ** (-jnp.arange(half, dtype=F32) / half)
    ang = jnp.arange(length).astype(F32)[:, None] * inv_freq[None, :]
    cos, sin = jnp.cos(ang), jnp.sin(ang)
    zero = jnp.zeros_like(cos)
    widen = lambda a, b: jnp.pad(jnp.tile(jnp.concatenate([a, b], axis=1), (1, 2)),
                                 ((0, 0), (0, LANES - 2 * MLA_ROPE_DIM)))
    return widen(cos, cos), widen(zero, sin), widen(-sin, zero)


def _bias_selectors():
    half = (N_PAIRS // 2) * LANES
    sel = np.zeros((LANES, 2 * half), np.float32)
    const = np.zeros((1, 2 * half), np.float32)
    for h in range(FOX_HEADS):
        base = LANES * (h // 2) + EXTRA_GROUP * (h % 2)
        for piece in range(3):
            sel[piece * FOX_HEADS + h, base + piece] = 1.0
            const[0, base + 3 + piece] = 1.0
            const[0, half + base + piece] = 1.0
            sel[piece * FOX_HEADS + h, half + base + 3 + piece] = -1.0
    return jnp.asarray(sel, BF16), jnp.asarray(const)


def _w_in_kernel(w_ref, o_ref):
    o_krope = MLA_Q_RANK + MLA_KV_RANK
    o_fq = o_krope + MLA_ROPE_DIM
    o_flogit = o_fq + 3 * FOX_WIDTH
    o_gate = o_flogit + FOX_HEADS
    cols = w_ref.shape[1]

    def put(dst, rows):
        o_ref[:, dst:dst + rows.shape[0]] = rows.T.astype(BF16)

    step = 512
    for src0, dst0, n in ((0, C_QLAT, o_krope), (o_fq, C_FQ, 3 * FOX_WIDTH),
                          (o_gate, C_GATE, 2 * D_MODEL)):
        for r in range(0, n, step):
            put(dst0 + r, w_ref[src0 + r:src0 + r + step, :])
    k_rope = w_ref[o_krope:o_fq, :]
    put(C_KROPE, jnp.concatenate(
        [k_rope, k_rope, jnp.zeros((LANES - 2 * MLA_ROPE_DIM, cols), F32)], axis=0))
    put(C_FLOGIT, jnp.concatenate(
        [w_ref[o_flogit:o_gate, :], jnp.zeros((LANES - FOX_HEADS, cols), F32)], axis=0))


def _regroup_w_in(w_in_t):
    cols = 256
    return pl.pallas_call(
        _w_in_kernel, grid=(D_MODEL // cols,),
        in_specs=[pl.BlockSpec((w_in_t.shape[0], cols), lambda i: (0, i))],
        out_specs=pl.BlockSpec((cols, C_TOTAL), lambda i: (i, 0)),
        out_shape=jax.ShapeDtypeStruct((D_MODEL, C_TOTAL), BF16),
        compiler_params=pltpu.CompilerParams(dimension_semantics=("arbitrary",),
                                             vmem_limit_bytes=VMEM_LIMIT),
        name="w_in_regroup",
    )(w_in_t)


def _prepare_weights(w_in, b_gate, b_forget, q_norm_g, w_q_up, kv_norm_g, w_kv_up, w_branch_mla,
                     w_branch_fox, w_out, w_ffn_up, conv_w, conv_b, w_ffn_down):
    n_half = N_PAIRS // 2
    w_in_p = _regroup_w_in(w_in.T)
    wq = w_q_up.reshape(MLA_Q_RANK, n_half, 2, MLA_QK_DIM)
    wq_p = jnp.concatenate([
        wq[..., :MLA_NOPE_DIM].reshape(MLA_Q_RANK, n_half, LANES),
        wq[..., MLA_NOPE_DIM:].reshape(MLA_Q_RANK, n_half, 2 * MLA_ROPE_DIM),
        jnp.zeros((MLA_Q_RANK, n_half, LANES - 2 * MLA_ROPE_DIM), F32)], axis=-1)
    wkv = w_kv_up.reshape(MLA_KV_RANK, MLA_HEADS, MLA_NOPE_DIM + MLA_V_DIM)
    return {
        "w_in": w_in_p,
        "b_gate": b_gate.reshape(1, -1),
        "b_forget": jnp.pad(b_forget.reshape(1, -1), ((0, 0), (0, LANES - FOX_HEADS))),
        "q_norm_g": q_norm_g.reshape(1, -1),
        "w_q": wq_p.reshape(MLA_Q_RANK, n_half * PAIR_QK).astype(BF16),
        "kv_norm_g": kv_norm_g.reshape(1, -1),
        "w_k": wkv[..., :MLA_NOPE_DIM].reshape(MLA_KV_RANK, -1).astype(BF16),
        "w_v": wkv[..., MLA_NOPE_DIM:].reshape(MLA_KV_RANK, -1).astype(BF16),
        "w_bm": w_branch_mla.astype(BF16),
        "w_bf": w_branch_fox.astype(BF16),
        "w_out": w_out.astype(BF16),
        "w_up": w_ffn_up.astype(BF16),
        "conv_w": conv_w,
        "conv_b": conv_b.reshape(1, -1),
        "w_down": w_ffn_down.astype(BF16),
    }


def _layer(x3, tabs, cum0, halo0, prefix, w):
    seq = x3.shape[1]
    is_prefix = halo0 is None
    q, k, v, gates, cum_end = _proj_call(x3, tabs, cum0, w, min(PROJ_ROWS, seq))
    o = _attn_call(q, k, v, prefix, min(ATTN_KEYS, seq), min(ATTN_ROWS, seq),
                   N_PAIRS if is_prefix else ATTN_PAIRS)
    h1 = _merge_call(x3, o, gates, w, min(WIDE_ROWS, seq))
    if is_prefix:
        return None, (k[0], v[0]), cum_end[0], _gate_tail_call(h1, w["w_up"])
    return _ffn_call(h1, halo0, w, min(WIDE_ROWS, seq)), None, None, None


def kernel(x, meta_tokens, ln_emb_g, ln_emb_b, w_in, b_gate, b_forget, q_norm_g, w_q_up, kv_norm_g, w_kv_up, w_branch_mla, w_branch_fox, w_out, ln_mix_g, ln_mix_b, w_ffn_up, conv_w, conv_b, w_ffn_down, ln_ffn_g, ln_ffn_b):
    seq = x.shape[1]
    w = _prepare_weights(w_in[0], b_gate[0], b_forget[0], q_norm_g[0], w_q_up[0], kv_norm_g[0],
                         w_kv_up[0], w_branch_mla[0], w_branch_fox[0], w_out[0], w_ffn_up[0],
                         conv_w[0], conv_b[0], w_ffn_down[0])
    w.update(ln_emb_g=ln_emb_g.reshape(1, -1), ln_emb_b=ln_emb_b.reshape(1, -1),
             ln_mix_g=ln_mix_g[0].reshape(1, -1), ln_mix_b=ln_mix_b[0].reshape(1, -1),
             ln_ffn_g=ln_ffn_g[0].reshape(1, -1), ln_ffn_b=ln_ffn_b[0].reshape(1, -1))
    w["sel"], w["sel_const"] = _bias_selectors()
    tabs = _rope_tables(N_META + seq)
    meta_tabs = tuple(t[:N_META] for t in tabs)
    real_tabs = tuple(t[N_META:] for t in tabs)

    _, meta_kv, meta_cum, meta_tail = _layer(
        meta_tokens[None].astype(x.dtype), meta_tabs, jnp.zeros((HALO_ROWS, LANES), F32),
        None, None, w)
    out, _, _, _ = _layer(x, real_tabs, meta_cum, meta_tail, meta_kv, w)
    return out
```

```python
import functools

import numpy as np
import jax
import jax.numpy as jnp
from jax import lax
from jax.experimental import pallas as pl
from jax.experimental.pallas import tpu as pltpu

D_MODEL = 1024
N_META = 16
MLA_HEADS = 8
MLA_Q_RANK = 384
MLA_KV_RANK = 128
MLA_NOPE_DIM = 64
MLA_ROPE_DIM = 32
MLA_QK_DIM = MLA_NOPE_DIM + MLA_ROPE_DIM
MLA_V_DIM = 64
ROPE_THETA = 10000.0
FOX_HEADS = 8
FOX_HEAD_DIM = 64
FOX_WIDTH = FOX_HEADS * FOX_HEAD_DIM
D_FF = 2816
LN_EPS = 1e-5
RMS_EPS = 1e-6
DEPTH = 1
DN_ALPHA = (2 * DEPTH) ** 0.25
NEG_INF = -1e30
LOG2E = 1.4426950408889634

LANES = 128
N_PAIRS = 8
PAIR_QK = 2 * LANES
PAIR_V = 2 * MLA_V_DIM
QK_WIDTH = N_PAIRS * PAIR_QK
V_WIDTH = N_PAIRS * PAIR_V
VX_WIDTH = N_PAIRS * PAIR_QK
EXTRA_GROUP = 32
FF_CHUNK = 256
N_FF_CHUNKS = D_FF // FF_CHUNK
HALO_ROWS = 8
SUB_ROWS = 256
PROJ_ROWS = 512
WIDE_ROWS = 1024
ATTN_KEYS = 256
ATTN_ROWS = 128
ATTN_PAIRS = 2
VMEM_LIMIT = 56 * 1024 * 1024

C_QLAT = 0
C_KVLAT = C_QLAT + MLA_Q_RANK
C_KROPE = C_KVLAT + MLA_KV_RANK
C_FLOGIT = C_KROPE + LANES
C_FQ = C_FLOGIT + LANES
C_FK = C_FQ + FOX_WIDTH
C_FV = C_FK + FOX_WIDTH
C_GATE = C_FV + FOX_WIDTH
C_TOTAL = C_GATE + 2 * D_MODEL

F32 = jnp.float32
BF16 = jnp.bfloat16


def _dot(a, b):
    return jnp.dot(a, b, preferred_element_type=F32)


def _layer_norm(x, g, b):
    mu = jnp.mean(x, axis=-1, keepdims=True)
    xc = x - mu
    var = jnp.mean(xc * xc, axis=-1, keepdims=True)
    return xc * lax.rsqrt(var + LN_EPS) * g + b


def _rms_norm(x, g):
    return x * lax.rsqrt(jnp.mean(x * x, axis=-1, keepdims=True) + RMS_EPS) * g


def _resident(shape):
    zeros = (0,) * len(shape)
    return pl.BlockSpec(shape, lambda *_: zeros, pipeline_mode=pl.Buffered(1))


def _params():
    return pltpu.CompilerParams(dimension_semantics=("arbitrary", "arbitrary"),
                                vmem_limit_bytes=VMEM_LIMIT)


def _rope(t, cos_t, sa_t, sb_t):
    return (t * cos_t + pltpu.roll(t, MLA_ROPE_DIM // 2, 1) * sa_t
            + pltpu.roll(t, LANES - MLA_ROPE_DIM // 2, 1) * sb_t)


def _cumsum_rows(x):
    rows = x.shape[0]
    row = lax.broadcasted_iota(jnp.int32, x.shape, 0)
    step = 1
    while step < rows:
        x = x + jnp.where(row >= step, pltpu.roll(x, step, 0), 0.0)
        step *= 2
    return x


def _store_values(v_ref, first_pair, v):
    low_half = lax.broadcasted_iota(jnp.int32, (1, LANES), 1) < MLA_V_DIM
    one = jnp.ones((v.shape[0], LANES), v.dtype)
    for p in range(N_PAIRS // 2):
        pair = v[:, p * LANES:(p + 1) * LANES]
        col = (first_pair + p) * PAIR_QK
        v_ref[0, :, col:col + LANES] = jnp.where(low_half, pair, one)
        v_ref[0, :, col + LANES:col + PAIR_QK] = jnp.where(low_half, one, pair)


def _proj_kernel(x_ref, lng_ref, lnb_ref, win_ref, bgate_ref, bforget_ref, qng_ref, wq_ref,
                 kvng_ref, wk_ref, wv_ref, cos_ref, sa_ref, sb_ref, sel_ref, selc_ref, cum0_ref,
                 q_ref, k_ref, v_ref, g_ref, cumend_ref,
                 carry_ref):
    @pl.when(pl.program_id(1) == 0)
    def _():
        carry_ref[...] = cum0_ref[...]

    hb = _layer_norm(x_ref[0], lng_ref[...], lnb_ref[...]).astype(BF16)
    cos_t, sa_t, sb_t = cos_ref[...], sa_ref[...], sb_ref[...]
    mla_scale = MLA_QK_DIM ** -0.5 * LOG2E
    fox_scale = FOX_HEAD_DIM ** -0.5 * LOG2E
    n_half = N_PAIRS // 2

    lat = _dot(hb, win_ref[:, C_QLAT:C_KROPE])
    small = _dot(hb, win_ref[:, C_KROPE:C_FQ])
    gate_chunk = 512
    for c in range(2 * D_MODEL // gate_chunk):
        sl = slice(c * gate_chunk, (c + 1) * gate_chunk)
        zg = _dot(hb, win_ref[:, C_GATE + c * gate_chunk:C_GATE + (c + 1) * gate_chunk])
        zg = zg + bgate_ref[:, sl]
        g_ref[0, :, sl] = (1.0 / (1.0 + jnp.exp(-zg))).astype(BF16)

    qn = _rms_norm(lat[:, :MLA_Q_RANK], qng_ref[...]).astype(BF16)
    kvn = _rms_norm(lat[:, MLA_Q_RANK:], kvng_ref[...]).astype(BF16)
    for p in range(n_half):
        qp = _dot(qn, wq_ref[:, p * PAIR_QK:(p + 1) * PAIR_QK]) * mla_scale
        q_ref[0, :, p* PAIR_QK:p * PAIR_QK + LANES] = qp[:, :LANES].astype(BF16)
        q_ref[0, :, p* PAIR_QK + LANES:(p + 1) * PAIR_QK] = _rope(
            qp[:, LANES:], cos_t, sa_t, sb_t).astype(BF16)
    k_pe = _rope(small[:, :LANES], cos_t, sa_t, sb_t).astype(BF16)
    k_nope = _dot(kvn, wk_ref[...]).astype(BF16)
    for p in range(n_half):
        k_ref[0, :, p* PAIR_QK:p * PAIR_QK + LANES] = k_nope[:, p * LANES:(p + 1) * LANES]
        k_ref[0, :, p* PAIR_QK + LANES:(p + 1) * PAIR_QK] = k_pe
    _store_values(v_ref, 0, _dot(kvn, wv_ref[...]).astype(BF16))

    z = small[:, LANES:] + bforget_ref[...]
    log_f = -(jnp.maximum(-z, 0.0) + jnp.log1p(jnp.exp(-jnp.abs(z))))
    cum = _cumsum_rows(log_f) + carry_ref[0:1, :]
    rows = cum.shape[0]
    carry_ref[...] = jnp.broadcast_to(cum[rows - 1:rows, :], carry_ref.shape)
    cumend_ref[0] = jnp.broadcast_to(cum[rows - 1:rows, :], carry_ref.shape)
    head_lane = lax.broadcasted_iota(jnp.int32, (1, LANES), 1) < FOX_HEADS
    cum = jnp.where(head_lane, cum * LOG2E, 0.0)
    c_hi = cum.astype(BF16).astype(F32)
    r1 = cum - c_hi
    c_mid = r1.astype(BF16).astype(F32)
    c_lo = (r1 - c_mid).astype(BF16).astype(F32)
    pieces = (c_hi + pltpu.roll(c_mid, FOX_HEADS, 1) + pltpu.roll(c_lo, 2 * FOX_HEADS, 1))
    aug = (_dot(pieces.astype(BF16), sel_ref[...]) + selc_ref[...]).astype(BF16)
    base = n_half * PAIR_QK
    for p in range(n_half):
        lo_col = base + p * PAIR_QK + LANES
        q_ref[0, :, lo_col:lo_col + LANES] = aug[:, p * LANES:(p + 1) * LANES]
        k_ref[0, :, lo_col:lo_col + LANES] = aug[:, (n_half + p) * LANES:(n_half + p + 1) * LANES]

    fq = (_dot(hb, win_ref[:, C_FQ:C_FQ + FOX_WIDTH]) * fox_scale).astype(BF16)
    fk = _dot(hb, win_ref[:, C_FK:C_FK + FOX_WIDTH]).astype(BF16)
    for p in range(n_half):
        lo_col = base + p * PAIR_QK
        q_ref[0, :, lo_col:lo_col + LANES] = fq[:, p * LANES:(p + 1) * LANES]
        k_ref[0, :, lo_col:lo_col + LANES] = fk[:, p * LANES:(p + 1) * LANES]
    _store_values(v_ref, n_half, _dot(hb, win_ref[:, C_FV:C_FV + FOX_WIDTH]).astype(BF16))


def _proj_call(x3, tabs, cum0, w, tm):
    nb, seq, _ = x3.shape
    row_blk = lambda width: pl.BlockSpec((1, tm, width), lambda b, j: (b, j, 0))
    tab_blk = pl.BlockSpec((tm, LANES), lambda b, j: (j, 0))
    consts = [w["ln_emb_g"], w["ln_emb_b"], w["w_in"], w["b_gate"], w["b_forget"], w["q_norm_g"],
              w["w_q"], w["kv_norm_g"], w["w_k"], w["w_v"]]
    sels = [w["sel"], w["sel_const"], cum0]
    in_specs = ([row_blk(D_MODEL)] + [_resident(a.shape) for a in consts] + [tab_blk] * 3
                + [_resident(a.shape) for a in sels])
    out_shape = (jax.ShapeDtypeStruct((nb, seq, QK_WIDTH), BF16),
                 jax.ShapeDtypeStruct((nb, seq, QK_WIDTH), BF16),
                 jax.ShapeDtypeStruct((nb, seq, VX_WIDTH), BF16),
                 jax.ShapeDtypeStruct((nb, seq, 2 * D_MODEL), BF16),
                 jax.ShapeDtypeStruct((nb, HALO_ROWS, LANES), F32))
    out_specs = (row_blk(QK_WIDTH), row_blk(QK_WIDTH), row_blk(VX_WIDTH), row_blk(2 * D_MODEL),
                 pl.BlockSpec((1, HALO_ROWS, LANES), lambda b, j: (b, 0, 0)))
    return pl.pallas_call(
        _proj_kernel, grid=(nb, seq // tm), in_specs=in_specs, out_specs=out_specs,
        out_shape=out_shape, scratch_shapes=[pltpu.VMEM((HALO_ROWS, LANES), F32)],
        compiler_params=_params(), name="proj",
    )(x3, *consts, *tabs, *sels)


def _attn_kernel(*refs, tk, rb, n_prefix, pairs):
    for pp in range(pairs):
        qk = slice(pp * PAIR_QK, (pp + 1) * PAIR_QK)
        views = [r.at[:, :, qk] for r in refs[:3]]
        if n_prefix:
            views += [r.at[:, qk] for r in refs[3:5]]
        o_ref, m_ref, acc_ref = refs[-3:]
        views += [o_ref.at[:, :, pp * PAIR_V:(pp + 1) * PAIR_V], m_ref.at[2 * pp:2 * pp + 2],
                  acc_ref.at[2 * pp:2 * pp + 2]]
        _attn_pair(*views, tk=tk, rb=rb, n_prefix=n_prefix)


def _attn_pair(*refs, tk, rb, n_prefix):
    if n_prefix:
        q_ref, k_ref, v_ref, kp_ref, vp_ref, o_ref, m_ref, acc_ref = refs
    else:
        q_ref, k_ref, v_ref, o_ref, m_ref, acc_ref = refs
    q = q_ref[0]
    tq = q.shape[0]
    lane = lax.broadcasted_iota(jnp.int32, (1, PAIR_QK), 1)
    head_a = (lane < MLA_NOPE_DIM) | ((lane >= LANES) & (lane < LANES + EXTRA_GROUP))
    head_b = (((lane >= MLA_NOPE_DIM) & (lane < LANES))
              | ((lane >= LANES + EXTRA_GROUP) & (lane < LANES + 2 * EXTRA_GROUP)))
    zero = jnp.zeros_like(q)
    qs = (jnp.where(head_a, q, zero), jnp.where(head_b, q, zero))
    nt_dims = (((1,), (1,)), ((), ()))
    key_lane = lax.broadcasted_iota(jnp.int32, (1, LANES), 1)

    if n_prefix:
        grp_a = key_lane < n_prefix
        grp_b = (key_lane >= n_prefix) & (key_lane < 2 * n_prefix)
        s_all = lax.dot_general(q, kp_ref[...], nt_dims, preferred_element_type=F32)
        for r0 in range(0, tq, rb):
            rows = slice(r0, r0 + rb)
            s = s_all[rows, :]
            m_a = jnp.max(jnp.where(grp_a, s, NEG_INF), axis=1, keepdims=True)
            m_b = jnp.max(jnp.where(grp_b, s, NEG_INF), axis=1, keepdims=True)
            x = jnp.where(grp_a, s - m_a, jnp.where(grp_b, s - m_b, NEG_INF))
            p = jnp.exp2(x.astype(BF16))
            pv = _dot(p, vp_ref[...])
            for h, m_h in ((0, m_a), (1, m_b)):
                m_ref[h, rows, :] = jnp.broadcast_to(m_h, (rb, LANES))
                acc_ref[h, rows, :] = pv[:, h * LANES:(h + 1) * LANES]
    else:
        m_ref[...] = jnp.full(m_ref.shape, NEG_INF, F32)
        acc_ref[...] = jnp.zeros(acc_ref.shape, F32)

    def scores(h, key0):
        row0 = (key0 // rb) * rb
        return lax.dot_general(qs[h][row0:, :], k_ref[0, key0:key0 + tk, :], nt_dims,
                               preferred_element_type=F32)

    def fold(h, r0, key0, s):
        width = s.shape[1]
        step = min(width, LANES)
        tiles = width // step
        rows = slice(r0, r0 + rb)
        m_cur = s[:, :step]
        for t in range(1, tiles):
            m_cur = jnp.maximum(m_cur, s[:, t * step:(t + 1) * step])
        m_prev = m_ref[h, rows, :]
        m_new = jnp.maximum(m_prev, jnp.max(m_cur, axis=1, keepdims=True))
        alpha = jnp.exp2(m_prev - m_new)
        m_wide = m_new[:, :step] if tiles == 1 else jnp.tile(m_new, (1, tiles))
        p = jnp.exp2((s - m_wide).astype(BF16))
        pv = _dot(p, v_ref[0, key0:key0 + width, h * LANES:(h + 1) * LANES])
        acc_ref[h, rows, :] = alpha * acc_ref[h, rows, :] + pv
        m_ref[h, rows, :] = m_new

    def update(keys, s_tiles):
        first = keys[0]
        for r0 in range((first // rb) * rb, tq, rb):
            if r0 + rb <= first:
                continue
            seen = [i for i, key0 in enumerate(keys) if r0 + rb > key0]
            width = len(seen) * tk
            for h in range(2):
                parts = []
                for i in seen:
                    row0 = (keys[i] // rb) * rb
                    parts.append(s_tiles[i][h][r0 - row0:r0 - row0 + rb, :])
                s = parts[0] if len(parts) == 1 else jnp.concatenate(parts, axis=1)
                if r0 < first + width - 1:
                    row = lax.broadcasted_iota(jnp.int32, (rb, width), 0) + r0
                    col = lax.broadcasted_iota(jnp.int32, (rb, width), 1) + first
                    s = jnp.where(col <= row, s, NEG_INF)
                fold(h, r0, first, s)

    key_tiles = list(range(0, tq, tk))
    per_group = 4 if len(key_tiles) % 4 == 0 else 1
    groups = [key_tiles[i:i + per_group] for i in range(0, len(key_tiles), per_group)]
    group_scores = lambda keys: [[scores(h, key0) for h in range(2)] for key0 in keys]
    ahead = 3
    pending = [group_scores(keys) for keys in groups[:ahead]]
    for c, keys in enumerate(groups):
        s_cur = pending.pop(0)
        if c + ahead < len(groups):
            pending.append(group_scores(groups[c + ahead]))
        update(keys, s_cur)

    acc_a, acc_b = acc_ref[0], acc_ref[1]
    o = jnp.where(key_lane < MLA_V_DIM, acc_a / pltpu.roll(acc_a, MLA_V_DIM, 1),
                  acc_b / pltpu.roll(acc_b, MLA_V_DIM, 1))
    o_ref[0] = o.astype(BF16)


def _pair_prefix(k_meta, v_meta):
    n = k_meta.shape[0]
    lane = jnp.arange(QK_WIDTH) % PAIR_QK
    k_a = (lane < MLA_NOPE_DIM) | ((lane >= LANES) & (lane < LANES + EXTRA_GROUP))
    k_b = (((lane >= MLA_NOPE_DIM) & (lane < LANES))
           | ((lane >= LANES + EXTRA_GROUP) & (lane < LANES + 2 * EXTRA_GROUP)))
    v_a = lane < LANES
    zero_k, zero_v = jnp.zeros_like(k_meta), jnp.zeros_like(v_meta)
    pad = ((0, LANES - 2 * n), (0, 0))
    kp = jnp.concatenate([jnp.where(k_a, k_meta, zero_k), jnp.where(k_b, k_meta, zero_k)])
    vp = jnp.concatenate([jnp.where(v_a, v_meta, zero_v), jnp.where(v_a, zero_v, v_meta)])
    return jnp.pad(kp, pad), jnp.pad(vp, pad)


def _attn_call(q, k, v, prefix, tk, rb, pairs):
    nb, seq, _ = q.shape
    n_prefix = 0 if prefix is None else N_META
    wide = pl.BlockSpec((1, seq, pairs * PAIR_QK), lambda b, p: (b, 0, p))
    in_specs = [wide, wide, wide]
    args = [q, k, v]
    if n_prefix:
        in_specs += [pl.BlockSpec((LANES, pairs * PAIR_QK), lambda b, p: (0, p))] * 2
        args += list(_pair_prefix(*prefix))
    return pl.pallas_call(
        functools.partial(_attn_kernel, tk=tk, rb=rb, n_prefix=n_prefix, pairs=pairs),
        grid=(nb, N_PAIRS // pairs), in_specs=in_specs,
        out_specs=pl.BlockSpec((1, seq, pairs * PAIR_V), lambda b, p: (b, 0, p)),
        out_shape=jax.ShapeDtypeStruct((nb, seq, V_WIDTH), BF16),
        scratch_shapes=[pltpu.VMEM((2 * pairs, seq, LANES), F32),
                        pltpu.VMEM((2 * pairs, seq, PAIR_V), F32)],
        compiler_params=_params(),
        name="attn",
    )(*args)


def _merge_kernel(x_ref, o_ref, g_ref, lng_ref, lnb_ref, wbm_ref, wbf_ref, wout_ref, mixg_ref,
                  mixb_ref, h_ref):
    half = V_WIDTH // 2
    n_rows = x_ref.shape[1]
    sub = min(n_rows, SUB_ROWS)
    parts = [slice(r, r + sub) for r in range(0, n_rows, sub)]

    def branches(rows):
        return (_dot(o_ref[0, rows, :half], wbm_ref[...]),
                _dot(o_ref[0, rows, half:], wbf_ref[...]))

    def mix(rows, o_mla, o_fox):
        merged = (g_ref[0, rows, :D_MODEL].astype(F32) * o_mla
                  + g_ref[0, rows, D_MODEL:].astype(F32) * o_fox)
        return _dot(merged.astype(BF16), wout_ref[...])

    def finish(rows, mixed):
        h0 = _layer_norm(x_ref[0, rows, :], lng_ref[...], lnb_ref[...])
        h_ref[0, rows, :] = _layer_norm(DN_ALPHA * h0 + mixed, mixg_ref[...], mixb_ref[...])

    projected = [branches(rows) for rows in parts]
    mixed = [mix(rows, *pr) for rows, pr in zip(parts, projected)]
    for rows, mx in zip(parts, mixed):
        finish(rows, mx)


def _merge_call(x3, o, g, w, tm):
    nb, seq, _ = x3.shape
    row_blk = lambda width: pl.BlockSpec((1, tm, width), lambda b, j: (b, j, 0))
    consts = [w["ln_emb_g"], w["ln_emb_b"], w["w_bm"], w["w_bf"], w["w_out"], w["ln_mix_g"],
              w["ln_mix_b"]]
    return pl.pallas_call(
        _merge_kernel, grid=(nb, seq // tm),
        in_specs=[row_blk(D_MODEL), row_blk(V_WIDTH), row_blk(2 * D_MODEL)]
        + [_resident(a.shape) for a in consts],
        out_specs=row_blk(D_MODEL), out_shape=jax.ShapeDtypeStruct((nb, seq, D_MODEL), F32),
        compiler_params=_params(), name="merge",
    )(x3, o, g, *consts)


def _ffn_kernel(h_ref, halo0_ref, wup_ref, cw_ref, cb_ref, wdown_ref, lng_ref, lnb_ref,
                out_ref, halo_ref, gate_ref, act_ref):
    @pl.when(pl.program_id(1) == 0)
    def _():
        halo_ref[...] = halo0_ref[...]

    n_rows = h_ref.shape[1]
    sub = min(n_rows, SUB_ROWS)

    def gated(r0):
        rows = slice(r0, r0 + sub)
        hb = h_ref[0, rows, :].astype(BF16)
        for c in range(N_FF_CHUNKS):
            sl = slice(c * FF_CHUNK, (c + 1) * FF_CHUNK)
            gate = _dot(hb, wup_ref[:, sl])
            val = _dot(hb, wup_ref[:, D_FF + c * FF_CHUNK:D_FF + (c + 1) * FF_CHUNK])
            gate_ref[c, :HALO_ROWS, :] = halo_ref[:, sl]
            gate_ref[c, HALO_ROWS:, :] = gate
            halo_ref[:, sl] = gate[sub - HALO_ROWS:, :]
            back1 = gate_ref[c, HALO_ROWS - 1:HALO_ROWS - 1 + sub, :]
            back2 = gate_ref[c, HALO_ROWS - 2:HALO_ROWS - 2 + sub, :]
            conv = (cw_ref[0:1, sl] * back2 + cw_ref[1:2, sl] * back1 + cw_ref[2:3, sl] * gate
                    + cb_ref[:, sl])
            act_ref[rows, sl] = (conv * (1.0 / (1.0 + jnp.exp(-conv))) * val).astype(BF16)

    starts = list(range(0, n_rows, sub))
    for r0 in starts:
        gated(r0)
    down = [_dot(act_ref[r0:r0 + sub, :], wdown_ref[...]) for r0 in starts]
    for r0, ffn in zip(starts, down):
        rows = slice(r0, r0 + sub)
        out_ref[0, rows, :] = _layer_norm(DN_ALPHA * h_ref[0, rows, :] + ffn, lng_ref[...],
                                          lnb_ref[...])


def _ffn_call(h1, halo0, w, tm):
    nb, seq, _ = h1.shape
    row_blk = pl.BlockSpec((1, tm, D_MODEL), lambda b, j: (b, j, 0))
    consts = [halo0, w["w_up"], w["conv_w"], w["conv_b"], w["w_down"], w["ln_ffn_g"],
              w["ln_ffn_b"]]
    return pl.pallas_call(
        _ffn_kernel, grid=(nb, seq // tm),
        in_specs=[row_blk] + [_resident(a.shape) for a in consts],
        out_specs=row_blk, out_shape=jax.ShapeDtypeStruct((nb, seq, D_MODEL), F32),
        scratch_shapes=[pltpu.VMEM((HALO_ROWS, D_FF), F32),
                        pltpu.VMEM((N_FF_CHUNKS, HALO_ROWS + min(tm, SUB_ROWS), FF_CHUNK), F32),
                        pltpu.VMEM((tm, D_FF), BF16)],
        compiler_params=_params(), name="ffn",
    )(h1, *consts)


def _gate_tail_kernel(h_ref, wup_ref, tail_ref):
    gate = _dot(h_ref[0].astype(BF16), wup_ref[...])
    tail_ref[...] = gate[gate.shape[0] - HALO_ROWS:, :]


def _gate_tail_call(h1, w_up):
    return pl.pallas_call(
        _gate_tail_kernel, grid=(1,),
        in_specs=[pl.BlockSpec(h1.shape, lambda i: (0, 0, 0)),
                  pl.BlockSpec((D_MODEL, D_FF), lambda i: (0, 0))],
        out_specs=pl.BlockSpec((HALO_ROWS, D_FF), lambda i: (0, 0)),
        out_shape=jax.ShapeDtypeStruct((HALO_ROWS, D_FF), F32),
        compiler_params=pltpu.CompilerParams(dimension_semantics=("arbitrary",),
                                             vmem_limit_bytes=VMEM_LIMIT),
        name="gate_tail",
    )(h1, w_up)


def _rope_tables(length):
    half = MLA_ROPE_DIM // 2
    inv_freq = ROPE_THETA ** (-jnp.arange(half, dtype=F32) / half)
    ang = jnp.arange(length).astype(F32)[:, None] * inv_freq[None, :]
    cos, sin = jnp.cos(ang), jnp.sin(ang)
    zero = jnp.zeros_like(cos)
    widen = lambda a, b: jnp.pad(jnp.tile(jnp.concatenate([a, b], axis=1), (1, 2)),
                                 ((0, 0), (0, LANES - 2 * MLA_ROPE_DIM)))
    return widen(cos, cos), widen(zero, sin), widen(-sin, zero)


def _bias_selectors():
    half = (N_PAIRS // 2) * LANES
    sel = np.zeros((LANES, 2 * half), np.float32)
    const = np.zeros((1, 2 * half), np.float32)
    for h in range(FOX_HEADS):
        base = LANES * (h // 2) + EXTRA_GROUP * (h % 2)
        for piece in range(3):
            sel[piece * FOX_HEADS + h, base + piece] = 1.0
            const[0, base + 3 + piece] = 1.0
            const[0, half + base + piece] = 1.0
            sel[piece * FOX_HEADS + h, half + base + 3 + piece] = -1.0
    return jnp.asarray(sel, BF16), jnp.asarray(const)


def _w_in_kernel(w_ref, o_ref):
    o_krope = MLA_Q_RANK + MLA_KV_RANK
    o_fq = o_krope + MLA_ROPE_DIM
    o_flogit = o_fq + 3 * FOX_WIDTH
    o_gate = o_flogit + FOX_HEADS
    cols = w_ref.shape[1]

    def put(dst, rows):
        o_ref[:, dst:dst + rows.shape[0]] = rows.T.astype(BF16)

    step = 512
    for src0, dst0, n in ((0, C_QLAT, o_krope), (o_fq, C_FQ, 3 * FOX_WIDTH),
                          (o_gate, C_GATE, 2 * D_MODEL)):
        for r in range(0, n, step):
            put(dst0 + r, w_ref[src0 + r:src0 + r + step, :])
    k_rope = w_ref[o_krope:o_fq, :]
    put(C_KROPE, jnp.concatenate(
        [k_rope, k_rope, jnp.zeros((LANES - 2 * MLA_ROPE_DIM, cols), F32)], axis=0))
    put(C_FLOGIT, jnp.concatenate(
        [w_ref[o_flogit:o_gate, :], jnp.zeros((LANES - FOX_HEADS, cols), F32)], axis=0))


def _regroup_w_in(w_in_t):
    cols = 256
    return pl.pallas_call(
        _w_in_kernel, grid=(D_MODEL // cols,),
        in_specs=[pl.BlockSpec((w_in_t.shape[0], cols), lambda i: (0, i))],
        out_specs=pl.BlockSpec((cols, C_TOTAL), lambda i: (i, 0)),
        out_shape=jax.ShapeDtypeStruct((D_MODEL, C_TOTAL), BF16),
        compiler_params=pltpu.CompilerParams(dimension_semantics=("arbitrary",),
                                             vmem_limit_bytes=VMEM_LIMIT),
        name="w_in_regroup",
    )(w_in_t)


def _prepare_weights(w_in, b_gate, b_forget, q_norm_g, w_q_up, kv_norm_g, w_kv_up, w_branch_mla,
                     w_branch_fox, w_out, w_ffn_up, conv_w, conv_b, w_ffn_down):
    n_half = N_PAIRS // 2
    w_in_p = _regroup_w_in(w_in.T)
    wq = w_q_up.reshape(MLA_Q_RANK, n_half, 2, MLA_QK_DIM)
    wq_p = jnp.concatenate([
        wq[..., :MLA_NOPE_DIM].reshape(MLA_Q_RANK, n_half, LANES),
        wq[..., MLA_NOPE_DIM:].reshape(MLA_Q_RANK, n_half, 2 * MLA_ROPE_DIM),
        jnp.zeros((MLA_Q_RANK, n_half, LANES - 2 * MLA_ROPE_DIM), F32)], axis=-1)
    wkv = w_kv_up.reshape(MLA_KV_RANK, MLA_HEADS, MLA_NOPE_DIM + MLA_V_DIM)
    return {
        "w_in": w_in_p,
        "b_gate": b_gate.reshape(1, -1),
        "b_forget": jnp.pad(b_forget.reshape(1, -1), ((0, 0), (0, LANES - FOX_HEADS))),
        "q_norm_g": q_norm_g.reshape(1, -1),
        "w_q": wq_p.reshape(MLA_Q_RANK, n_half * PAIR_QK).astype(BF16),
        "kv_norm_g": kv_norm_g.reshape(1, -1),
        "w_k": wkv[..., :MLA_NOPE_DIM].reshape(MLA_KV_RANK, -1).astype(BF16),
        "w_v": wkv[..., MLA_NOPE_DIM:].reshape(MLA_KV_RANK, -1).astype(BF16),
        "w_bm": w_branch_mla.astype(BF16),
        "w_bf": w_branch_fox.astype(BF16),
        "w_out": w_out.astype(BF16),
        "w_up": w_ffn_up.astype(BF16),
        "conv_w": conv_w,
        "conv_b": conv_b.reshape(1, -1),
        "w_down": w_ffn_down.astype(BF16),
    }


def _layer(x3, tabs, cum0, halo0, prefix, w):
    seq = x3.shape[1]
    is_prefix = halo0 is None
    q, k, v, gates, cum_end = _proj_call(x3, tabs, cum0, w, min(PROJ_ROWS, seq))
    o = _attn_call(q, k, v, prefix, min(ATTN_KEYS, seq), min(ATTN_ROWS, seq),
                   N_PAIRS if is_prefix else ATTN_PAIRS)
    h1 = _merge_call(x3, o, gates, w, min(WIDE_ROWS, seq))
    if is_prefix:
        return None, (k[0], v[0]), cum_end[0], _gate_tail_call(h1, w["w_up"])
    return _ffn_call(h1, halo0, w, min(WIDE_ROWS, seq)), None, None, None


def kernel(x, meta_tokens, ln_emb_g, ln_emb_b, w_in, b_gate, b_forget, q_norm_g, w_q_up, kv_norm_g, w_kv_up, w_branch_mla, w_branch_fox, w_out, ln_mix_g, ln_mix_b, w_ffn_up, conv_w, conv_b, w_ffn_down, ln_ffn_g, ln_ffn_b):
    seq = x.shape[1]
    w = _prepare_weights(w_in[0], b_gate[0], b_forget[0], q_norm_g[0], w_q_up[0], kv_norm_g[0],
                         w_kv_up[0], w_branch_mla[0], w_branch_fox[0], w_out[0], w_ffn_up[0],
                         conv_w[0], conv_b[0], w_ffn_down[0])
    w.update(ln_emb_g=ln_emb_g.reshape(1, -1), ln_emb_b=ln_emb_b.reshape(1, -1),
             ln_mix_g=ln_mix_g[0].reshape(1, -1), ln_mix_b=ln_mix_b[0].reshape(1, -1),
             ln_ffn_g=ln_ffn_g[0].reshape(1, -1), ln_ffn_b=ln_ffn_b[0].reshape(1, -1))
    w["sel"], w["sel_const"] = _bias_selectors()
    tabs = _rope_tables(N_META + seq)
    meta_tabs = tuple(t[:N_META] for t in tabs)
    real_tabs = tuple(t[N_META:] for t in tabs)

    _, meta_kv, meta_cum, meta_tail = _layer(
        meta_tokens[None].astype(x.dtype), meta_tabs, jnp.zeros((HALO_ROWS, LANES), F32),
        None, None, w)
    out, _, _, _ = _layer(x, real_tabs, meta_cum, meta_tail, meta_kv, w)
    return out
```

```python
import functools

import numpy as np
import jax
import jax.numpy as jnp
from jax import lax
from jax.experimental import pallas as pl
from jax.experimental.pallas import tpu as pltpu

D_MODEL = 1024
N_META = 16
MLA_HEADS = 8
MLA_Q_RANK = 384
MLA_KV_RANK = 128
MLA_NOPE_DIM = 64
MLA_ROPE_DIM = 32
MLA_QK_DIM = MLA_NOPE_DIM + MLA_ROPE_DIM
MLA_V_DIM = 64
ROPE_THETA = 10000.0
FOX_HEADS = 8
FOX_HEAD_DIM = 64
FOX_WIDTH = FOX_HEADS * FOX_HEAD_DIM
D_FF = 2816
LN_EPS = 1e-5
RMS_EPS = 1e-6
DEPTH = 1
DN_ALPHA = (2 * DEPTH) ** 0.25
NEG_INF = -1e30
LOG2E = 1.4426950408889634

LANES = 128
N_PAIRS = 8
PAIR_QK = 2 * LANES
PAIR_V = 2 * MLA_V_DIM
QK_WIDTH = N_PAIRS * PAIR_QK
V_WIDTH = N_PAIRS * PAIR_V
VX_WIDTH = N_PAIRS * PAIR_QK
EXTRA_GROUP = 32
FF_CHUNK = 256
N_FF_CHUNKS = D_FF // FF_CHUNK
HALO_ROWS = 8
SUB_ROWS = 256
PROJ_ROWS = 512
WIDE_ROWS = 1024
ATTN_KEYS = 256
ATTN_ROWS = 128
ATTN_PAIRS = 2
VMEM_LIMIT = 56 * 1024 * 1024

C_QLAT = 0
C_KVLAT = C_QLAT + MLA_Q_RANK
C_KROPE = C_KVLAT + MLA_KV_RANK
C_FLOGIT = C_KROPE + LANES
C_FQ = C_FLOGIT + LANES
C_FK = C_FQ + FOX_WIDTH
C_FV = C_FK + FOX_WIDTH
C_GATE = C_FV + FOX_WIDTH
C_TOTAL = C_GATE + 2 * D_MODEL

F32 = jnp.float32
BF16 = jnp.bfloat16


def _dot(a, b):
    return jnp.dot(a, b, preferred_element_type=F32)


def _layer_norm(x, g, b):
    mu = jnp.mean(x, axis=-1, keepdims=True)
    xc = x - mu
    var = jnp.mean(xc * xc, axis=-1, keepdims=True)
    return xc * lax.rsqrt(var + LN_EPS) * g + b


def _rms_norm(x, g):
    return x * lax.rsqrt(jnp.mean(x * x, axis=-1, keepdims=True) + RMS_EPS) * g


def _resident(shape):
    zeros = (0,) * len(shape)
    return pl.BlockSpec(shape, lambda *_: zeros, pipeline_mode=pl.Buffered(1))


def _params():
    return pltpu.CompilerParams(dimension_semantics=("arbitrary", "arbitrary"),
                                vmem_limit_bytes=VMEM_LIMIT)


def _rope(t, cos_t, sa_t, sb_t):
    return (t * cos_t + pltpu.roll(t, MLA_ROPE_DIM // 2, 1) * sa_t
            + pltpu.roll(t, LANES - MLA_ROPE_DIM // 2, 1) * sb_t)


def _cumsum_rows(x):
    rows = x.shape[0]
    row = lax.broadcasted_iota(jnp.int32, x.shape, 0)
    step = 1
    while step < rows:
        x = x + jnp.where(row >= step, pltpu.roll(x, step, 0), 0.0)
        step *= 2
    return x


def _store_values(v_ref, first_pair, v):
    low_half = lax.broadcasted_iota(jnp.int32, (1, LANES), 1) < MLA_V_DIM
    one = jnp.ones((v.shape[0], LANES), v.dtype)
    for p in range(N_PAIRS // 2):
        pair = v[:, p * LANES:(p + 1) * LANES]
        col = (first_pair + p) * PAIR_QK
        v_ref[0, :, col:col + LANES] = jnp.where(low_half, pair, one)
        v_ref[0, :, col + LANES:col + PAIR_QK] = jnp.where(low_half, one, pair)


def _proj_kernel(x_ref, lng_ref, lnb_ref, win_ref, bgate_ref, bforget_ref, qng_ref, wq_ref,
                 kvng_ref, wk_ref, wv_ref, cos_ref, sa_ref, sb_ref, sel_ref, selc_ref, cum0_ref,
                 q_ref, k_ref, v_ref, g_ref, cumend_ref,
                 carry_ref):
    @pl.when(pl.program_id(1) == 0)
    def _():
        carry_ref[...] = cum0_ref[...]

    hb = _layer_norm(x_ref[0], lng_ref[...], lnb_ref[...]).astype(BF16)
    cos_t, sa_t, sb_t = cos_ref[...], sa_ref[...], sb_ref[...]
    mla_scale = MLA_QK_DIM ** -0.5 * LOG2E
    fox_scale = FOX_HEAD_DIM ** -0.5 * LOG2E
    n_half = N_PAIRS // 2

    lat = _dot(hb, win_ref[:, C_QLAT:C_KROPE])
    small = _dot(hb, win_ref[:, C_KROPE:C_FQ])
    gate_chunk = 512
    for c in range(2 * D_MODEL // gate_chunk):
        sl = slice(c * gate_chunk, (c + 1) * gate_chunk)
        zg = _dot(hb, win_ref[:, C_GATE + c * gate_chunk:C_GATE + (c + 1) * gate_chunk])
        zg = zg + bgate_ref[:, sl]
        g_ref[0, :, sl] = (1.0 / (1.0 + jnp.exp(-zg))).astype(BF16)

    qn = _rms_norm(lat[:, :MLA_Q_RANK], qng_ref[...]).astype(BF16)
    kvn = _rms_norm(lat[:, MLA_Q_RANK:], kvng_ref[...]).astype(BF16)
    for p in range(n_half):
        qp = _dot(qn, wq_ref[:, p * PAIR_QK:(p + 1) * PAIR_QK]) * mla_scale
        q_ref[0, :, p* PAIR_QK:p * PAIR_QK + LANES] = qp[:, :LANES].astype(BF16)
        q_ref[0, :, p* PAIR_QK + LANES:(p + 1) * PAIR_QK] = _rope(
            qp[:, LANES:], cos_t, sa_t, sb_t).astype(BF16)
    k_pe = _rope(small[:, :LANES], cos_t, sa_t, sb_t).astype(BF16)
    k_nope = _dot(kvn, wk_ref[...]).astype(BF16)
    for p in range(n_half):
        k_ref[0, :, p* PAIR_QK:p * PAIR_QK + LANES] = k_nope[:, p * LANES:(p + 1) * LANES]
        k_ref[0, :, p* PAIR_QK + LANES:(p + 1) * PAIR_QK] = k_pe
    _store_values(v_ref, 0, _dot(kvn, wv_ref[...]).astype(BF16))

    z = small[:, LANES:] + bforget_ref[...]
    log_f = -(jnp.maximum(-z, 0.0) + jnp.log1p(jnp.exp(-jnp.abs(z))))
    cum = _cumsum_rows(log_f) + carry_ref[0:1, :]
    rows = cum.shape[0]
    carry_ref[...] = jnp.broadcast_to(cum[rows - 1:rows, :], carry_ref.shape)
    cumend_ref[0] = jnp.broadcast_to(cum[rows - 1:rows, :], carry_ref.shape)
    head_lane = lax.broadcasted_iota(jnp.int32, (1, LANES), 1) < FOX_HEADS
    cum = jnp.where(head_lane, cum * LOG2E, 0.0)
    c_hi = cum.astype(BF16).astype(F32)
    r1 = cum - c_hi
    c_mid = r1.astype(BF16).astype(F32)
    c_lo = (r1 - c_mid).astype(BF16).astype(F32)
    pieces = (c_hi + pltpu.roll(c_mid, FOX_HEADS, 1) + pltpu.roll(c_lo, 2 * FOX_HEADS, 1))
    aug = (_dot(pieces.astype(BF16), sel_ref[...]) + selc_ref[...]).astype(BF16)
    base = n_half * PAIR_QK
    for p in range(n_half):
        lo_col = base + p * PAIR_QK + LANES
        q_ref[0, :, lo_col:lo_col + LANES] = aug[:, p * LANES:(p + 1) * LANES]
        k_ref[0, :, lo_col:lo_col + LANES] = aug[:, (n_half + p) * LANES:(n_half + p + 1) * LANES]

    fq = (_dot(hb, win_ref[:, C_FQ:C_FQ + FOX_WIDTH]) * fox_scale).astype(BF16)
    fk = _dot(hb, win_ref[:, C_FK:C_FK + FOX_WIDTH]).astype(BF16)
    for p in range(n_half):
        lo_col = base + p * PAIR_QK
        q_ref[0, :, lo_col:lo_col + LANES] = fq[:, p * LANES:(p + 1) * LANES]
        k_ref[0, :, lo_col:lo_col + LANES] = fk[:, p * LANES:(p + 1) * LANES]
    _store_values(v_ref, n_half, _dot(hb, win_ref[:, C_FV:C_FV + FOX_WIDTH]).astype(BF16))


def _proj_call(x3, tabs, cum0, w, tm):
    nb, seq, _ = x3.shape
    row_blk = lambda width: pl.BlockSpec((1, tm, width), lambda b, j: (b, j, 0))
    tab_blk = pl.BlockSpec((tm, LANES), lambda b, j: (j, 0))
    consts = [w["ln_emb_g"], w["ln_emb_b"], w["w_in"], w["b_gate"], w["b_forget"], w["q_norm_g"],
              w["w_q"], w["kv_norm_g"], w["w_k"], w["w_v"]]
    sels = [w["sel"], w["sel_const"], cum0]
    in_specs = ([row_blk(D_MODEL)] + [_resident(a.shape) for a in consts] + [tab_blk] * 3
                + [_resident(a.shape) for a in sels])
    out_shape = (jax.ShapeDtypeStruct((nb, seq, QK_WIDTH), BF16),
                 jax.ShapeDtypeStruct((nb, seq, QK_WIDTH), BF16),
                 jax.ShapeDtypeStruct((nb, seq, VX_WIDTH), BF16),
                 jax.ShapeDtypeStruct((nb, seq, 2 * D_MODEL), BF16),
                 jax.ShapeDtypeStruct((nb, HALO_ROWS, LANES), F32))
    out_specs = (row_blk(QK_WIDTH), row_blk(QK_WIDTH), row_blk(VX_WIDTH), row_blk(2 * D_MODEL),
                 pl.BlockSpec((1, HALO_ROWS, LANES), lambda b, j: (b, 0, 0)))
    return pl.pallas_call(
        _proj_kernel, grid=(nb, seq // tm), in_specs=in_specs, out_specs=out_specs,
        out_shape=out_shape, scratch_shapes=[pltpu.VMEM((HALO_ROWS, LANES), F32)],
        compiler_params=_params(), name="proj",
    )(x3, *consts, *tabs, *sels)


def _attn_kernel(*refs, tk, rb, n_prefix, pairs):
    for pp in range(pairs):
        qk = slice(pp * PAIR_QK, (pp + 1) * PAIR_QK)
        views = [r.at[:, :, qk] for r in refs[:3]]
        if n_prefix:
            views += [r.at[:, qk] for r in refs[3:5]]
        o_ref, m_ref, acc_ref = refs[-3:]
        views += [o_ref.at[:, :, pp * PAIR_V:(pp + 1) * PAIR_V], m_ref.at[2 * pp:2 * pp + 2],
                  acc_ref.at[2 * pp:2 * pp + 2]]
        _attn_pair(*views, tk=tk, rb=rb, n_prefix=n_prefix)


def _attn_pair(*refs, tk, rb, n_prefix):
    if n_prefix:
        q_ref, k_ref, v_ref, kp_ref, vp_ref, o_ref, m_ref, acc_ref = refs
    else:
        q_ref, k_ref, v_ref, o_ref, m_ref, acc_ref = refs
    q = q_ref[0]
    tq = q.shape[0]
    lane = lax.broadcasted_iota(jnp.int32, (1, PAIR_QK), 1)
    head_a = (lane < MLA_NOPE_DIM) | ((lane >= LANES) & (lane < LANES + EXTRA_GROUP))
    head_b = (((lane >= MLA_NOPE_DIM) & (lane < LANES))
              | ((lane >= LANES + EXTRA_GROUP) & (lane < LANES + 2 * EXTRA_GROUP)))
    zero = jnp.zeros_like(q)
    qs = (jnp.where(head_a, q, zero), jnp.where(head_b, q, zero))
    nt_dims = (((1,), (1,)), ((), ()))
    key_lane = lax.broadcasted_iota(jnp.int32, (1, LANES), 1)

    if n_prefix:
        grp_a = key_lane < n_prefix
        grp_b = (key_lane >= n_prefix) & (key_lane < 2 * n_prefix)
        s_all = lax.dot_general(q, kp_ref[...], nt_dims, preferred_element_type=F32)
        for r0 in range(0, tq, rb):
            rows = slice(r0, r0 + rb)
            s = s_all[rows, :]
            m_a = jnp.max(jnp.where(grp_a, s, NEG_INF), axis=1, keepdims=True)
            m_b = jnp.max(jnp.where(grp_b, s, NEG_INF), axis=1, keepdims=True)
            x = jnp.where(grp_a, s - m_a, jnp.where(grp_b, s - m_b, NEG_INF))
            p = jnp.exp2(x.astype(BF16))
            pv = _dot(p, vp_ref[...])
            for h, m_h in ((0, m_a), (1, m_b)):
                m_ref[h, rows, :] = jnp.broadcast_to(m_h, (rb, LANES))
                acc_ref[h, rows, :] = pv[:, h * LANES:(h + 1) * LANES]
    else:
        m_ref[...] = jnp.full(m_ref.shape, NEG_INF, F32)
        acc_ref[...] = jnp.zeros(acc_ref.shape, F32)

    def scores(h, key0):
        row0 = (key0 // rb) * rb
        return lax.dot_general(qs[h][row0:, :], k_ref[0, key0:key0 + tk, :], nt_dims,
                               preferred_element_type=F32)

    def fold(h, r0, key0, s):
        width = s.shape[1]
        step = min(width, LANES)
        tiles = width // step
        rows = slice(r0, r0 + rb)
        m_cur = s[:, :step]
        for t in range(1, tiles):
            m_cur = jnp.maximum(m_cur, s[:, t * step:(t + 1) * step])
        m_prev = m_ref[h, rows, :]
        m_new = jnp.maximum(m_prev, jnp.max(m_cur, axis=1, keepdims=True))
        alpha = jnp.exp2(m_prev - m_new)
        m_wide = m_new[:, :step] if tiles == 1 else jnp.tile(m_new, (1, tiles))
        p = jnp.exp2((s - m_wide).astype(BF16))
        pv = _dot(p, v_ref[0, key0:key0 + width, h * LANES:(h + 1) * LANES])
        acc_ref[h, rows, :] = alpha * acc_ref[h, rows, :] + pv
        m_ref[h, rows, :] = m_new

    def update(keys, s_tiles):
        first = keys[0]
        for r0 in range((first // rb) * rb, tq, rb):
            if r0 + rb <= first:
                continue
            seen = [i for i, key0 in enumerate(keys) if r0 + rb > key0]
            width = len(seen) * tk
            for h in range(2):
                parts = []
                for i in seen:
                    row0 = (keys[i] // rb) * rb
                    parts.append(s_tiles[i][h][r0 - row0:r0 - row0 + rb, :])
                s = parts[0] if len(parts) == 1 else jnp.concatenate(parts, axis=1)
                if r0 < first + width - 1:
                    row = lax.broadcasted_iota(jnp.int32, (rb, width), 0) + r0
                    col = lax.broadcasted_iota(jnp.int32, (rb, width), 1) + first
                    s = jnp.where(col <= row, s, NEG_INF)
                fold(h, r0, first, s)

    key_tiles = list(range(0, tq, tk))
    per_group = 8 if len(key_tiles) % 8 == 0 else 1
    groups = [key_tiles[i:i + per_group] for i in range(0, len(key_tiles), per_group)]
    group_scores = lambda keys: [[scores(h, key0) for h in range(2)] for key0 in keys]
    ahead = 3
    pending = [group_scores(keys) for keys in groups[:ahead]]
    for c, keys in enumerate(groups):
        s_cur = pending.pop(0)
        if c + ahead < len(groups):
            pending.append(group_scores(groups[c + ahead]))
        update(keys, s_cur)

    acc_a, acc_b = acc_ref[0], acc_ref[1]
    o = jnp.where(key_lane < MLA_V_DIM, acc_a / pltpu.roll(acc_a, MLA_V_DIM, 1),
                  acc_b / pltpu.roll(acc_b, MLA_V_DIM, 1))
    o_ref[0] = o.astype(BF16)


def _pair_prefix(k_meta, v_meta):
    n = k_meta.shape[0]
    lane = jnp.arange(QK_WIDTH) % PAIR_QK
    k_a = (lane < MLA_NOPE_DIM) | ((lane >= LANES) & (lane < LANES + EXTRA_GROUP))
    k_b = (((lane >= MLA_NOPE_DIM) & (lane < LANES))
           | ((lane >= LANES + EXTRA_GROUP) & (lane < LANES + 2 * EXTRA_GROUP)))
    v_a = lane < LANES
    zero_k, zero_v = jnp.zeros_like(k_meta), jnp.zeros_like(v_meta)
    pad = ((0, LANES - 2 * n), (0, 0))
    kp = jnp.concatenate([jnp.where(k_a, k_meta, zero_k), jnp.where(k_b, k_meta, zero_k)])
    vp = jnp.concatenate([jnp.where(v_a, v_meta, zero_v), jnp.where(v_a, zero_v, v_meta)])
    return jnp.pad(kp, pad), jnp.pad(vp, pad)


def _attn_call(q, k, v, prefix, tk, rb, pairs):
    nb, seq, _ = q.shape
    n_prefix = 0 if prefix is None else N_META
    wide = pl.BlockSpec((1, seq, pairs * PAIR_QK), lambda b, p: (b, 0, p))
    in_specs = [wide, wide, wide]
    args = [q, k, v]
    if n_prefix:
        in_specs += [pl.BlockSpec((LANES, pairs * PAIR_QK), lambda b, p: (0, p))] * 2
        args += list(_pair_prefix(*prefix))
    return pl.pallas_call(
        functools.partial(_attn_kernel, tk=tk, rb=rb, n_prefix=n_prefix, pairs=pairs),
        grid=(nb, N_PAIRS // pairs), in_specs=in_specs,
        out_specs=pl.BlockSpec((1, seq, pairs * PAIR_V), lambda b, p: (b, 0, p)),
        out_shape=jax.ShapeDtypeStruct((nb, seq, V_WIDTH), BF16),
        scratch_shapes=[pltpu.VMEM((2 * pairs, seq, LANES), F32),
                        pltpu.VMEM((2 * pairs, seq, PAIR_V), F32)],
        compiler_params=_params(),
        name="attn",
    )(*args)


def _merge_kernel(x_ref, o_ref, g_ref, lng_ref, lnb_ref, wbm_ref, wbf_ref, wout_ref, mixg_ref,
                  mixb_ref, h_ref):
    half = V_WIDTH // 2
    n_rows = x_ref.shape[1]
    sub = min(n_rows, SUB_ROWS)
    parts = [slice(r, r + sub) for r in range(0, n_rows, sub)]

    def branches(rows):
        return (_dot(o_ref[0, rows, :half], wbm_ref[...]),
                _dot(o_ref[0, rows, half:], wbf_ref[...]))

    def mix(rows, o_mla, o_fox):
        merged = (g_ref[0, rows, :D_MODEL].astype(F32) * o_mla
                  + g_ref[0, rows, D_MODEL:].astype(F32) * o_fox)
        return _dot(merged.astype(BF16), wout_ref[...])

    def finish(rows, mixed):
        h0 = _layer_norm(x_ref[0, rows, :], lng_ref[...], lnb_ref[...])
        h_ref[0, rows, :] = _layer_norm(DN_ALPHA * h0 + mixed, mixg_ref[...], mixb_ref[...])

    projected = [branches(rows) for rows in parts]
    mixed = [mix(rows, *pr) for rows, pr in zip(parts, projected)]
    for rows, mx in zip(parts, mixed):
        finish(rows, mx)


def _merge_call(x3, o, g, w, tm):
    nb, seq, _ = x3.shape
    row_blk = lambda width: pl.BlockSpec((1, tm, width), lambda b, j: (b, j, 0))
    consts = [w["ln_emb_g"], w["ln_emb_b"], w["w_bm"], w["w_bf"], w["w_out"], w["ln_mix_g"],
              w["ln_mix_b"]]
    return pl.pallas_call(
        _merge_kernel, grid=(nb, seq // tm),
        in_specs=[row_blk(D_MODEL), row_blk(V_WIDTH), row_blk(2 * D_MODEL)]
        + [_resident(a.shape) for a in consts],
        out_specs=row_blk(D_MODEL), out_shape=jax.ShapeDtypeStruct((nb, seq, D_MODEL), F32),
        compiler_params=_params(), name="merge",
    )(x3, o, g, *consts)


def _ffn_kernel(h_ref, halo0_ref, wup_ref, cw_ref, cb_ref, wdown_ref, lng_ref, lnb_ref,
                out_ref, halo_ref, gate_ref, act_ref):
    @pl.when(pl.program_id(1) == 0)
    def _():
        halo_ref[...] = halo0_ref[...]

    n_rows = h_ref.shape[1]
    sub = min(n_rows, SUB_ROWS)

    def gated(r0):
        rows = slice(r0, r0 + sub)
        hb = h_ref[0, rows, :].astype(BF16)
        for c in range(N_FF_CHUNKS):
            sl = slice(c * FF_CHUNK, (c + 1) * FF_CHUNK)
            gate = _dot(hb, wup_ref[:, sl])
            val = _dot(hb, wup_ref[:, D_FF + c * FF_CHUNK:D_FF + (c + 1) * FF_CHUNK])
            gate_ref[c, :HALO_ROWS, :] = halo_ref[:, sl]
            gate_ref[c, HALO_ROWS:, :] = gate
            halo_ref[:, sl] = gate[sub - HALO_ROWS:, :]
            back1 = gate_ref[c, HALO_ROWS - 1:HALO_ROWS - 1 + sub, :]
            back2 = gate_ref[c, HALO_ROWS - 2:HALO_ROWS - 2 + sub, :]
            conv = (cw_ref[0:1, sl] * back2 + cw_ref[1:2, sl] * back1 + cw_ref[2:3, sl] * gate
                    + cb_ref[:, sl])
            act_ref[rows, sl] = (conv * (1.0 / (1.0 + jnp.exp(-conv))) * val).astype(BF16)

    starts = list(range(0, n_rows, sub))
    for r0 in starts:
        gated(r0)
    down = [_dot(act_ref[r0:r0 + sub, :], wdown_ref[...]) for r0 in starts]
    for r0, ffn in zip(starts, down):
        rows = slice(r0, r0 + sub)
        out_ref[0, rows, :] = _layer_norm(DN_ALPHA * h_ref[0, rows, :] + ffn, lng_ref[...],
                                          lnb_ref[...])


def _ffn_call(h1, halo0, w, tm):
    nb, seq, _ = h1.shape
    row_blk = pl.BlockSpec((1, tm, D_MODEL), lambda b, j: (b, j, 0))
    consts = [halo0, w["w_up"], w["conv_w"], w["conv_b"], w["w_down"], w["ln_ffn_g"],
              w["ln_ffn_b"]]
    return pl.pallas_call(
        _ffn_kernel, grid=(nb, seq // tm),
        in_specs=[row_blk] + [_resident(a.shape) for a in consts],
        out_specs=row_blk, out_shape=jax.ShapeDtypeStruct((nb, seq, D_MODEL), F32),
        scratch_shapes=[pltpu.VMEM((HALO_ROWS, D_FF), F32),
                        pltpu.VMEM((N_FF_CHUNKS, HALO_ROWS + min(tm, SUB_ROWS), FF_CHUNK), F32),
                        pltpu.VMEM((tm, D_FF), BF16)],
        compiler_params=_params(), name="ffn",
    )(h1, *consts)


def _gate_tail_kernel(h_ref, wup_ref, tail_ref):
    gate = _dot(h_ref[0].astype(BF16), wup_ref[...])
    tail_ref[...] = gate[gate.shape[0] - HALO_ROWS:, :]


def _gate_tail_call(h1, w_up):
    return pl.pallas_call(
        _gate_tail_kernel, grid=(1,),
        in_specs=[pl.BlockSpec(h1.shape, lambda i: (0, 0, 0)),
                  pl.BlockSpec((D_MODEL, D_FF), lambda i: (0, 0))],
        out_specs=pl.BlockSpec((HALO_ROWS, D_FF), lambda i: (0, 0)),
        out_shape=jax.ShapeDtypeStruct((HALO_ROWS, D_FF), F32),
        compiler_params=pltpu.CompilerParams(dimension_semantics=("arbitrary",),
                                             vmem_limit_bytes=VMEM_LIMIT),
        name="gate_tail",
    )(h1, w_up)


def _rope_tables(length):
    half = MLA_ROPE_DIM // 2
    inv_freq = ROPE_THETA ** (-jnp.arange(half, dtype=F32) / half)
    ang = jnp.arange(length).astype(F32)[:, None] * inv_freq[None, :]
    cos, sin = jnp.cos(ang), jnp.sin(ang)
    zero = jnp.zeros_like(cos)
    widen = lambda a, b: jnp.pad(jnp.tile(jnp.concatenate([a, b], axis=1), (1, 2)),
                                 ((0, 0), (0, LANES - 2 * MLA_ROPE_DIM)))
    return widen(cos, cos), widen(zero, sin), widen(-sin, zero)


def _bias_selectors():
    half = (N_PAIRS // 2) * LANES
    sel = np.zeros((LANES, 2 * half), np.float32)
    const = np.zeros((1, 2 * half), np.float32)
    for h in range(FOX_HEADS):
        base = LANES * (h // 2) + EXTRA_GROUP * (h % 2)
        for piece in range(3):
            sel[piece * FOX_HEADS + h, base + piece] = 1.0
            const[0, base + 3 + piece] = 1.0
            const[0, half + base + piece] = 1.0
            sel[piece * FOX_HEADS + h, half + base + 3 + piece] = -1.0
    return jnp.asarray(sel, BF16), jnp.asarray(const)


def _w_in_kernel(w_ref, o_ref):
    o_krope = MLA_Q_RANK + MLA_KV_RANK
    o_fq = o_krope + MLA_ROPE_DIM
    o_flogit = o_fq + 3 * FOX_WIDTH
    o_gate = o_flogit + FOX_HEADS
    cols = w_ref.shape[1]

    def put(dst, rows):
        o_ref[:, dst:dst + rows.shape[0]] = rows.T.astype(BF16)

    step = 512
    for src0, dst0, n in ((0, C_QLAT, o_krope), (o_fq, C_FQ, 3 * FOX_WIDTH),
                          (o_gate, C_GATE, 2 * D_MODEL)):
        for r in range(0, n, step):
            put(dst0 + r, w_ref[src0 + r:src0 + r + step, :])
    k_rope = w_ref[o_krope:o_fq, :]
    put(C_KROPE, jnp.concatenate(
        [k_rope, k_rope, jnp.zeros((LANES - 2 * MLA_ROPE_DIM, cols), F32)], axis=0))
    put(C_FLOGIT, jnp.concatenate(
        [w_ref[o_flogit:o_gate, :], jnp.zeros((LANES - FOX_HEADS, cols), F32)], axis=0))


def _regroup_w_in(w_in_t):
    cols = 256
    return pl.pallas_call(
        _w_in_kernel, grid=(D_MODEL // cols,),
        in_specs=[pl.BlockSpec((w_in_t.shape[0], cols), lambda i: (0, i))],
        out_specs=pl.BlockSpec((cols, C_TOTAL), lambda i: (i, 0)),
        out_shape=jax.ShapeDtypeStruct((D_MODEL, C_TOTAL), BF16),
        compiler_params=pltpu.CompilerParams(dimension_semantics=("arbitrary",),
                                             vmem_limit_bytes=VMEM_LIMIT),
        name="w_in_regroup",
    )(w_in_t)


def _prepare_weights(w_in, b_gate, b_forget, q_norm_g, w_q_up, kv_norm_g, w_kv_up, w_branch_mla,
                     w_branch_fox, w_out, w_ffn_up, conv_w, conv_b, w_ffn_down):
    n_half = N_PAIRS // 2
    w_in_p = _regroup_w_in(w_in.T)
    wq = w_q_up.reshape(MLA_Q_RANK, n_half, 2, MLA_QK_DIM)
    wq_p = jnp.concatenate([
        wq[..., :MLA_NOPE_DIM].reshape(MLA_Q_RANK, n_half, LANES),
        wq[..., MLA_NOPE_DIM:].reshape(MLA_Q_RANK, n_half, 2 * MLA_ROPE_DIM),
        jnp.zeros((MLA_Q_RANK, n_half, LANES - 2 * MLA_ROPE_DIM), F32)], axis=-1)
    wkv = w_kv_up.reshape(MLA_KV_RANK, MLA_HEADS, MLA_NOPE_DIM + MLA_V_DIM)
    return {
        "w_in": w_in_p,
        "b_gate": b_gate.reshape(1, -1),
        "b_forget": jnp.pad(b_forget.reshape(1, -1), ((0, 0), (0, LANES - FOX_HEADS))),
        "q_norm_g": q_norm_g.reshape(1, -1),
        "w_q": wq_p.reshape(MLA_Q_RANK, n_half * PAIR_QK).astype(BF16),
        "kv_norm_g": kv_norm_g.reshape(1, -1),
        "w_k": wkv[..., :MLA_NOPE_DIM].reshape(MLA_KV_RANK, -1).astype(BF16),
        "w_v": wkv[..., MLA_NOPE_DIM:].reshape(MLA_KV_RANK, -1).astype(BF16),
        "w_bm": w_branch_mla.astype(BF16),
        "w_bf": w_branch_fox.astype(BF16),
        "w_out": w_out.astype(BF16),
        "w_up": w_ffn_up.astype(BF16),
        "conv_w": conv_w,
        "conv_b": conv_b.reshape(1, -1),
        "w_down": w_ffn_down.astype(BF16),
    }


def _layer(x3, tabs, cum0, halo0, prefix, w):
    seq = x3.shape[1]
    is_prefix = halo0 is None
    q, k, v, gates, cum_end = _proj_call(x3, tabs, cum0, w, min(PROJ_ROWS, seq))
    o = _attn_call(q, k, v, prefix, min(ATTN_KEYS, seq), min(ATTN_ROWS, seq),
                   N_PAIRS if is_prefix else ATTN_PAIRS)
    h1 = _merge_call(x3, o, gates, w, min(WIDE_ROWS, seq))
    if is_prefix:
        return None, (k[0], v[0]), cum_end[0], _gate_tail_call(h1, w["w_up"])
    return _ffn_call(h1, halo0, w, min(WIDE_ROWS, seq)), None, None, None


def kernel(x, meta_tokens, ln_emb_g, ln_emb_b, w_in, b_gate, b_forget, q_norm_g, w_q_up, kv_norm_g, w_kv_up, w_branch_mla, w_branch_fox, w_out, ln_mix_g, ln_mix_b, w_ffn_up, conv_w, conv_b, w_ffn_down, ln_ffn_g, ln_ffn_b):
    seq = x.shape[1]
    w = _prepare_weights(w_in[0], b_gate[0], b_forget[0], q_norm_g[0], w_q_up[0], kv_norm_g[0],
                         w_kv_up[0], w_branch_mla[0], w_branch_fox[0], w_out[0], w_ffn_up[0],
                         conv_w[0], conv_b[0], w_ffn_down[0])
    w.update(ln_emb_g=ln_emb_g.reshape(1, -1), ln_emb_b=ln_emb_b.reshape(1, -1),
             ln_mix_g=ln_mix_g[0].reshape(1, -1), ln_mix_b=ln_mix_b[0].reshape(1, -1),
             ln_ffn_g=ln_ffn_g[0].reshape(1, -1), ln_ffn_b=ln_ffn_b[0].reshape(1, -1))
    w["sel"], w["sel_const"] = _bias_selectors()
    tabs = _rope_tables(N_META + seq)
    meta_tabs = tuple(t[:N_META] for t in tabs)
    real_tabs = tuple(t[N_META:] for t in tabs)

    _, meta_kv, meta_cum, meta_tail = _layer(
        meta_tokens[None].astype(x.dtype), meta_tabs, jnp.zeros((HALO_ROWS, LANES), F32),
        None, None, w)
    out, _, _, _ = _layer(x, real_tabs, meta_cum, meta_tail, meta_kv, w)
    return out
```

```python
import functools

import numpy as np
import jax
import jax.numpy as jnp
from jax import lax
from jax.experimental import pallas as pl
from jax.experimental.pallas import tpu as pltpu

D_MODEL = 1024
N_META = 16
MLA_HEADS = 8
MLA_Q_RANK = 384
MLA_KV_RANK = 128
MLA_NOPE_DIM = 64
MLA_ROPE_DIM = 32
MLA_QK_DIM = MLA_NOPE_DIM + MLA_ROPE_DIM
MLA_V_DIM = 64
ROPE_THETA = 10000.0
FOX_HEADS = 8
FOX_HEAD_DIM = 64
FOX_WIDTH = FOX_HEADS * FOX_HEAD_DIM
D_FF = 2816
LN_EPS = 1e-5
RMS_EPS = 1e-6
DEPTH = 1
DN_ALPHA = (2 * DEPTH) ** 0.25
NEG_INF = -1e30
LOG2E = 1.4426950408889634

LANES = 128
N_PAIRS = 8
PAIR_QK = 2 * LANES
PAIR_V = 2 * MLA_V_DIM
QK_WIDTH = N_PAIRS * PAIR_QK
V_WIDTH = N_PAIRS * PAIR_V
VX_WIDTH = N_PAIRS * PAIR_QK
EXTRA_GROUP = 32
FF_CHUNK = 256
N_FF_CHUNKS = D_FF // FF_CHUNK
HALO_ROWS = 8
SUB_ROWS = 256
PROJ_ROWS = 512
WIDE_ROWS = 1024
ATTN_KEYS = 256
ATTN_ROWS = 256
ATTN_PAIRS = 2
VMEM_LIMIT = 56 * 1024 * 1024

C_QLAT = 0
C_KVLAT = C_QLAT + MLA_Q_RANK
C_KROPE = C_KVLAT + MLA_KV_RANK
C_FLOGIT = C_KROPE + LANES
C_FQ = C_FLOGIT + LANES
C_FK = C_FQ + FOX_WIDTH
C_FV = C_FK + FOX_WIDTH
C_GATE = C_FV + FOX_WIDTH
C_TOTAL = C_GATE + 2 * D_MODEL

F32 = jnp.float32
BF16 = jnp.bfloat16


def _dot(a, b):
    return jnp.dot(a, b, preferred_element_type=F32)


def _layer_norm(x, g, b):
    mu = jnp.mean(x, axis=-1, keepdims=True)
    xc = x - mu
    var = jnp.mean(xc * xc, axis=-1, keepdims=True)
    return xc * lax.rsqrt(var + LN_EPS) * g + b


def _rms_norm(x, g):
    return x * lax.rsqrt(jnp.mean(x * x, axis=-1, keepdims=True) + RMS_EPS) * g


def _resident(shape):
    zeros = (0,) * len(shape)
    return pl.BlockSpec(shape, lambda *_: zeros, pipeline_mode=pl.Buffered(1))


def _params():
    return pltpu.CompilerParams(dimension_semantics=("arbitrary", "arbitrary"),
                                vmem_limit_bytes=VMEM_LIMIT)


def _rope(t, cos_t, sa_t, sb_t):
    return (t * cos_t + pltpu.roll(t, MLA_ROPE_DIM // 2, 1) * sa_t
            + pltpu.roll(t, LANES - MLA_ROPE_DIM // 2, 1) * sb_t)


def _cumsum_rows(x):
    rows = x.shape[0]
    row = lax.broadcasted_iota(jnp.int32, x.shape, 0)
    step = 1
    while step < rows:
        x = x + jnp.where(row >= step, pltpu.roll(x, step, 0), 0.0)
        step *= 2
    return x


def _store_values(v_ref, first_pair, v):
    low_half = lax.broadcasted_iota(jnp.int32, (1, LANES), 1) < MLA_V_DIM
    one = jnp.ones((v.shape[0], LANES), v.dtype)
    for p in range(N_PAIRS // 2):
        pair = v[:, p * LANES:(p + 1) * LANES]
        col = (first_pair + p) * PAIR_QK
        v_ref[0, :, col:col + LANES] = jnp.where(low_half, pair, one)
        v_ref[0, :, col + LANES:col + PAIR_QK] = jnp.where(low_half, one, pair)


def _proj_kernel(x_ref, lng_ref, lnb_ref, win_ref, bgate_ref, bforget_ref, qng_ref, wq_ref,
                 kvng_ref, wk_ref, wv_ref, cos_ref, sa_ref, sb_ref, sel_ref, selc_ref, cum0_ref,
                 q_ref, k_ref, v_ref, g_ref, cumend_ref,
                 carry_ref):
    @pl.when(pl.program_id(1) == 0)
    def _():
        carry_ref[...] = cum0_ref[...]

    hb = _layer_norm(x_ref[0], lng_ref[...], lnb_ref[...]).astype(BF16)
    cos_t, sa_t, sb_t = cos_ref[...], sa_ref[...], sb_ref[...]
    mla_scale = MLA_QK_DIM ** -0.5 * LOG2E
    fox_scale = FOX_HEAD_DIM ** -0.5 * LOG2E
    n_half = N_PAIRS // 2

    lat = _dot(hb, win_ref[:, C_QLAT:C_KROPE])
    small = _dot(hb, win_ref[:, C_KROPE:C_FQ])
    gate_chunk = 512
    for c in range(2 * D_MODEL // gate_chunk):
        sl = slice(c * gate_chunk, (c + 1) * gate_chunk)
        zg = _dot(hb, win_ref[:, C_GATE + c * gate_chunk:C_GATE + (c + 1) * gate_chunk])
        zg = zg + bgate_ref[:, sl]
        g_ref[0, :, sl] = (1.0 / (1.0 + jnp.exp(-zg))).astype(BF16)

    qn = _rms_norm(lat[:, :MLA_Q_RANK], qng_ref[...]).astype(BF16)
    kvn = _rms_norm(lat[:, MLA_Q_RANK:], kvng_ref[...]).astype(BF16)
    for p in range(n_half):
        qp = _dot(qn, wq_ref[:, p * PAIR_QK:(p + 1) * PAIR_QK]) * mla_scale
        q_ref[0, :, p* PAIR_QK:p * PAIR_QK + LANES] = qp[:, :LANES].astype(BF16)
        q_ref[0, :, p* PAIR_QK + LANES:(p + 1) * PAIR_QK] = _rope(
            qp[:, LANES:], cos_t, sa_t, sb_t).astype(BF16)
    k_pe = _rope(small[:, :LANES], cos_t, sa_t, sb_t).astype(BF16)
    k_nope = _dot(kvn, wk_ref[...]).astype(BF16)
    for p in range(n_half):
        k_ref[0, :, p* PAIR_QK:p * PAIR_QK + LANES] = k_nope[:, p * LANES:(p + 1) * LANES]
        k_ref[0, :, p* PAIR_QK + LANES:(p + 1) * PAIR_QK] = k_pe
    _store_values(v_ref, 0, _dot(kvn, wv_ref[...]).astype(BF16))

    z = small[:, LANES:] + bforget_ref[...]
    log_f = -(jnp.maximum(-z, 0.0) + jnp.log1p(jnp.exp(-jnp.abs(z))))
    cum = _cumsum_rows(log_f) + carry_ref[0:1, :]
    rows = cum.shape[0]
    carry_ref[...] = jnp.broadcast_to(cum[rows - 1:rows, :], carry_ref.shape)
    cumend_ref[0] = jnp.broadcast_to(cum[rows - 1:rows, :], carry_ref.shape)
    head_lane = lax.broadcasted_iota(jnp.int32, (1, LANES), 1) < FOX_HEADS
    cum = jnp.where(head_lane, cum * LOG2E, 0.0)
    c_hi = cum.astype(BF16).astype(F32)
    r1 = cum - c_hi
    c_mid = r1.astype(BF16).astype(F32)
    c_lo = (r1 - c_mid).astype(BF16).astype(F32)
    pieces = (c_hi + pltpu.roll(c_mid, FOX_HEADS, 1) + pltpu.roll(c_lo, 2 * FOX_HEADS, 1))
    aug = (_dot(pieces.astype(BF16), sel_ref[...]) + selc_ref[...]).astype(BF16)
    base = n_half * PAIR_QK
    for p in range(n_half):
        lo_col = base + p * PAIR_QK + LANES
        q_ref[0, :, lo_col:lo_col + LANES] = aug[:, p * LANES:(p + 1) * LANES]
        k_ref[0, :, lo_col:lo_col + LANES] = aug[:, (n_half + p) * LANES:(n_half + p + 1) * LANES]

    fq = (_dot(hb, win_ref[:, C_FQ:C_FQ + FOX_WIDTH]) * fox_scale).astype(BF16)
    fk = _dot(hb, win_ref[:, C_FK:C_FK + FOX_WIDTH]).astype(BF16)
    for p in range(n_half):
        lo_col = base + p * PAIR_QK
        q_ref[0, :, lo_col:lo_col + LANES] = fq[:, p * LANES:(p + 1) * LANES]
        k_ref[0, :, lo_col:lo_col + LANES] = fk[:, p * LANES:(p + 1) * LANES]
    _store_values(v_ref, n_half, _dot(hb, win_ref[:, C_FV:C_FV + FOX_WIDTH]).astype(BF16))


def _proj_call(x3, tabs, cum0, w, tm):
    nb, seq, _ = x3.shape
    row_blk = lambda width: pl.BlockSpec((1, tm, width), lambda b, j: (b, j, 0))
    tab_blk = pl.BlockSpec((tm, LANES), lambda b, j: (j, 0))
    consts = [w["ln_emb_g"], w["ln_emb_b"], w["w_in"], w["b_gate"], w["b_forget"], w["q_norm_g"],
              w["w_q"], w["kv_norm_g"], w["w_k"], w["w_v"]]
    sels = [w["sel"], w["sel_const"], cum0]
    in_specs = ([row_blk(D_MODEL)] + [_resident(a.shape) for a in consts] + [tab_blk] * 3
                + [_resident(a.shape) for a in sels])
    out_shape = (jax.ShapeDtypeStruct((nb, seq, QK_WIDTH), BF16),
                 jax.ShapeDtypeStruct((nb, seq, QK_WIDTH), BF16),
                 jax.ShapeDtypeStruct((nb, seq, VX_WIDTH), BF16),
                 jax.ShapeDtypeStruct((nb, seq, 2 * D_MODEL), BF16),
                 jax.ShapeDtypeStruct((nb, HALO_ROWS, LANES), F32))
    out_specs = (row_blk(QK_WIDTH), row_blk(QK_WIDTH), row_blk(VX_WIDTH), row_blk(2 * D_MODEL),
                 pl.BlockSpec((1, HALO_ROWS, LANES), lambda b, j: (b, 0, 0)))
    return pl.pallas_call(
        _proj_kernel, grid=(nb, seq // tm), in_specs=in_specs, out_specs=out_specs,
        out_shape=out_shape, scratch_shapes=[pltpu.VMEM((HALO_ROWS, LANES), F32)],
        compiler_params=_params(), name="proj",
    )(x3, *consts, *tabs, *sels)


def _attn_kernel(*refs, tk, rb, n_prefix, pairs):
    for pp in range(pairs):
        qk = slice(pp * PAIR_QK, (pp + 1) * PAIR_QK)
        views = [r.at[:, :, qk] for r in refs[:3]]
        if n_prefix:
            views += [r.at[:, qk] for r in refs[3:5]]
        o_ref, m_ref, acc_ref = refs[-3:]
        views += [o_ref.at[:, :, pp * PAIR_V:(pp + 1) * PAIR_V], m_ref.at[2 * pp:2 * pp + 2],
                  acc_ref.at[2 * pp:2 * pp + 2]]
        _attn_pair(*views, tk=tk, rb=rb, n_prefix=n_prefix)


def _attn_pair(*refs, tk, rb, n_prefix):
    if n_prefix:
        q_ref, k_ref, v_ref, kp_ref, vp_ref, o_ref, m_ref, acc_ref = refs
    else:
        q_ref, k_ref, v_ref, o_ref, m_ref, acc_ref = refs
    q = q_ref[0]
    tq = q.shape[0]
    lane = lax.broadcasted_iota(jnp.int32, (1, PAIR_QK), 1)
    head_a = (lane < MLA_NOPE_DIM) | ((lane >= LANES) & (lane < LANES + EXTRA_GROUP))
    head_b = (((lane >= MLA_NOPE_DIM) & (lane < LANES))
              | ((lane >= LANES + EXTRA_GROUP) & (lane < LANES + 2 * EXTRA_GROUP)))
    zero = jnp.zeros_like(q)
    qs = (jnp.where(head_a, q, zero), jnp.where(head_b, q, zero))
    nt_dims = (((1,), (1,)), ((), ()))
    key_lane = lax.broadcasted_iota(jnp.int32, (1, LANES), 1)

    if n_prefix:
        grp_a = key_lane < n_prefix
        grp_b = (key_lane >= n_prefix) & (key_lane < 2 * n_prefix)
        s_all = lax.dot_general(q, kp_ref[...], nt_dims, preferred_element_type=F32)
        for r0 in range(0, tq, rb):
            rows = slice(r0, r0 + rb)
            s = s_all[rows, :]
            m_a = jnp.max(jnp.where(grp_a, s, NEG_INF), axis=1, keepdims=True)
            m_b = jnp.max(jnp.where(grp_b, s, NEG_INF), axis=1, keepdims=True)
            x = jnp.where(grp_a, s - m_a, jnp.where(grp_b, s - m_b, NEG_INF))
            p = jnp.exp2(x.astype(BF16))
            pv = _dot(p, vp_ref[...])
            for h, m_h in ((0, m_a), (1, m_b)):
                m_ref[h, rows, :] = jnp.broadcast_to(m_h, (rb, LANES))
                acc_ref[h, rows, :] = pv[:, h * LANES:(h + 1) * LANES]
    else:
        m_ref[...] = jnp.full(m_ref.shape, NEG_INF, F32)
        acc_ref[...] = jnp.zeros(acc_ref.shape, F32)

    def scores(h, key0):
        row0 = (key0 // rb) * rb
        return lax.dot_general(qs[h][row0:, :], k_ref[0, key0:key0 + tk, :], nt_dims,
                               preferred_element_type=F32)

    def fold(h, r0, key0, s):
        width = s.shape[1]
        step = min(width, LANES)
        tiles = width // step
        rows = slice(r0, r0 + rb)
        m_cur = s[:, :step]
        for t in range(1, tiles):
            m_cur = jnp.maximum(m_cur, s[:, t * step:(t + 1) * step])
        m_prev = m_ref[h, rows, :]
        m_new = jnp.maximum(m_prev, jnp.max(m_cur, axis=1, keepdims=True))
        alpha = jnp.exp2(m_prev - m_new)
        m_wide = m_new[:, :step] if tiles == 1 else jnp.tile(m_new, (1, tiles))
        p = jnp.exp2((s - m_wide).astype(BF16))
        pv = _dot(p, v_ref[0, key0:key0 + width, h * LANES:(h + 1) * LANES])
        acc_ref[h, rows, :] = alpha * acc_ref[h, rows, :] + pv
        m_ref[h, rows, :] = m_new

    def update(keys, s_tiles):
        first = keys[0]
        for r0 in range((first // rb) * rb, tq, rb):
            if r0 + rb <= first:
                continue
            seen = [i for i, key0 in enumerate(keys) if r0 + rb > key0]
            width = len(seen) * tk
            for h in range(2):
                parts = []
                for i in seen:
                    row0 = (keys[i] // rb) * rb
                    parts.append(s_tiles[i][h][r0 - row0:r0 - row0 + rb, :])
                s = parts[0] if len(parts) == 1 else jnp.concatenate(parts, axis=1)
                if r0 < first + width - 1:
                    row = lax.broadcasted_iota(jnp.int32, (rb, width), 0) + r0
                    col = lax.broadcasted_iota(jnp.int32, (rb, width), 1) + first
                    s = jnp.where(col <= row, s, NEG_INF)
                fold(h, r0, first, s)

    key_tiles = list(range(0, tq, tk))
    per_group = 8 if len(key_tiles) % 8 == 0 else 1
    groups = [key_tiles[i:i + per_group] for i in range(0, len(key_tiles), per_group)]
    group_scores = lambda keys: [[scores(h, key0) for h in range(2)] for key0 in keys]
    ahead = 3
    pending = [group_scores(keys) for keys in groups[:ahead]]
    for c, keys in enumerate(groups):
        s_cur = pending.pop(0)
        if c + ahead < len(groups):
            pending.append(group_scores(groups[c + ahead]))
        update(keys, s_cur)

    acc_a, acc_b = acc_ref[0], acc_ref[1]
    o = jnp.where(key_lane < MLA_V_DIM, acc_a / pltpu.roll(acc_a, MLA_V_DIM, 1),
                  acc_b / pltpu.roll(acc_b, MLA_V_DIM, 1))
    o_ref[0] = o.astype(BF16)


def _pair_prefix(k_meta, v_meta):
    n = k_meta.shape[0]
    lane = jnp.arange(QK_WIDTH) % PAIR_QK
    k_a = (lane < MLA_NOPE_DIM) | ((lane >= LANES) & (lane < LANES + EXTRA_GROUP))
    k_b = (((lane >= MLA_NOPE_DIM) & (lane < LANES))
           | ((lane >= LANES + EXTRA_GROUP) & (lane < LANES + 2 * EXTRA_GROUP)))
    v_a = lane < LANES
    zero_k, zero_v = jnp.zeros_like(k_meta), jnp.zeros_like(v_meta)
    pad = ((0, LANES - 2 * n), (0, 0))
    kp = jnp.concatenate([jnp.where(k_a, k_meta, zero_k), jnp.where(k_b, k_meta, zero_k)])
    vp = jnp.concatenate([jnp.where(v_a, v_meta, zero_v), jnp.where(v_a, zero_v, v_meta)])
    return jnp.pad(kp, pad), jnp.pad(vp, pad)


def _attn_call(q, k, v, prefix, tk, rb, pairs):
    nb, seq, _ = q.shape
    n_prefix = 0 if prefix is None else N_META
    wide = pl.BlockSpec((1, seq, pairs * PAIR_QK), lambda b, p: (b, 0, p))
    in_specs = [wide, wide, wide]
    args = [q, k, v]
    if n_prefix:
        in_specs += [pl.BlockSpec((LANES, pairs * PAIR_QK), lambda b, p: (0, p))] * 2
        args += list(_pair_prefix(*prefix))
    return pl.pallas_call(
        functools.partial(_attn_kernel, tk=tk, rb=rb, n_prefix=n_prefix, pairs=pairs),
        grid=(nb, N_PAIRS // pairs), in_specs=in_specs,
        out_specs=pl.BlockSpec((1, seq, pairs * PAIR_V), lambda b, p: (b, 0, p)),
        out_shape=jax.ShapeDtypeStruct((nb, seq, V_WIDTH), BF16),
        scratch_shapes=[pltpu.VMEM((2 * pairs, seq, LANES), F32),
                        pltpu.VMEM((2 * pairs, seq, PAIR_V), F32)],
        compiler_params=_params(),
        name="attn",
    )(*args)


def _merge_kernel(x_ref, o_ref, g_ref, lng_ref, lnb_ref, wbm_ref, wbf_ref, wout_ref, mixg_ref,
                  mixb_ref, h_ref):
    half = V_WIDTH // 2
    n_rows = x_ref.shape[1]
    sub = min(n_rows, SUB_ROWS)
    parts = [slice(r, r + sub) for r in range(0, n_rows, sub)]

    def branches(rows):
        return (_dot(o_ref[0, rows, :half], wbm_ref[...]),
                _dot(o_ref[0, rows, half:], wbf_ref[...]))

    def mix(rows, o_mla, o_fox):
        merged = (g_ref[0, rows, :D_MODEL].astype(F32) * o_mla
                  + g_ref[0, rows, D_MODEL:].astype(F32) * o_fox)
        return _dot(merged.astype(BF16), wout_ref[...])

    def finish(rows, mixed):
        h0 = _layer_norm(x_ref[0, rows, :], lng_ref[...], lnb_ref[...])
        h_ref[0, rows, :] = _layer_norm(DN_ALPHA * h0 + mixed, mixg_ref[...], mixb_ref[...])

    projected = [branches(rows) for rows in parts]
    mixed = [mix(rows, *pr) for rows, pr in zip(parts, projected)]
    for rows, mx in zip(parts, mixed):
        finish(rows, mx)


def _merge_call(x3, o, g, w, tm):
    nb, seq, _ = x3.shape
    row_blk = lambda width: pl.BlockSpec((1, tm, width), lambda b, j: (b, j, 0))
    consts = [w["ln_emb_g"], w["ln_emb_b"], w["w_bm"], w["w_bf"], w["w_out"], w["ln_mix_g"],
              w["ln_mix_b"]]
    return pl.pallas_call(
        _merge_kernel, grid=(nb, seq // tm),
        in_specs=[row_blk(D_MODEL), row_blk(V_WIDTH), row_blk(2 * D_MODEL)]
        + [_resident(a.shape) for a in consts],
        out_specs=row_blk(D_MODEL), out_shape=jax.ShapeDtypeStruct((nb, seq, D_MODEL), F32),
        compiler_params=_params(), name="merge",
    )(x3, o, g, *consts)


def _ffn_kernel(h_ref, halo0_ref, wup_ref, cw_ref, cb_ref, wdown_ref, lng_ref, lnb_ref,
                out_ref, halo_ref, gate_ref, act_ref):
    @pl.when(pl.program_id(1) == 0)
    def _():
        halo_ref[...] = halo0_ref[...]

    n_rows = h_ref.shape[1]
    sub = min(n_rows, SUB_ROWS)

    def gated(r0):
        rows = slice(r0, r0 + sub)
        hb = h_ref[0, rows, :].astype(BF16)
        for c in range(N_FF_CHUNKS):
            sl = slice(c * FF_CHUNK, (c + 1) * FF_CHUNK)
            gate = _dot(hb, wup_ref[:, sl])
            val = _dot(hb, wup_ref[:, D_FF + c * FF_CHUNK:D_FF + (c + 1) * FF_CHUNK])
            gate_ref[c, :HALO_ROWS, :] = halo_ref[:, sl]
            gate_ref[c, HALO_ROWS:, :] = gate
            halo_ref[:, sl] = gate[sub - HALO_ROWS:, :]
            back1 = gate_ref[c, HALO_ROWS - 1:HALO_ROWS - 1 + sub, :]
            back2 = gate_ref[c, HALO_ROWS - 2:HALO_ROWS - 2 + sub, :]
            conv = (cw_ref[0:1, sl] * back2 + cw_ref[1:2, sl] * back1 + cw_ref[2:3, sl] * gate
                    + cb_ref[:, sl])
            act_ref[rows, sl] = (conv * (1.0 / (1.0 + jnp.exp(-conv))) * val).astype(BF16)

    starts = list(range(0, n_rows, sub))
    for r0 in starts:
        gated(r0)
    down = [_dot(act_ref[r0:r0 + sub, :], wdown_ref[...]) for r0 in starts]
    for r0, ffn in zip(starts, down):
        rows = slice(r0, r0 + sub)
        out_ref[0, rows, :] = _layer_norm(DN_ALPHA * h_ref[0, rows, :] + ffn, lng_ref[...],
                                          lnb_ref[...])


def _ffn_call(h1, halo0, w, tm):
    nb, seq, _ = h1.shape
    row_blk = pl.BlockSpec((1, tm, D_MODEL), lambda b, j: (b, j, 0))
    consts = [halo0, w["w_up"], w["conv_w"], w["conv_b"], w["w_down"], w["ln_ffn_g"],
              w["ln_ffn_b"]]
    return pl.pallas_call(
        _ffn_kernel, grid=(nb, seq // tm),
        in_specs=[row_blk] + [_resident(a.shape) for a in consts],
        out_specs=row_blk, out_shape=jax.ShapeDtypeStruct((nb, seq, D_MODEL), F32),
        scratch_shapes=[pltpu.VMEM((HALO_ROWS, D_FF), F32),
                        pltpu.VMEM((N_FF_CHUNKS, HALO_ROWS + min(tm, SUB_ROWS), FF_CHUNK), F32),
                        pltpu.VMEM((tm, D_FF), BF16)],
        compiler_params=_params(), name="ffn",
    )(h1, *consts)


def _gate_tail_kernel(h_ref, wup_ref, tail_ref):
    gate = _dot(h_ref[0].astype(BF16), wup_ref[...])
    tail_ref[...] = gate[gate.shape[0] - HALO_ROWS:, :]


def _gate_tail_call(h1, w_up):
    return pl.pallas_call(
        _gate_tail_kernel, grid=(1,),
        in_specs=[pl.BlockSpec(h1.shape, lambda i: (0, 0, 0)),
                  pl.BlockSpec((D_MODEL, D_FF), lambda i: (0, 0))],
        out_specs=pl.BlockSpec((HALO_ROWS, D_FF), lambda i: (0, 0)),
        out_shape=jax.ShapeDtypeStruct((HALO_ROWS, D_FF), F32),
        compiler_params=pltpu.CompilerParams(dimension_semantics=("arbitrary",),
                                             vmem_limit_bytes=VMEM_LIMIT),
        name="gate_tail",
    )(h1, w_up)


def _rope_tables(length):
    half = MLA_ROPE_DIM // 2
    inv_freq = ROPE_THETA ** (-jnp.arange(half, dtype=F32) / half)
    ang = jnp.arange(length).astype(F32)[:, None] * inv_freq[None, :]
    cos, sin = jnp.cos(ang), jnp.sin(ang)
    zero = jnp.zeros_like(cos)
    widen = lambda a, b: jnp.pad(jnp.tile(jnp.concatenate([a, b], axis=1), (1, 2)),
                                 ((0, 0), (0, LANES - 2 * MLA_ROPE_DIM)))
    return widen(cos, cos), widen(zero, sin), widen(-sin, zero)


def _bias_selectors():
    half = (N_PAIRS // 2) * LANES
    sel = np.zeros((LANES, 2 * half), np.float32)
    const = np.zeros((1, 2 * half), np.float32)
    for h in range(FOX_HEADS):
        base = LANES * (h // 2) + EXTRA_GROUP * (h % 2)
        for piece in range(3):
            sel[piece * FOX_HEADS + h, base + piece] = 1.0
            const[0, base + 3 + piece] = 1.0
            const[0, half + base + piece] = 1.0
            sel[piece * FOX_HEADS + h, half + base + 3 + piece] = -1.0
    return jnp.asarray(sel, BF16), jnp.asarray(const)


def _w_in_kernel(w_ref, o_ref):
    o_krope = MLA_Q_RANK + MLA_KV_RANK
    o_fq = o_krope + MLA_ROPE_DIM
    o_flogit = o_fq + 3 * FOX_WIDTH
    o_gate = o_flogit + FOX_HEADS
    cols = w_ref.shape[1]

    def put(dst, rows):
        o_ref[:, dst:dst + rows.shape[0]] = rows.T.astype(BF16)

    step = 512
    for src0, dst0, n in ((0, C_QLAT, o_krope), (o_fq, C_FQ, 3 * FOX_WIDTH),
                          (o_gate, C_GATE, 2 * D_MODEL)):
        for r in range(0, n, step):
            put(dst0 + r, w_ref[src0 + r:src0 + r + step, :])
    k_rope = w_ref[o_krope:o_fq, :]
    put(C_KROPE, jnp.concatenate(
        [k_rope, k_rope, jnp.zeros((LANES - 2 * MLA_ROPE_DIM, cols), F32)], axis=0))
    put(C_FLOGIT, jnp.concatenate(
        [w_ref[o_flogit:o_gate, :], jnp.zeros((LANES - FOX_HEADS, cols), F32)], axis=0))


def _regroup_w_in(w_in_t):
    cols = 256
    return pl.pallas_call(
        _w_in_kernel, grid=(D_MODEL // cols,),
        in_specs=[pl.BlockSpec((w_in_t.shape[0], cols), lambda i: (0, i))],
        out_specs=pl.BlockSpec((cols, C_TOTAL), lambda i: (i, 0)),
        out_shape=jax.ShapeDtypeStruct((D_MODEL, C_TOTAL), BF16),
        compiler_params=pltpu.CompilerParams(dimension_semantics=("arbitrary",),
                                             vmem_limit_bytes=VMEM_LIMIT),
        name="w_in_regroup",
    )(w_in_t)


def _prepare_weights(w_in, b_gate, b_forget, q_norm_g, w_q_up, kv_norm_g, w_kv_up, w_branch_mla,
                     w_branch_fox, w_out, w_ffn_up, conv_w, conv_b, w_ffn_down):
    n_half = N_PAIRS // 2
    w_in_p = _regroup_w_in(w_in.T)
    wq = w_q_up.reshape(MLA_Q_RANK, n_half, 2, MLA_QK_DIM)
    wq_p = jnp.concatenate([
        wq[..., :MLA_NOPE_DIM].reshape(MLA_Q_RANK, n_half, LANES),
        wq[..., MLA_NOPE_DIM:].reshape(MLA_Q_RANK, n_half, 2 * MLA_ROPE_DIM),
        jnp.zeros((MLA_Q_RANK, n_half, LANES - 2 * MLA_ROPE_DIM), F32)], axis=-1)
    wkv = w_kv_up.reshape(MLA_KV_RANK, MLA_HEADS, MLA_NOPE_DIM + MLA_V_DIM)
    return {
        "w_in": w_in_p,
        "b_gate": b_gate.reshape(1, -1),
        "b_forget": jnp.pad(b_forget.reshape(1, -1), ((0, 0), (0, LANES - FOX_HEADS))),
        "q_norm_g": q_norm_g.reshape(1, -1),
        "w_q": wq_p.reshape(MLA_Q_RANK, n_half * PAIR_QK).astype(BF16),
        "kv_norm_g": kv_norm_g.reshape(1, -1),
        "w_k": wkv[..., :MLA_NOPE_DIM].reshape(MLA_KV_RANK, -1).astype(BF16),
        "w_v": wkv[..., MLA_NOPE_DIM:].reshape(MLA_KV_RANK, -1).astype(BF16),
        "w_bm": w_branch_mla.astype(BF16),
        "w_bf": w_branch_fox.astype(BF16),
        "w_out": w_out.astype(BF16),
        "w_up": w_ffn_up.astype(BF16),
        "conv_w": conv_w,
        "conv_b": conv_b.reshape(1, -1),
        "w_down": w_ffn_down.astype(BF16),
    }


def _layer(x3, tabs, cum0, halo0, prefix, w):
    seq = x3.shape[1]
    is_prefix = halo0 is None
    q, k, v, gates, cum_end = _proj_call(x3, tabs, cum0, w, min(PROJ_ROWS, seq))
    o = _attn_call(q, k, v, prefix, min(ATTN_KEYS, seq), min(ATTN_ROWS, seq),
                   N_PAIRS if is_prefix else ATTN_PAIRS)
    h1 = _merge_call(x3, o, gates, w, min(WIDE_ROWS, seq))
    if is_prefix:
        return None, (k[0], v[0]), cum_end[0], _gate_tail_call(h1, w["w_up"])
    return _ffn_call(h1, halo0, w, min(WIDE_ROWS, seq)), None, None, None


def kernel(x, meta_tokens, ln_emb_g, ln_emb_b, w_in, b_gate, b_forget, q_norm_g, w_q_up, kv_norm_g, w_kv_up, w_branch_mla, w_branch_fox, w_out, ln_mix_g, ln_mix_b, w_ffn_up, conv_w, conv_b, w_ffn_down, ln_ffn_g, ln_ffn_b):
    seq = x.shape[1]
    w = _prepare_weights(w_in[0], b_gate[0], b_forget[0], q_norm_g[0], w_q_up[0], kv_norm_g[0],
                         w_kv_up[0], w_branch_mla[0], w_branch_fox[0], w_out[0], w_ffn_up[0],
                         conv_w[0], conv_b[0], w_ffn_down[0])
    w.update(ln_emb_g=ln_emb_g.reshape(1, -1), ln_emb_b=ln_emb_b.reshape(1, -1),
             ln_mix_g=ln_mix_g[0].reshape(1, -1), ln_mix_b=ln_mix_b[0].reshape(1, -1),
             ln_ffn_g=ln_ffn_g[0].reshape(1, -1), ln_ffn_b=ln_ffn_b[0].reshape(1, -1))
    w["sel"], w["sel_const"] = _bias_selectors()
    tabs = _rope_tables(N_META + seq)
    meta_tabs = tuple(t[:N_META] for t in tabs)
    real_tabs = tuple(t[N_META:] for t in tabs)

    _, meta_kv, meta_cum, meta_tail = _layer(
        meta_tokens[None].astype(x.dtype), meta_tabs, jnp.zeros((HALO_ROWS, LANES), F32),
        None, None, w)
    out, _, _, _ = _layer(x, real_tabs, meta_cum, meta_tail, meta_kv, w)
    return out
```
